```python
import math
import jax, jax.numpy as jnp
from jax import lax
import numpy as np

D_MODEL = 2048
BATCH = 1
SEQ = 8192
DEPTH = 1
DEC_BATCH = 128
DEC_SEQ = 1
PAST_LEN = 2048
PAGE_SIZE = 128

HEAD_DIM = 128
ATT_HEADS = D_MODEL // 256
ATT_WIDTH = ATT_HEADS * HEAD_DIM
MLSTM_HEADS = D_MODEL // 256
MLSTM_DV = HEAD_DIM
MLSTM_DK = MLSTM_DV // 2
MLSTM_WIDTH = MLSTM_HEADS * MLSTM_DV
MLSTM_QK_WIDTH = MLSTM_HEADS * MLSTM_DK
MIX_WIDTH = ATT_WIDTH + MLSTM_WIDTH
DILATED_PATTERNS = ((128, 1), (512, 4), (2048, 16))
WIN_MAX = 2048
ATT_BLOCK = 128
REL_BUCKETS = 32
REL_MAX_DIST = WIN_MAX
MLSTM_CHUNK = 128
N_MEM = 256
XATT_HEADS = 4
XATT_WIDTH = XATT_HEADS * HEAD_DIM
D_FF = 4 * D_MODEL
EPS = 1e-6
IN_SPLIT = (ATT_WIDTH, 2 * ATT_WIDTH, 3 * ATT_WIDTH, 3 * ATT_WIDTH + MLSTM_QK_WIDTH, 3 * ATT_WIDTH + 2 * MLSTM_QK_WIDTH, 3 * ATT_WIDTH + 2 * MLSTM_QK_WIDTH + MLSTM_WIDTH, 3 * ATT_WIDTH + 2 * MLSTM_QK_WIDTH + 2 * MLSTM_WIDTH)
IN_WIDTH = 3 * ATT_WIDTH + 2 * MLSTM_QK_WIDTH + 2 * MLSTM_WIDTH + 2 * MLSTM_HEADS

kernel_name = 'hybrid_dilated_swa_mlstm_decoder_step'


def rmsnorm(x, g):
    xf = x.astype(jnp.float32)
    y = xf * lax.rsqrt(jnp.mean(xf * xf, axis=-1, keepdims=True) + EPS)
    return (y * g.astype(jnp.float32)).astype(x.dtype)


def rel_bucket(dist):
    dist = np.asarray(dist)
    exact = REL_BUCKETS // 2
    far = exact + (np.log(np.maximum(dist, exact) / exact) / math.log(REL_MAX_DIST / exact) * (REL_BUCKETS - exact)).astype(np.int32)
    return np.where(dist < exact, dist, np.minimum(far, REL_BUCKETS - 1)).astype(np.int32)


def dilated_attn_prompt(q, k, v, rel_bias, window, dilation):
    B, S, H, D = q.shape
    L = S // dilation
    nb = -(-L // ATT_BLOCK)
    Lp = nb * ATT_BLOCK

    def to_sub(t):
        t = t.reshape(B, L, dilation, H, D).transpose(0, 2, 1, 3, 4)
        t = jnp.pad(t, ((0, 0), (0, 0), (0, Lp - L), (0, 0), (0, 0)))
        return t.reshape(B, dilation, nb, ATT_BLOCK, H, D)

    def with_prev(t):
        prev = jnp.pad(t, ((0, 0), (0, 0), (1, 0), (0, 0), (0, 0), (0, 0)))[:, :, :nb]
        return jnp.concatenate([prev, t], axis=3)

    qb = to_sub(q)
    kk = with_prev(to_sub(k))
    vv = with_prev(to_sub(v))
    qi = np.arange(ATT_BLOCK)[:, None]
    ki = np.arange(2 * ATT_BLOCK)[None, :]
    dist_sub = ATT_BLOCK + qi - ki
    band = (dist_sub >= 0) & (dist_sub <= window // dilation)
    valid = band[None] & ((np.arange(nb)[:, None, None] > 0) | (ki >= ATT_BLOCK)[None])
    bias = rel_bias[rel_bucket(np.where(band, dist_sub, 0) * dilation)].transpose(2, 0, 1)
    s = jnp.einsum('bgnqhd,bgnkhd->bgnhqk', qb, kk, preferred_element_type=jnp.float32) * (D ** -0.5)
    s = s + bias[None, None, None].astype(jnp.float32)
    s = jnp.where(valid[None, None, :, None], s, -jnp.inf)
    m = jnp.max(s, axis=-1, keepdims=True)
    p = jnp.exp(s - m)
    l = jnp.sum(p, axis=-1, keepdims=True)
    o = jnp.einsum('bgnhqk,bgnkhd->bgnqhd', (p / l).astype(v.dtype), vv)
    lse = (m + jnp.log(l))[..., 0]
    o = o.reshape(B, dilation, Lp, H, D)[:, :, :L].transpose(0, 2, 1, 3, 4).reshape(B, S, H, D)
    lse = lse.transpose(0, 1, 2, 4, 3).reshape(B, dilation, Lp, H)[:, :, :L].transpose(0, 2, 1, 3).reshape(B, S, H)
    return o, lse


def dilated_attn_sample(q, k_new, v_new, k_buf, v_buf, rel_bias, window, dilation):
    T = q.shape[1]
    W = k_buf.shape[1]
    D = q.shape[-1]
    J = window // dilation + 1
    j = np.arange(J)
    idx = W + np.arange(T)[:, None] - j[None, :] * dilation
    valid = idx >= 0
    in_buf = (idx < W)[None, :, :, None, None]
    ib = np.clip(idx, 0, W - 1)
    inew = np.clip(idx - W, 0, T - 1)
    kg = jnp.where(in_buf, k_buf[:, ib], k_new[:, inew])
    vg = jnp.where(in_buf, v_buf[:, ib], v_new[:, inew])
    bias = rel_bias[rel_bucket(j * dilation)].T
    s = jnp.einsum('bthd,btjhd->bhtj', q, kg, preferred_element_type=jnp.float32) * (D ** -0.5)
    s = s + bias[None, :, None, :].astype(jnp.float32)
    s = jnp.where(valid[None, None], s, -jnp.inf)
    m = jnp.max(s, axis=-1, keepdims=True)
    p = jnp.exp(s - m)
    l = jnp.sum(p, axis=-1, keepdims=True)
    o = jnp.einsum('bhtj,btjhd->bthd', (p / l).astype(v_new.dtype), vg)
    lse = (m + jnp.log(l))[..., 0].transpose(0, 2, 1)
    return o, lse


def attend_dilated(q, k, v, rel_bias, k_buf=None, v_buf=None):
    outs, lses = [], []
    for window, dilation in DILATED_PATTERNS:
        if k_buf is None:
            o, l = dilated_attn_prompt(q, k, v, rel_bias, window, dilation)
        else:
            o, l = dilated_attn_sample(q, k, v, k_buf, v_buf, rel_bias, window, dilation)
        outs.append(o)
        lses.append(l)
    w = jax.nn.softmax(jnp.stack(lses, axis=0), axis=0)
    return jnp.einsum('pbth,pbthd->bthd', w, jnp.stack(outs, axis=0).astype(jnp.float32))


def mlstm_chunkwise(q, k, v, li, lf, C0, n0, m0):
    B, S, H, DK = q.shape
    DV = v.shape[-1]
    c = math.gcd(S, MLSTM_CHUNK)
    nc = S // c

    def chunks(t):
        return jnp.moveaxis(t.astype(jnp.float32).reshape((B, nc, c) + t.shape[2:]), 1, 0)

    causal = np.tril(np.ones((c, c), dtype=bool))

    def step(carry, inp):
        C, n, m = carry
        qc, kc, vc, lic, lfc = inp
        qc = qc * (DK ** -0.5)
        b = jnp.cumsum(lfc, axis=1).transpose(0, 2, 1)
        ig = lic.transpose(0, 2, 1)
        dmat = jnp.where(causal, b[..., :, None] - b[..., None, :] + ig[..., None, :], -jnp.inf)
        inter = b + m[..., None]
        m_t = jnp.maximum(inter, jnp.max(dmat, axis=-1))
        w_intra = jnp.exp(dmat - m_t[..., None])
        w_prev = jnp.exp(inter - m_t)
        sqk = jnp.einsum('bqhd,bkhd->bhqk', qc, kc) * w_intra
        num = jnp.einsum('bhqk,bkhv->bqhv', sqk, vc) + jnp.einsum('bhq,bqhd,bhdv->bqhv', w_prev, qc, C)
        den = jnp.sum(sqk, axis=-1) + w_prev * jnp.einsum('bqhd,bhd->bhq', qc, n)
        h = num / jnp.maximum(jnp.abs(den), jnp.exp(-m_t)).transpose(0, 2, 1)[..., None]
        b_last = b[..., -1]
        m_new = m_t[..., -1]
        w_k = jnp.exp(b_last[..., None] - b + ig - m_new[..., None])
        decay = jnp.exp(b_last + m - m_new)
        C_new = decay[..., None, None] * C + jnp.einsum('bhk,bkhd,bkhv->bhdv', w_k, kc, vc)
        n_new = decay[..., None] * n + jnp.einsum('bhk,bkhd->bhd', w_k, kc)
        return (C_new, n_new, m_new), h

    init = (C0.astype(jnp.float32), n0.astype(jnp.float32), m0.astype(jnp.float32))
    state, h = lax.scan(step, init, (chunks(q), chunks(k), chunks(v), chunks(li), chunks(lf)))
    h = jnp.moveaxis(h, 0, 1).reshape(B, S, H, DV)
    return h, state


def cross_attention(u, mem_k, mem_v, w_xq, w_xo):
    B, T, _ = u.shape
    q = (u @ w_xq).reshape(B, T, XATT_HEADS, HEAD_DIM)
    s = jnp.einsum('bthd,bmhd->bhtm', q, mem_k, preferred_element_type=jnp.float32) * (HEAD_DIM ** -0.5)
    p = jax.nn.softmax(s, axis=-1)
    o = jnp.einsum('bhtm,bmhd->bthd', p.astype(mem_v.dtype), mem_v).reshape(B, T, XATT_WIDTH)
    return o @ w_xo


def decoder_layer(x, mem_k, mem_v, attend, mstate, g_pre_mix, g_post_mix, w_in, b_gate, g_mlstm, w_out,
                  g_pre_xatt, g_post_xatt, w_xq, w_xo, g_pre_mlp, g_post_mlp, w_up, w_down):
    B, T, _ = x.shape
    u = rmsnorm(x, g_pre_mix)
    z = u @ w_in
    aq, ak, av, mq, mk, mv, mo, gates = jnp.split(z, IN_SPLIT, axis=-1)
    aq = aq.reshape(B, T, ATT_HEADS, HEAD_DIM)
    ak = ak.reshape(B, T, ATT_HEADS, HEAD_DIM)
    av = av.reshape(B, T, ATT_HEADS, HEAD_DIM)
    att = attend(aq, ak, av).reshape(B, T, ATT_WIDTH).astype(x.dtype)
    gates = gates.astype(jnp.float32) + b_gate.astype(jnp.float32)
    li = gates[..., :MLSTM_HEADS]
    lf = jax.nn.log_sigmoid(gates[..., MLSTM_HEADS:])
    h, new_mstate = mlstm_chunkwise(mq.reshape(B, T, MLSTM_HEADS, MLSTM_DK), mk.reshape(B, T, MLSTM_HEADS, MLSTM_DK),
                                    mv.reshape(B, T, MLSTM_HEADS, MLSTM_DV), li, lf, *mstate)
    hn = h * lax.rsqrt(jnp.mean(h * h, axis=-1, keepdims=True) + EPS)
    m_out = (hn.reshape(B, T, MLSTM_WIDTH) * g_mlstm.astype(jnp.float32) * jax.nn.sigmoid(mo.astype(jnp.float32))).astype(x.dtype)
    mix = jnp.concatenate([att, m_out], axis=-1) @ w_out
    x = x + rmsnorm(mix, g_post_mix)
    c = cross_attention(rmsnorm(x, g_pre_xatt), mem_k, mem_v, w_xq, w_xo)
    x = x + rmsnorm(c, g_post_xatt)
    f = jnp.square(jax.nn.relu(rmsnorm(x, g_pre_mlp) @ w_up)) @ w_down
    x = x + rmsnorm(f, g_post_mlp)
    return x, ak, av, new_mstate


def setup_inputs(seed: int = 0) -> dict:
    key = jax.random.key(seed)
    ks = jax.random.split(key, 32)
    f32 = jnp.float32
    win_buf = min(WIN_MAX, PAST_LEN)

    def nrm(k, shape, scale):
        return scale * jax.random.normal(k, shape, f32)

    def gain(k, shape):
        return 1.0 + 0.02 * jax.random.normal(k, shape, f32)

    b_i = nrm(ks[25], (DEPTH, MLSTM_HEADS), 0.1)
    b_f = jnp.linspace(3.0, 6.0, MLSTM_HEADS, dtype=f32)[None, :] + nrm(ks[26], (DEPTH, MLSTM_HEADS), 0.1)
    return {
        'x_prompt': nrm(ks[0], (BATCH, SEQ, D_MODEL), 1.0),
        'x_sample': nrm(ks[1], (DEC_BATCH, DEC_SEQ, D_MODEL), 1.0),
        'mem_prompt': nrm(ks[2], (BATCH, N_MEM, D_MODEL), 1.0),
        'cache_win_k': nrm(ks[3], (DEPTH, DEC_BATCH, win_buf, ATT_HEADS, HEAD_DIM), 1.0),
        'cache_win_v': nrm(ks[4], (DEPTH, DEC_BATCH, win_buf, ATT_HEADS, HEAD_DIM), 1.0),
        'state_mlstm_C': nrm(ks[5], (DEPTH, DEC_BATCH, MLSTM_HEADS, MLSTM_DK, MLSTM_DV), 0.5),
        'state_mlstm_n': nrm(ks[6], (DEPTH, DEC_BATCH, MLSTM_HEADS, MLSTM_DK), 0.5),
        'state_mlstm_m': nrm(ks[7], (DEPTH, DEC_BATCH, MLSTM_HEADS), 1.0),
        'cache_mem_k': nrm(ks[8], (DEPTH, DEC_BATCH, N_MEM, XATT_HEADS, HEAD_DIM), 1.0),
        'cache_mem_v': nrm(ks[9], (DEPTH, DEC_BATCH, N_MEM, XATT_HEADS, HEAD_DIM), 1.0),
        'rel_bias': nrm(ks[10], (REL_BUCKETS, ATT_HEADS), 0.5),
        'g_pre_mix': gain(ks[11], (DEPTH, D_MODEL)),
        'g_post_mix': gain(ks[12], (DEPTH, D_MODEL)),
        'w_in': nrm(ks[13], (DEPTH, D_MODEL, IN_WIDTH), D_MODEL ** -0.5),
        'b_gate': jnp.concatenate([b_i, b_f], axis=-1),
        'g_mlstm': gain(ks[14], (DEPTH, MLSTM_WIDTH)),
        'w_out': nrm(ks[15], (DEPTH, MIX_WIDTH, D_MODEL), MIX_WIDTH ** -0.5),
        'g_pre_xatt': gain(ks[16], (DEPTH, D_MODEL)),
        'g_post_xatt': gain(ks[17], (DEPTH, D_MODEL)),
        'g_mem': gain(ks[18], (DEPTH, D_MODEL)),
        'w_xq': nrm(ks[19], (DEPTH, D_MODEL, XATT_WIDTH), D_MODEL ** -0.5),
        'w_xkv': nrm(ks[20], (DEPTH, D_MODEL, 2 * XATT_WIDTH), D_MODEL ** -0.5),
        'w_xo': nrm(ks[21], (DEPTH, XATT_WIDTH, D_MODEL), XATT_WIDTH ** -0.5),
        'g_pre_mlp': gain(ks[22], (DEPTH, D_MODEL)),
        'g_post_mlp': gain(ks[23], (DEPTH, D_MODEL)),
        'w_up': nrm(ks[24], (DEPTH, D_MODEL, D_FF), D_MODEL ** -0.5),
        'w_down': nrm(ks[27], (DEPTH, D_FF, D_MODEL), D_FF ** -0.5),
    }


def reference(x_prompt, x_sample, mem_prompt, cache_win_k, cache_win_v, state_mlstm_C, state_mlstm_n, state_mlstm_m,
              cache_mem_k, cache_mem_v, rel_bias, g_pre_mix, g_post_mix, w_in, b_gate, g_mlstm, w_out,
              g_pre_xatt, g_post_xatt, g_mem, w_xq, w_xkv, w_xo, g_pre_mlp, g_post_mlp, w_up, w_down):
    B, S, _ = x_prompt.shape
    keep = min(WIN_MAX, S)
    yp, ys = x_prompt, x_sample
    wkp, wvp, Cp_l, np_l, mp_l, mkp, mvp = [], [], [], [], [], [], []
    wks, wvs, Cs_l, ns_l, ms_l = [], [], [], [], []
    for i in range(DEPTH):
        lw = dict(g_pre_mix=g_pre_mix[i], g_post_mix=g_post_mix[i], w_in=w_in[i], b_gate=b_gate[i], g_mlstm=g_mlstm[i],
                  w_out=w_out[i], g_pre_xatt=g_pre_xatt[i], g_post_xatt=g_post_xatt[i], w_xq=w_xq[i], w_xo=w_xo[i],
                  g_pre_mlp=g_pre_mlp[i], g_post_mlp=g_post_mlp[i], w_up=w_up[i], w_down=w_down[i])
        mem_kv = rmsnorm(mem_prompt, g_mem[i]) @ w_xkv[i]
        mk_p = mem_kv[..., :XATT_WIDTH].reshape(B, N_MEM, XATT_HEADS, HEAD_DIM)
        mv_p = mem_kv[..., XATT_WIDTH:].reshape(B, N_MEM, XATT_HEADS, HEAD_DIM)
        st0 = (jnp.zeros((B, MLSTM_HEADS, MLSTM_DK, MLSTM_DV), jnp.float32),
               jnp.zeros((B, MLSTM_HEADS, MLSTM_DK), jnp.float32),
               jnp.zeros((B, MLSTM_HEADS), jnp.float32))
        yp, ak, av, (Cp, n_p, m_p) = decoder_layer(
            yp, mk_p, mv_p, lambda q, k, v: attend_dilated(q, k, v, rel_bias), st0, **lw)
        ys, sk, sv, (Cs, n_s, m_s) = decoder_layer(
            ys, cache_mem_k[i], cache_mem_v[i],
            lambda q, k, v: attend_dilated(q, k, v, rel_bias, cache_win_k[i], cache_win_v[i]),
            (state_mlstm_C[i], state_mlstm_n[i], state_mlstm_m[i]), **lw)
        wkp.append(ak[:, S - keep:])
        wvp.append(av[:, S - keep:])
        Cp_l.append(Cp)
        np_l.append(n_p)
        mp_l.append(m_p)
        mkp.append(mk_p)
        mvp.append(mv_p)
        wks.append(sk)
        wvs.append(sv)
        Cs_l.append(Cs)
        ns_l.append(n_s)
        ms_l.append(m_s)
    return (yp, ys, jnp.stack(wkp), jnp.stack(wvp), jnp.stack(Cp_l), jnp.stack(np_l), jnp.stack(mp_l),
            jnp.stack(mkp), jnp.stack(mvp), jnp.stack(wks), jnp.stack(wvs), jnp.stack(Cs_l), jnp.stack(ns_l), jnp.stack(ms_l))
```

```python
import functools
import math

import numpy as np
import jax
import jax.numpy as jnp
from jax import lax
from jax.experimental import pallas as pl
from jax.experimental.pallas import tpu as pltpu

F32 = jnp.float32
BF16 = jnp.bfloat16

EPS = 1e-6
HEAD_DIM = 128
MLSTM_DK = 64
LANES = 128
DILATIONS = (1, 4, 16)
SUB_WINDOW = 128
ATT_BLOCK = 128
MLSTM_CHUNK = 128
REL_BUCKETS = 32
REL_MAX_DIST = 2048
NEG = -1e30
ATT_SCALE = HEAD_DIM ** -0.5
QK_SCALE = MLSTM_DK ** -0.5
VMEM_LIMIT = 56 * 2 ** 20


def _params(semantics):
    return pltpu.CompilerParams(dimension_semantics=semantics, vmem_limit_bytes=VMEM_LIMIT)


def _rms(xf, g):
    ms = jnp.mean(xf * xf, axis=-1, keepdims=True)
    return xf * lax.rsqrt(ms + EPS) * g


def _dot(a, b):
    return jnp.dot(a, b, preferred_element_type=F32)


def _dot_nt(a, b):
    return lax.dot_general(a, b, (((1,), (1,)), ((), ())), preferred_element_type=F32)


def _dot_tn(a, b):
    return lax.dot_general(a, b, (((0,), (0,)), ((), ())), preferred_element_type=F32)


def _log_sigmoid(x):
    return jnp.minimum(x, 0.0) - jnp.log(1.0 + jnp.exp(-jnp.abs(x)))


def _rel_bucket(dist):
    dist = np.asarray(dist)
    exact = REL_BUCKETS // 2
    far = exact + (np.log(np.maximum(dist, exact) / exact) / math.log(REL_MAX_DIST / exact)
                   * (REL_BUCKETS - exact)).astype(np.int32)
    return np.where(dist < exact, dist, np.minimum(far, REL_BUCKETS - 1)).astype(np.int32)


def _norm_matmul_kernel(x_ref, g_ref, w_ref, o_ref):
    xn = _rms(x_ref[...], g_ref[...]).astype(BF16)
    o_ref[...] = _dot(xn, w_ref[...])


def _norm_matmul(x, g, w):
    m, d = x.shape
    n = w.shape[1]
    return pl.pallas_call(
        _norm_matmul_kernel,
        out_shape=jax.ShapeDtypeStruct((m, n), F32),
        grid=(1,),
        in_specs=[pl.BlockSpec((m, d), lambda i: (0, 0)),
                  pl.BlockSpec((1, d), lambda i: (0, 0)),
                  pl.BlockSpec((d, n), lambda i: (0, 0))],
        out_specs=pl.BlockSpec((m, n), lambda i: (0, 0)),
        compiler_params=_params(("arbitrary",)),
        name="norm_matmul",
    )(x, g, w)


def _in_proj_kernel(x_ref, g_ref, w_ref, wg_ref, z_ref, kv_ref, gate_ref, xn_ref):
    j = pl.program_id(1)

    @pl.when(j == 0)
    def _():
        xn = _rms(x_ref[...], g_ref[...]).astype(BF16)
        xn_ref[...] = xn
        gate_ref[...] = _dot(xn, wg_ref[...])

    acc = _dot(xn_ref[...], w_ref[...])
    z_ref[...] = acc.astype(BF16)

    @pl.when((j == 1) | (j == 2))
    def _():
        kv_ref[...] = acc


def _in_proj(x, g, w6, wg, tm):
    m, d = x.shape
    tn = 1024
    nblk = w6.shape[1] // tn
    return pl.pallas_call(
        _in_proj_kernel,
        out_shape=(jax.ShapeDtypeStruct((nblk, m, tn), BF16),
                   jax.ShapeDtypeStruct((2, m, tn), F32),
                   jax.ShapeDtypeStruct((m, LANES), F32)),
        grid=(m // tm, nblk),
        in_specs=[pl.BlockSpec((tm, d), lambda i, j: (i, 0)),
                  pl.BlockSpec((1, d), lambda i, j: (0, 0)),
                  pl.BlockSpec((d, tn), lambda i, j: (0, j)),
                  pl.BlockSpec((d, LANES), lambda i, j: (0, 0))],
        out_specs=(pl.BlockSpec((None, tm, tn), lambda i, j: (j, i, 0)),
                   pl.BlockSpec((None, tm, tn), lambda i, j: (jnp.clip(j - 1, 0, 1), i, 0)),
                   pl.BlockSpec((tm, LANES), lambda i, j: (i, 0))),
        scratch_shapes=[pltpu.VMEM((tm, d), BF16)],
        compiler_params=_params(("parallel", "arbitrary")),
        name="in_proj",
    )(x, g, w6, wg)


def _dil_attn_kernel(heads, rb_ref, bkt_c_ref, bkt_p_ref, q_ref, kc_ref, kp_ref, vc_ref, vp_ref,
                     o_ref, lse_ref, bias_c, bias_p):
    first = (pl.program_id(0) == 0) & (pl.program_id(1) == 0)

    @pl.when(first)
    def _():
        for bkt_ref, dst in ((bkt_c_ref, bias_c), (bkt_p_ref, bias_p)):
            bkt = bkt_ref[...]
            for h in range(heads):
                acc = jnp.full(bkt.shape, NEG, F32)
                for k in range(REL_BUCKETS):
                    acc = jnp.where(bkt == k, rb_ref[k * heads + h], acc)
                dst[h] = acc

    no_prev = pl.program_id(1) == 0
    lse_ref[...] = jnp.zeros(lse_ref.shape, F32)
    for h in range(heads):
        sl = slice(h * HEAD_DIM, (h + 1) * HEAD_DIM)
        qh = q_ref[:, sl]
        sc = _dot_nt(qh, kc_ref[:, sl]) * ATT_SCALE + bias_c[h]
        sp = _dot_nt(qh, kp_ref[:, sl]) * ATT_SCALE + bias_p[h]
        sp = jnp.where(no_prev, NEG, sp)
        m = jnp.maximum(jnp.max(sc, axis=-1, keepdims=True), jnp.max(sp, axis=-1, keepdims=True))
        pc = jnp.exp(sc - m)
        pp = jnp.exp(sp - m)
        l = jnp.sum(pc, axis=-1, keepdims=True) + jnp.sum(pp, axis=-1, keepdims=True)
        o = _dot(pc.astype(BF16), vc_ref[:, sl]) + _dot(pp.astype(BF16), vp_ref[:, sl])
        o_ref[:, sl] = (o / l).astype(o_ref.dtype)
        lse_ref[:, h:h + 1] = m + jnp.log(l)


def _dil_attn(z6, rel_flat, dilation, heads):
    _, s, w = z6.shape
    d = dilation
    sub_len = s // d
    nb = sub_len // ATT_BLOCK
    qi = np.arange(ATT_BLOCK)[:, None]
    ki = np.arange(ATT_BLOCK)[None, :]
    dist_c = qi - ki
    dist_p = ATT_BLOCK + qi - ki
    bkt_c = np.where(dist_c >= 0, _rel_bucket(np.maximum(dist_c, 0) * d), -1).astype(np.int32)
    bkt_p = np.where(dist_p <= SUB_WINDOW, _rel_bucket(dist_p * d), -1).astype(np.int32)
    zv = z6.reshape(6, sub_len, d * w)
    blk = (None, ATT_BLOCK, w)
    const = lambda r, n: (0, 0)
    o, lse = pl.pallas_call(
        functools.partial(_dil_attn_kernel, heads),
        out_shape=(jax.ShapeDtypeStruct((sub_len, d * w), BF16),
                   jax.ShapeDtypeStruct((sub_len, d * LANES), F32)),
        grid=(d, nb),
        in_specs=[pl.BlockSpec(memory_space=pltpu.SMEM),
                  pl.BlockSpec((ATT_BLOCK, ATT_BLOCK), const),
                  pl.BlockSpec((ATT_BLOCK, ATT_BLOCK), const),
                  pl.BlockSpec(blk, lambda r, n: (0, n, r)),
                  pl.BlockSpec(blk, lambda r, n: (1, n, r)),
                  pl.BlockSpec(blk, lambda r, n: (1, jnp.maximum(n - 1, 0), r)),
                  pl.BlockSpec(blk, lambda r, n: (2, n, r)),
                  pl.BlockSpec(blk, lambda r, n: (2, jnp.maximum(n - 1, 0), r))],
        out_specs=(pl.BlockSpec((ATT_BLOCK, w), lambda r, n: (n, r)),
                   pl.BlockSpec((ATT_BLOCK, LANES), lambda r, n: (n, r))),
        scratch_shapes=[pltpu.VMEM((heads, ATT_BLOCK, ATT_BLOCK), F32),
                        pltpu.VMEM((heads, ATT_BLOCK, ATT_BLOCK), F32)],
        compiler_params=_params(("arbitrary", "arbitrary")),
        name=f"dil_attn_d{d}",
    )(rel_flat, jnp.asarray(bkt_c), jnp.asarray(bkt_p), zv, zv, zv, zv, zv)
    return o.reshape(s, w), lse.reshape(s, LANES)


def _mlstm_kernel(heads, qk_ref, v_ref, mo_ref, gn_ref, gt_ref, bgr_ref, bgc_ref, gm_ref,
                  c0_ref, n0_ref, m0_ref, out_ref, c_out, n_out, m_out, c_s, n_s, m_s):
    step = pl.program_id(0)
    t = MLSTM_CHUNK
    qw = heads * MLSTM_DK

    @pl.when(step == 0)
    def _():
        c_s[...] = c0_ref[...]
        n_s[...] = n0_ref[...]
        m_s[...] = m0_ref[...]

    row = lax.broadcasted_iota(jnp.int32, (t, t), 0)
    col = lax.broadcasted_iota(jnp.int32, (t, t), 1)
    causal = col <= row
    upto = row <= col
    lane = lax.broadcasted_iota(jnp.int32, (1, LANES), 1)
    subl = lax.broadcasted_iota(jnp.int32, (LANES, 1), 0)

    for pair in range(heads // 2):
        psl = slice(pair * LANES, (pair + 1) * LANES)
        q2 = qk_ref[:, psl]
        k2 = qk_ref[:, qw + pair * LANES: qw + (pair + 1) * LANES]
        c2 = c_s[psl, :]
        n2 = n_s[:, psl]
        d_c, d_n, decays = [], [], []
        for half in range(2):
            h = 2 * pair + half
            sl = slice(h * HEAD_DIM, (h + 1) * HEAD_DIM)
            mine = (lane >= half * MLSTM_DK) & (lane < (half + 1) * MLSTM_DK)
            li_row = gt_ref[h:h + 1, :] + bgc_ref[h:h + 1, :]
            lf_row = _log_sigmoid(gt_ref[heads + h:heads + h + 1, :] + bgc_ref[heads + h:heads + h + 1, :])
            li_col = gn_ref[:, h:h + 1] + bgr_ref[:, h:h + 1]
            lf_col = _log_sigmoid(gn_ref[:, heads + h:heads + h + 1] + bgr_ref[:, heads + h:heads + h + 1])
            b_col = jnp.sum(jnp.where(causal, lf_row, 0.0), axis=-1, keepdims=True)
            b_row = jnp.sum(jnp.where(upto, lf_col, 0.0), axis=0, keepdims=True)
            m_prev = m_s[h:h + 1, 0:1]
            cm = jnp.where(causal, li_row - b_row, NEG)
            a_col = jnp.maximum(jnp.max(cm, axis=-1, keepdims=True), m_prev)
            w_intra = jnp.exp(cm - a_col)
            w_prev = jnp.exp(m_prev - a_col)
            qm = jnp.where(mine, q2, 0.0) * QK_SCALE
            km = jnp.where(mine, k2, 0.0)
            sqk = _dot_nt(qm, km) * w_intra
            vh = v_ref[:, sl]
            num = _dot(sqk.astype(BF16), vh) + w_prev * _dot(qm, c2.astype(BF16))
            qn = jnp.sum(qm.astype(F32) * n2, axis=-1, keepdims=True)
            den = jnp.sum(sqk, axis=-1, keepdims=True) + w_prev * qn
            m_t = b_col + a_col
            hh = num / jnp.maximum(jnp.abs(den), jnp.exp(-m_t))
            hn = hh * lax.rsqrt(jnp.mean(hh * hh, axis=-1, keepdims=True) + EPS)
            gate = jax.nn.sigmoid(mo_ref[:, sl].astype(F32))
            out_ref[:, sl] = (hn * gm_ref[:, sl] * gate).astype(out_ref.dtype)
            a_last = a_col[t - 1:t, :]
            b_last = b_col[t - 1:t, :]
            m_new = b_last + a_last
            w_k = jnp.exp((li_col - b_col) - a_last)
            decay = jnp.exp(b_last + m_prev - m_new)
            kw = km.astype(F32) * w_k
            d_c.append(_dot_tn(kw.astype(BF16), vh))
            d_n.append(jnp.sum(kw, axis=0, keepdims=True))
            decays.append(decay)
            m_s[h:h + 1, :] = jnp.broadcast_to(m_new, (1, LANES))
        dec_col = jnp.where(subl < MLSTM_DK, decays[0], decays[1])
        dec_row = jnp.where(lane < MLSTM_DK, decays[0], decays[1])
        c_s[psl, :] = dec_col * c2 + d_c[0] + d_c[1]
        n_s[:, psl] = dec_row * n2 + d_n[0] + d_n[1]

    @pl.when(step == pl.num_programs(0) - 1)
    def _():
        c_out[...] = c_s[...]
        n_out[...] = n_s[...]
        m_out[...] = m_s[...]


def _mlstm_prompt(z6, gates, b_gate, g_mlstm, heads):
    _, s, w = z6.shape
    t = MLSTM_CHUNK
    gt = gates[:, :2 * heads].T
    bg_row = jnp.zeros((1, LANES), F32).at[0, :2 * heads].set(b_gate)
    bg_col = b_gate.reshape(2 * heads, 1)
    qw = heads * MLSTM_DK
    c0 = jnp.zeros((qw, HEAD_DIM), F32)
    n0 = jnp.zeros((1, qw), F32)
    m0 = jnp.zeros((heads, LANES), F32)
    const = lambda c: (0, 0)
    return pl.pallas_call(
        functools.partial(_mlstm_kernel, heads),
        out_shape=(jax.ShapeDtypeStruct((s, w), BF16),
                   jax.ShapeDtypeStruct((qw, HEAD_DIM), F32),
                   jax.ShapeDtypeStruct((1, qw), F32),
                   jax.ShapeDtypeStruct((heads, LANES), F32)),
        grid=(s // t,),
        in_specs=[pl.BlockSpec((None, t, w), lambda c: (3, c, 0)),
                  pl.BlockSpec((None, t, w), lambda c: (4, c, 0)),
                  pl.BlockSpec((None, t, w), lambda c: (5, c, 0)),
                  pl.BlockSpec((t, LANES), lambda c: (c, 0)),
                  pl.BlockSpec((2 * heads, t), lambda c: (0, c)),
                  pl.BlockSpec((1, LANES), const),
                  pl.BlockSpec((2 * heads, 1), const),
                  pl.BlockSpec((1, w), const),
                  pl.BlockSpec((qw, HEAD_DIM), const),
                  pl.BlockSpec((1, qw), const),
                  pl.BlockSpec((heads, LANES), const)],
        out_specs=(pl.BlockSpec((t, w), lambda c: (c, 0)),
                   pl.BlockSpec((qw, HEAD_DIM), const),
                   pl.BlockSpec((1, qw), const),
                   pl.BlockSpec((heads, LANES), const)),
        scratch_shapes=[pltpu.VMEM((qw, HEAD_DIM), F32),
                        pltpu.VMEM((1, qw), F32),
                        pltpu.VMEM((heads, LANES), F32)],
        compiler_params=_params(("arbitrary",)),
        name="mlstm_prompt",
    )(z6, z6, z6, gates, gt, bg_row, bg_col, g_mlstm.reshape(1, w), c0, n0, m0)


def _out_proj_kernel(merge, heads, *refs):
    if merge:
        (o1, o2, o3, l1, l2, l3, mo_ref, x_ref, wa_ref, wm_ref, gpost_ref, gpre_ref, wq_ref,
         x1_ref, qx_ref, att_s) = refs
        la, lb, lc = l1[...], l2[...], l3[...]
        mx = jnp.maximum(jnp.maximum(la, lb), lc)
        ea, eb, ec = jnp.exp(la - mx), jnp.exp(lb - mx), jnp.exp(lc - mx)
        tot = ea + eb + ec
        wa_, wb_, wc_ = ea / tot, eb / tot, ec / tot
        for h in range(heads):
            sl = slice(h * HEAD_DIM, (h + 1) * HEAD_DIM)
            mix = (wa_[:, h:h + 1] * o1[:, sl].astype(F32) + wb_[:, h:h + 1] * o2[:, sl].astype(F32)
                   + wc_[:, h:h + 1] * o3[:, sl].astype(F32))
            att_s[:, sl] = mix.astype(BF16)
        att = att_s[...]
    else:
        att_ref, mo_ref, x_ref, wa_ref, wm_ref, gpost_ref, gpre_ref, wq_ref, x1_ref, qx_ref = refs
        att = att_ref[...]
    y = _dot(att, wa_ref[...]) + _dot(mo_ref[...], wm_ref[...])
    x1 = x_ref[...] + _rms(y, gpost_ref[...])
    x1_ref[...] = x1
    u = _rms(x1, gpre_ref[...]).astype(BF16)
    qx_ref[...] = _dot(u, wq_ref[...]).astype(qx_ref.dtype)


def _out_proj(att_parts, m_out, x, w_att, w_ml, g_post, g_pre, w_xq, tm, heads):
    m, d = x.shape
    wa = w_att.shape[0]
    wm = w_ml.shape[0]
    nq = w_xq.shape[1]
    merge = len(att_parts) == 6
    row = lambda i: (i, 0)
    const = lambda i: (0, 0)
    part_specs = [pl.BlockSpec((tm, wa), row)] * (3 if merge else 1)
    if merge:
        part_specs += [pl.BlockSpec((tm, LANES), row)] * 3
    return pl.pallas_call(
        functools.partial(_out_proj_kernel, merge, heads),
        out_shape=(jax.ShapeDtypeStruct((m, d), F32), jax.ShapeDtypeStruct((m, nq), BF16)),
        grid=(m // tm,),
        in_specs=part_specs + [pl.BlockSpec((tm, wm), row),
                               pl.BlockSpec((tm, d), row),
                               pl.BlockSpec((wa, d), const),
                               pl.BlockSpec((wm, d), const),
                               pl.BlockSpec((1, d), const),
                               pl.BlockSpec((1, d), const),
                               pl.BlockSpec((d, nq), const)],
        out_specs=(pl.BlockSpec((tm, d), row), pl.BlockSpec((tm, nq), row)),
        scratch_shapes=[pltpu.VMEM((tm, wa), BF16)] if merge else [],
        compiler_params=_params(("parallel",)),
        name="out_proj_merge" if merge else "out_proj",
    )(*att_parts, m_out, x, w_att, w_ml, g_post, g_pre, w_xq)


def _xattn_kernel(heads, q_ref, mk_ref, mv_ref, o_ref):
    for h in range(heads):
        sl = slice(h * HEAD_DIM, (h + 1) * HEAD_DIM)
        s = _dot_nt(q_ref[:, sl], mk_ref[:, sl].astype(BF16)) * ATT_SCALE
        m = jnp.max(s, axis=-1, keepdims=True)
        p = jnp.exp(s - m)
        l = jnp.sum(p, axis=-1, keepdims=True)
        o = _dot(p.astype(BF16), mv_ref[:, sl].astype(BF16))
        o_ref[:, sl] = (o / l).astype(o_ref.dtype)


def _xattn(qx, mem_k, mem_v, tm, heads):
    m, w = qx.shape
    n_mem = mem_k.shape[0]
    return pl.pallas_call(
        functools.partial(_xattn_kernel, heads),
        out_shape=jax.ShapeDtypeStruct((m, w), BF16),
        grid=(m // tm,),
        in_specs=[pl.BlockSpec((tm, w), lambda i: (i, 0)),
                  pl.BlockSpec((n_mem, w), lambda i: (0, 0)),
                  pl.BlockSpec((n_mem, w), lambda i: (0, 0))],
        out_specs=pl.BlockSpec((tm, w), lambda i: (i, 0)),
        compiler_params=_params(("parallel",)),
        name="xattn_prompt",
    )(qx, mem_k, mem_v)


def _proj_post_kernel(a_ref, x_ref, w_ref, g_ref, y_ref):
    y_ref[...] = x_ref[...] + _rms(_dot(a_ref[...], w_ref[...]), g_ref[...])


def _proj_post(a, x, w, g, tm):
    m, d = x.shape
    k = a.shape[1]
    return pl.pallas_call(
        _proj_post_kernel,
        out_shape=jax.ShapeDtypeStruct((m, d), F32),
        grid=(m // tm,),
        in_specs=[pl.BlockSpec((tm, k), lambda i: (i, 0)),
                  pl.BlockSpec((tm, d), lambda i: (i, 0)),
                  pl.BlockSpec((k, d), lambda i: (0, 0)),
                  pl.BlockSpec((1, d), lambda i: (0, 0))],
        out_specs=pl.BlockSpec((tm, d), lambda i: (i, 0)),
        compiler_params=_params(("parallel",)),
        name="proj_post",
    )(a, x, w, g)


def _mlp_kernel(x_ref, gpre_ref, wu_ref, wd_ref, gpost_ref, y_ref, xn_s, acc_s):
    j = pl.program_id(1)

    @pl.when(j == 0)
    def _():
        xn_s[...] = _rms(x_ref[...], gpre_ref[...]).astype(BF16)
        acc_s[...] = jnp.zeros(acc_s.shape, F32)

    hidden = jnp.square(jnp.maximum(_dot(xn_s[...], wu_ref[...]), 0.0))
    acc_s[...] += _dot(hidden.astype(BF16), wd_ref[...])

    @pl.when(j == pl.num_programs(1) - 1)
    def _():
        y_ref[...] = x_ref[...] + _rms(acc_s[...], gpost_ref[...])


def _mlp(x, g_pre, w_up, w_down, g_post, tm, tf):
    m, d = x.shape
    ff = w_up.shape[1]
    return pl.pallas_call(
        _mlp_kernel,
        out_shape=jax.ShapeDtypeStruct((m, d), F32),
        grid=(m // tm, ff // tf),
        in_specs=[pl.BlockSpec((tm, d), lambda i, j: (i, 0)),
                  pl.BlockSpec((1, d), lambda i, j: (0, 0)),
                  pl.BlockSpec((d, tf), lambda i, j: (0, j)),
                  pl.BlockSpec((tf, d), lambda i, j: (j, 0)),
                  pl.BlockSpec((1, d), lambda i, j: (0, 0))],
        out_specs=pl.BlockSpec((tm, d), lambda i, j: (i, 0)),
        scratch_shapes=[pltpu.VMEM((tm, d), BF16), pltpu.VMEM((tm, d), F32)],
        compiler_params=_params(("parallel", "arbitrary")),
        name="mlp",
    )(x, g_pre, w_up, w_down, g_post)


def _head_indicator(width, transpose):
    if transpose:
        h = lax.broadcasted_iota(jnp.int32, (LANES, width), 0)
        c = lax.broadcasted_iota(jnp.int32, (LANES, width), 1)
    else:
        c = lax.broadcasted_iota(jnp.int32, (width, LANES), 0)
        h = lax.broadcasted_iota(jnp.int32, (width, LANES), 1)
    return jnp.where(c // HEAD_DIM == h, 1.0, 0.0).astype(BF16)


def _single_query_attention(q_row, k_groups, v_groups, bias_groups, new_kv, new_bias, new_count,
                            ind, ind_t, heads):
    width = q_row.shape[1]
    scores = []
    for kg, bg in zip(k_groups, bias_groups):
        s = _dot((kg * q_row).astype(BF16), ind) * ATT_SCALE
        scores.append(s if bg is None else s + bg)
    mx = jnp.max(scores[0], axis=0, keepdims=True)
    for s in scores[1:]:
        mx = jnp.maximum(mx, jnp.max(s, axis=0, keepdims=True))
    if new_kv is not None:
        k_new, v_new = new_kv
        prod = jnp.broadcast_to(k_new * q_row, (8, width)).astype(BF16)
        s_new = _dot(prod, ind)[0:1, :] * ATT_SCALE + new_bias
        mx = jnp.maximum(mx, s_new)
    den = jnp.zeros((1, LANES), F32)
    acc = jnp.zeros((1, width), F32)
    for s, vg in zip(scores, v_groups):
        p = jnp.exp(s - mx)
        den = den + jnp.sum(p, axis=0, keepdims=True)
        acc = acc + jnp.sum(_dot(p.astype(BF16), ind_t) * vg, axis=0, keepdims=True)
    if new_kv is not None:
        p_new = jnp.exp(s_new - mx) * new_count
        den = den + p_new
        p_wide = _dot(jnp.broadcast_to(p_new, (8, LANES)).astype(BF16), ind_t)[0:1, :]
        acc = acc + p_wide * v_new
    inv = 1.0 / den
    outs = [acc[:, h * HEAD_DIM:(h + 1) * HEAD_DIM] * inv[:, h:h + 1] for h in range(heads)]
    return jnp.concatenate(outs, axis=-1)


def _sample_win_attn_kernel(heads, tb, rb_ref, bkt_ref, q_ref, kn_ref, vn_ref,
                            k1, k4, k16, v1, v4, v16, o_ref, bias_s):
    step = pl.program_id(0)

    @pl.when(step == 0)
    def _():
        for p in range(len(DILATIONS)):
            bkt = bkt_ref[p]
            lane = lax.broadcasted_iota(jnp.int32, (bkt.shape[0], LANES), 1)
            acc = jnp.zeros((bkt.shape[0], LANES), F32)
            for h in range(heads):
                col = jnp.zeros(bkt.shape, F32)
                for k in range(REL_BUCKETS):
                    col = jnp.where(bkt == k, rb_ref[k * heads + h], col)
                acc = jnp.where(lane == h, col, acc)
            bias_s[p] = acc

    width = q_ref.shape[1]
    ind = _head_indicator(width, False)
    ind_t = _head_indicator(width, True)
    lane = lax.broadcasted_iota(jnp.int32, (1, LANES), 1)
    new_bias = jnp.zeros((1, LANES), F32)
    for h in range(heads):
        new_bias = jnp.where(lane == h, rb_ref[h], new_bias)
    for b in range(tb):
        r = pl.ds(step * tb + b, 1)
        out = _single_query_attention(
            q_ref[r, :], [k1[b], k4[b], k16[b]], [v1[b], v4[b], v16[b]],
            [bias_s[0], bias_s[1], bias_s[2]],
            (kn_ref[r, :], vn_ref[r, :]), new_bias, float(len(DILATIONS)), ind, ind_t, heads)
        o_ref[r, :] = out


def _sample_win_attn(q, k_new, v_new, cache_k, cache_v, rel_flat, heads, tb):
    nb, w = q.shape
    buf = cache_k.shape[1]
    r = SUB_WINDOW
    assert buf == r * DILATIONS[-1], "window buffer must hold exactly the widest dilated pattern"
    j = r - np.arange(r)
    bkt = np.stack([_rel_bucket(j * d) for d in DILATIONS]).astype(np.int32)[:, :, None]
    views, specs = [], []
    for cache in (cache_k, cache_v):
        for d in DILATIONS:
            views.append(cache.reshape(nb, buf // d, d * w))
            specs.append(pl.BlockSpec((tb, r, w), functools.partial(
                lambda i, blk: (i, blk, 0), blk=buf // d // r - 1)))
    whole = pl.BlockSpec((nb, w), lambda i: (0, 0))
    return pl.pallas_call(
        functools.partial(_sample_win_attn_kernel, heads, tb),
        out_shape=jax.ShapeDtypeStruct((nb, w), F32),
        grid=(nb // tb,),
        in_specs=[pl.BlockSpec(memory_space=pltpu.SMEM),
                  pl.BlockSpec((len(DILATIONS), r, 1), lambda i: (0, 0, 0)),
                  whole, whole, whole] + specs,
        out_specs=whole,
        scratch_shapes=[pltpu.VMEM((len(DILATIONS), r, LANES), F32)],
        compiler_params=_params(("arbitrary",)),
        name="sample_win_attn",
    )(rel_flat, jnp.asarray(bkt), q, k_new, v_new, *views)


def _sample_xattn_kernel(heads, tb, q_ref, k_ref, v_ref, o_ref):
    width = q_ref.shape[1]
    ind = _head_indicator(width, False)
    ind_t = _head_indicator(width, True)
    for b in range(tb):
        out = _single_query_attention(q_ref[b:b + 1, :], [k_ref[b]], [v_ref[b]], [None], None, None, 0.0,
                                      ind, ind_t, heads)
        o_ref[b:b + 1, :] = out


def _sample_xattn(qx, mem_k, mem_v, heads, tb):
    nb, w = qx.shape
    n_mem = mem_k.shape[1]
    return pl.pallas_call(
        functools.partial(_sample_xattn_kernel, heads, tb),
        out_shape=jax.ShapeDtypeStruct((nb, w), F32),
        grid=(nb // tb,),
        in_specs=[pl.BlockSpec((tb, w), lambda i: (i, 0)),
                  pl.BlockSpec((tb, n_mem, w), lambda i: (i, 0, 0)),
                  pl.BlockSpec((tb, n_mem, w), lambda i: (i, 0, 0))],
        out_specs=pl.BlockSpec((tb, w), lambda i: (i, 0)),
        compiler_params=_params(("parallel",)),
        name="sample_xattn",
    )(qx, mem_k, mem_v)


def _sample_mlstm_kernel(heads, tb, q_ref, k_ref, v_ref, mo_ref, li_ref, lf_ref, bi_ref, bf_ref,
                         gm_ref, c_ref, n_ref, m_ref, out_ref, c_out, n_out, m_out):
    qw = heads * MLSTM_DK
    hrow = lax.broadcasted_iota(jnp.int32, (LANES, qw), 0)
    hcol = lax.broadcasted_iota(jnp.int32, (LANES, qw), 1) // MLSTM_DK
    diag = hrow == hcol
    rows8 = slice(0, heads)
    for b in range(tb):
        q_bd = jnp.where(diag, q_ref[b:b + 1, :], 0.0) * QK_SCALE
        k_bd = jnp.where(diag, k_ref[b:b + 1, :], 0.0)
        c_b = c_ref[b]
        n_row = n_ref[b:b + 1, :]
        li = li_ref[b] + bi_ref[...]
        lf = _log_sigmoid(lf_ref[b] + bf_ref[...])
        m_prev = m_ref[b]
        qk = jnp.sum(q_bd * k_ref[b:b + 1, :], axis=-1, keepdims=True)[rows8]
        qn = jnp.sum(q_bd * n_row, axis=-1, keepdims=True)[rows8]
        q_c = _dot(q_bd.astype(BF16), c_b.astype(BF16))[rows8]
        inter = lf + m_prev
        m_t = jnp.maximum(inter, li)
        w_intra = jnp.exp(li - m_t)
        w_prev = jnp.exp(inter - m_t)
        sqk = qk * w_intra
        v_b = v_ref[b]
        num = sqk * v_b + w_prev * q_c
        den = sqk + w_prev * qn
        hh = num / jnp.maximum(jnp.abs(den), jnp.exp(-m_t))
        hn = hh * lax.rsqrt(jnp.mean(hh * hh, axis=-1, keepdims=True) + EPS)
        out_ref[b] = hn * gm_ref[...] * jax.nn.sigmoid(mo_ref[b])
        wv = jnp.concatenate([w_intra * v_b, jnp.zeros((LANES - heads, HEAD_DIM), F32)], axis=0)
        d_c = _dot_tn(k_bd.astype(BF16), wv.astype(BF16))
        for h in range(heads):
            hs = slice(h * MLSTM_DK, (h + 1) * MLSTM_DK)
            c_out[b, hs, :] = w_prev[h:h + 1, :] * c_b[hs, :] + d_c[hs, :]
        k_w = jnp.sum(k_bd[rows8] * w_intra, axis=0, keepdims=True)
        dec_row = jnp.sum(jnp.where(diag[rows8], w_prev, 0.0), axis=0, keepdims=True)
        n_out[b:b + 1, :] = dec_row * n_row + k_w
        m_out[b] = m_t


def _sample_mlstm(z6, gates, b_gate, g_mlstm, state_c, state_n, state_m, heads, tb):
    _, nb, w = z6.shape
    qw = heads * MLSTM_DK
    q = z6[3, :, :qw].astype(F32)
    k = z6[3, :, qw:].astype(F32)
    v = z6[4].astype(F32).reshape(nb, heads, HEAD_DIM)
    mo = z6[5].astype(F32).reshape(nb, heads, HEAD_DIM)
    li = gates[:, :heads].reshape(nb, heads, 1)
    lf = gates[:, heads:2 * heads].reshape(nb, heads, 1)
    row = lambda i: (i, 0)
    blk3 = lambda i: (i, 0, 0)
    const = lambda i: (0, 0)
    return pl.pallas_call(
        functools.partial(_sample_mlstm_kernel, heads, tb),
        out_shape=(jax.ShapeDtypeStruct((nb, heads, HEAD_DIM), F32),
                   jax.ShapeDtypeStruct((nb, qw, HEAD_DIM), F32),
                   jax.ShapeDtypeStruct((nb, qw), F32),
                   jax.ShapeDtypeStruct((nb, heads, 1), F32)),
        grid=(nb // tb,),
        in_specs=[pl.BlockSpec((tb, qw), row),
                  pl.BlockSpec((tb, qw), row),
                  pl.BlockSpec((tb, heads, HEAD_DIM), blk3),
                  pl.BlockSpec((tb, heads, HEAD_DIM), blk3),
                  pl.BlockSpec((tb, heads, 1), blk3),
                  pl.BlockSpec((tb, heads, 1), blk3),
                  pl.BlockSpec((heads, 1), const),
                  pl.BlockSpec((heads, 1), const),
                  pl.BlockSpec((heads, HEAD_DIM), const),
                  pl.BlockSpec((tb, qw, HEAD_DIM), blk3),
                  pl.BlockSpec((tb, qw), row),
                  pl.BlockSpec((tb, heads, 1), blk3)],
        out_specs=(pl.BlockSpec((tb, heads, HEAD_DIM), blk3),
                   pl.BlockSpec((tb, qw, HEAD_DIM), blk3),
                   pl.BlockSpec((tb, qw), row),
                   pl.BlockSpec((tb, heads, 1), blk3)),
        compiler_params=_params(("parallel",)),
        name="sample_mlstm",
    )(q, k, v, mo, li, lf, b_gate[:heads].reshape(heads, 1), b_gate[heads:].reshape(heads, 1),
      g_mlstm.reshape(heads, HEAD_DIM), state_c.reshape(nb, qw, HEAD_DIM), state_n.reshape(nb, qw),
      state_m.reshape(nb, heads, 1))


def _tile(m, cap):
    return min(m, cap)


def kernel(x_prompt, x_sample, mem_prompt, cache_win_k, cache_win_v, state_mlstm_C, state_mlstm_n,
           state_mlstm_m, cache_mem_k, cache_mem_v, rel_bias, g_pre_mix, g_post_mix, w_in, b_gate,
           g_mlstm, w_out, g_pre_xatt, g_post_xatt, g_mem, w_xq, w_xkv, w_xo, g_pre_mlp, g_post_mlp,
           w_up, w_down):
    depth = w_in.shape[0]
    assert depth == 1 and x_prompt.shape[0] == 1 and x_sample.shape[1] == 1
    _, seq, d_model = x_prompt.shape
    n_dec = x_sample.shape[0]
    att_heads = cache_win_k.shape[3]
    ml_heads = state_mlstm_C.shape[2]
    x_heads = cache_mem_k.shape[3]
    att_w = att_heads * HEAD_DIM
    ml_w = ml_heads * HEAD_DIM
    x_w = x_heads * HEAD_DIM
    n_mem = mem_prompt.shape[1]
    keep = cache_win_k.shape[2]
    n_main = 3 * att_w + 2 * ml_heads * MLSTM_DK + 2 * ml_w
    assert att_w == ml_w == 1024 and n_main % 1024 == 0 and seq >= keep

    def row(g):
        return g[0].reshape(1, -1)

    w6 = w_in[0, :, :n_main].astype(BF16)
    wg = jnp.zeros((d_model, LANES), F32).at[:, :2 * ml_heads].set(w_in[0, :, n_main:]).astype(BF16)
    w_att = w_out[0, :att_w].astype(BF16)
    w_ml = w_out[0, att_w:].astype(BF16)
    wq = w_xq[0].astype(BF16)
    wkv = w_xkv[0].astype(BF16)
    wo = w_xo[0].astype(BF16)
    wu = w_up[0].astype(BF16)
    wd = w_down[0].astype(BF16)
    rel_flat = rel_bias.reshape(-1)

    def tail(x1, qx_att, mem_attend, tm):
        x2 = _proj_post(mem_attend(qx_att), x1, wo, row(g_post_xatt), tm)
        return _mlp(x2, row(g_pre_mlp), wu, wd, row(g_post_mlp), tm, 1024)

    xp = x_prompt[0]
    mem_kv = _norm_matmul(mem_prompt[0], row(g_mem), wkv)
    mk_p, mv_p = mem_kv[:, :x_w], mem_kv[:, x_w:]
    z6, kv32, gates = _in_proj(xp, row(g_pre_mix), w6, wg, _tile(seq, 1024))
    parts = [_dil_attn(z6, rel_flat, d, att_heads) for d in DILATIONS]
    m_out, c_p, n_p, m_p = _mlstm_prompt(z6, gates, b_gate[0], g_mlstm[0], ml_heads)
    tm = _tile(seq, 512)
    x1, qx = _out_proj(tuple(p[0] for p in parts) + tuple(p[1] for p in parts), m_out, xp, w_att, w_ml,
                       row(g_post_mix), row(g_pre_xatt), wq, tm, att_heads)
    y_prompt = tail(x1, qx, lambda q: _xattn(q, mk_p, mv_p, tm, x_heads), tm)

    xs = x_sample[:, 0]
    z6s, kv32s, gates_s = _in_proj(xs, row(g_pre_mix), w6, wg, n_dec)
    att_s = _sample_win_attn(z6s[0].astype(F32), kv32s[0], kv32s[1],
                             cache_win_k[0].reshape(n_dec, keep, att_w),
                             cache_win_v[0].reshape(n_dec, keep, att_w), rel_flat, att_heads, 4).astype(BF16)
    mo_s, c_s, n_s, m_s = _sample_mlstm(z6s, gates_s, b_gate[0], g_mlstm[0], state_mlstm_C[0],
                                        state_mlstm_n[0], state_mlstm_m[0], ml_heads, 8)
    x1s, qxs = _out_proj((att_s,), mo_s.reshape(n_dec, ml_w).astype(BF16), xs, w_att, w_ml,
                         row(g_post_mix), row(g_pre_xatt), wq, n_dec, att_heads)
    mem_k_s = cache_mem_k[0].reshape(n_dec, n_mem, x_w)
    mem_v_s = cache_mem_v[0].reshape(n_dec, n_mem, x_w)
    y_sample = tail(x1s, qxs, lambda q: _sample_xattn(q.astype(F32), mem_k_s, mem_v_s, x_heads, 8).astype(BF16),
                    n_dec)

    dk = MLSTM_DK
    return (y_prompt[None],
            y_sample[:, None],
            kv32[0, seq - keep:].reshape(1, 1, keep, att_heads, HEAD_DIM),
            kv32[1, seq - keep:].reshape(1, 1, keep, att_heads, HEAD_DIM),
            c_p.reshape(1, 1, ml_heads, dk, HEAD_DIM),
            n_p.reshape(1, 1, ml_heads, dk),
            m_p[:, 0].reshape(1, 1, ml_heads),
            mk_p.reshape(1, 1, n_mem, x_heads, HEAD_DIM),
            mv_p.reshape(1, 1, n_mem, x_heads, HEAD_DIM),
            kv32s[0].reshape(1, n_dec, 1, att_heads, HEAD_DIM),
            kv32s[1].reshape(1, n_dec, 1, att_heads, HEAD_DIM),
            c_s.reshape(1, n_dec, ml_heads, dk, HEAD_DIM),
            n_s.reshape(1, n_dec, ml_heads, dk),
            m_s.reshape(1, n_dec, ml_heads))
```

```python
import functools
import math

import numpy as np
import jax
import jax.numpy as jnp
from jax import lax
from jax.experimental import pallas as pl
from jax.experimental.pallas import tpu as pltpu

F32 = jnp.float32
BF16 = jnp.bfloat16

EPS = 1e-6
HEAD_DIM = 128
MLSTM_DK = 64
LANES = 128
DILATIONS = (1, 4, 16)
SUB_WINDOW = 128
ATT_BLOCK = 128
MLSTM_CHUNK = 128
REL_BUCKETS = 32
REL_MAX_DIST = 2048
NEG = -1e30
ATT_SCALE = HEAD_DIM ** -0.5
QK_SCALE = MLSTM_DK ** -0.5
VMEM_LIMIT = 56 * 2 ** 20


def _params(semantics):
    return pltpu.CompilerParams(dimension_semantics=semantics, vmem_limit_bytes=VMEM_LIMIT)


def _rms(xf, g):
    ms = jnp.mean(xf * xf, axis=-1, keepdims=True)
    return xf * lax.rsqrt(ms + EPS) * g


def _dot(a, b):
    return jnp.dot(a, b, preferred_element_type=F32)


def _dot_nt(a, b):
    return lax.dot_general(a, b, (((1,), (1,)), ((), ())), preferred_element_type=F32)


def _dot_tn(a, b):
    return lax.dot_general(a, b, (((0,), (0,)), ((), ())), preferred_element_type=F32)


def _log_sigmoid(x):
    return jnp.minimum(x, 0.0) - jnp.log(1.0 + jnp.exp(-jnp.abs(x)))


def _rel_bucket(dist):
    dist = np.asarray(dist)
    exact = REL_BUCKETS // 2
    far = exact + (np.log(np.maximum(dist, exact) / exact) / math.log(REL_MAX_DIST / exact)
                   * (REL_BUCKETS - exact)).astype(np.int32)
    return np.where(dist < exact, dist, np.minimum(far, REL_BUCKETS - 1)).astype(np.int32)


def _norm_matmul_kernel(x_ref, g_ref, w_ref, o_ref):
    xn = _rms(x_ref[...], g_ref[...]).astype(BF16)
    o_ref[...] = _dot(xn, w_ref[...])


def _norm_matmul(x, g, w):
    m, d = x.shape
    n = w.shape[1]
    return pl.pallas_call(
        _norm_matmul_kernel,
        out_shape=jax.ShapeDtypeStruct((m, n), F32),
        grid=(1,),
        in_specs=[pl.BlockSpec((m, d), lambda i: (0, 0)),
                  pl.BlockSpec((1, d), lambda i: (0, 0)),
                  pl.BlockSpec((d, n), lambda i: (0, 0))],
        out_specs=pl.BlockSpec((m, n), lambda i: (0, 0)),
        compiler_params=_params(("arbitrary",)),
        name="norm_matmul",
    )(x, g, w)


def _in_proj_kernel(dilations, x_ref, g_ref, w_ref, wg_ref, z_ref, kv_ref, gate_ref, *rest):
    zd_refs = rest[:len(dilations)]
    xn_ref = rest[len(dilations)]
    j = pl.program_id(1)

    @pl.when(j == 0)
    def _():
        xn = _rms(x_ref[...], g_ref[...]).astype(BF16)
        xn_ref[...] = xn
        gate_ref[...] = _dot(xn, wg_ref[...])

    acc = _dot(xn_ref[...], w_ref[...])
    z_ref[...] = acc.astype(BF16)

    @pl.when((j == 1) | (j == 2))
    def _():
        kv_ref[...] = acc

    if dilations:
        slab = rest[len(dilations) + 1]
        tm, tn = acc.shape

        @pl.when(j <= 2)
        def _():
            for h in range(tn // LANES):
                slab[h] = acc[:, h * LANES:(h + 1) * LANES]
            for d, zd_ref in zip(dilations, zd_refs):
                for r in range(d):
                    for h in range(tn // LANES):
                        zd_ref[r, :, h * LANES:(h + 1) * LANES] = (
                            slab[h, pl.ds(r, tm // d, stride=d), :].astype(BF16))


def _in_proj(x, g, w6, wg, tm, dilations):
    m, d = x.shape
    tn = 1024
    nblk = w6.shape[1] // tn
    qkv = lambda i, j: (jnp.minimum(j, 2), 0, i, 0)
    return pl.pallas_call(
        functools.partial(_in_proj_kernel, dilations),
        out_shape=(jax.ShapeDtypeStruct((nblk, m, tn), BF16),
                   jax.ShapeDtypeStruct((2, m, tn), F32),
                   jax.ShapeDtypeStruct((m, LANES), F32))
        + tuple(jax.ShapeDtypeStruct((3, dl, m // dl, tn), BF16) for dl in dilations),
        grid=(m // tm, nblk),
        in_specs=[pl.BlockSpec((tm, d), lambda i, j: (i, 0)),
                  pl.BlockSpec((1, d), lambda i, j: (0, 0)),
                  pl.BlockSpec((d, tn), lambda i, j: (0, j)),
                  pl.BlockSpec((d, LANES), lambda i, j: (0, 0))],
        out_specs=(pl.BlockSpec((None, tm, tn), lambda i, j: (j, i, 0)),
                   pl.BlockSpec((None, tm, tn), lambda i, j: (jnp.clip(j - 1, 0, 1), i, 0)),
                   pl.BlockSpec((tm, LANES), lambda i, j: (i, 0)))
        + tuple(pl.BlockSpec((None, dl, tm // dl, tn), qkv) for dl in dilations),
        scratch_shapes=[pltpu.VMEM((tm, d), BF16)]
        + ([pltpu.VMEM((tn // LANES, tm, LANES), F32)] if dilations else []),
        compiler_params=_params(("parallel", "arbitrary")),
        name="in_proj_dilated" if dilations else "in_proj",
    )(x, g, w6, wg)


def _dil_attn_kernel(heads, d, rb_ref, bkt_c_ref, bkt_p_ref, q_ref, kc_ref, kp_ref, vc_ref, vp_ref,
                     o_ref, lse_ref, bias_c, bias_p, s_scr, p_scr):
    n = pl.program_id(0)
    r = pl.program_id(1)

    @pl.when((n == 0) & (r == 0))
    def _():
        for bkt_ref, dst in ((bkt_c_ref, bias_c), (bkt_p_ref, bias_p)):
            bkt = bkt_ref[...]
            for h in range(heads):
                acc = jnp.full(bkt.shape, NEG, F32)
                for k in range(REL_BUCKETS):
                    acc = jnp.where(bkt == k, rb_ref[k * heads + h], acc)
                dst[h] = acc

    no_prev = n == 0
    blk = ATT_BLOCK
    for h in range(heads):
        sl = slice(h * HEAD_DIM, (h + 1) * HEAD_DIM)
        qh = q_ref[:, sl]
        s_scr[h, :, :blk] = _dot_nt(qh, kc_ref[:, sl]) * ATT_SCALE + bias_c[h]
        sp = _dot_nt(qh, kp_ref[:, sl]) * ATT_SCALE + bias_p[h]
        s_scr[h, :, blk:] = jnp.where(no_prev, NEG, sp)
    lane = lax.broadcasted_iota(jnp.int32, (blk, LANES), 1)
    lse_tile = jnp.zeros((blk, LANES), F32)
    inv_tile = jnp.zeros((blk, LANES), F32)
    for h in range(heads):
        s = s_scr[h]
        m = jnp.max(s, axis=-1, keepdims=True)
        p = jnp.exp(s - m)
        l = jnp.sum(p, axis=-1, keepdims=True)
        p_scr[h] = p.astype(BF16)
        lse_tile = jnp.where(lane == h, m + jnp.log(l), lse_tile)
        inv_tile = jnp.where(lane == h, 1.0 / l, inv_tile)
    rows = pl.ds(r, blk, stride=d)
    lse_ref[rows, :] = lse_tile
    for h in range(heads):
        sl = slice(h * HEAD_DIM, (h + 1) * HEAD_DIM)
        o = _dot(p_scr[h, :, :blk], vc_ref[:, sl]) + _dot(p_scr[h, :, blk:], vp_ref[:, sl])
        o_ref[h, rows, :] = o * inv_tile[:, h:h + 1]


def _dil_attn(zd, rel_flat, dilation, heads):
    d = dilation
    sub_len, w = zd.shape[2], zd.shape[3]
    s = sub_len * d
    nb = sub_len // ATT_BLOCK
    qi = np.arange(ATT_BLOCK)[:, None]
    ki = np.arange(ATT_BLOCK)[None, :]
    dist_c = qi - ki
    dist_p = ATT_BLOCK + qi - ki
    bkt_c = np.where(dist_c >= 0, _rel_bucket(np.maximum(dist_c, 0) * d), -1).astype(np.int32)
    bkt_p = np.where(dist_p <= SUB_WINDOW, _rel_bucket(dist_p * d), -1).astype(np.int32)
    blk = (None, None, ATT_BLOCK, w)
    const = lambda n, r: (0, 0)

    def cur(c):
        return pl.BlockSpec(blk, lambda n, r: (c, r, n, 0))

    def prev(c):
        return pl.BlockSpec(blk, lambda n, r: (c, r, jnp.maximum(n - 1, 0), 0))

    return pl.pallas_call(
        functools.partial(_dil_attn_kernel, heads, d),
        out_shape=(jax.ShapeDtypeStruct((heads, s, HEAD_DIM), F32),
                   jax.ShapeDtypeStruct((s, LANES), F32)),
        grid=(nb, d),
        in_specs=[pl.BlockSpec(memory_space=pltpu.SMEM),
                  pl.BlockSpec((ATT_BLOCK, ATT_BLOCK), const),
                  pl.BlockSpec((ATT_BLOCK, ATT_BLOCK), const),
                  cur(0), cur(1), prev(1), cur(2), prev(2)],
        out_specs=(pl.BlockSpec((heads, ATT_BLOCK * d, HEAD_DIM), lambda n, r: (0, n, 0)),
                   pl.BlockSpec((ATT_BLOCK * d, LANES), lambda n, r: (n, 0))),
        scratch_shapes=[pltpu.VMEM((heads, ATT_BLOCK, ATT_BLOCK), F32),
                        pltpu.VMEM((heads, ATT_BLOCK, ATT_BLOCK), F32),
                        pltpu.VMEM((heads, ATT_BLOCK, 2 * ATT_BLOCK), F32),
                        pltpu.VMEM((heads, ATT_BLOCK, 2 * ATT_BLOCK), BF16)],
        compiler_params=_params(("arbitrary", "arbitrary")),
        name=f"dil_attn_d{d}",
    )(rel_flat, jnp.asarray(bkt_c), jnp.asarray(bkt_p), zd, zd, zd, zd, zd)


def _mlstm_kernel(heads, qk_ref, v_ref, mo_ref, gn_ref, gt_ref, bgr_ref, bgc_ref, gm_ref,
                  c0_ref, n0_ref, m0_ref, out_ref, c_out, n_out, m_out, c_s, n_s, m_s):
    step = pl.program_id(0)
    t = MLSTM_CHUNK
    qw = heads * MLSTM_DK

    @pl.when(step == 0)
    def _():
        c_s[...] = c0_ref[...]
        n_s[...] = n0_ref[...]
        m_s[...] = m0_ref[...]

    row = lax.broadcasted_iota(jnp.int32, (t, t), 0)
    col = lax.broadcasted_iota(jnp.int32, (t, t), 1)
    causal = col <= row
    upto = row <= col
    lane = lax.broadcasted_iota(jnp.int32, (1, LANES), 1)
    subl = lax.broadcasted_iota(jnp.int32, (LANES, 1), 0)

    for pair in range(heads // 2):
        psl = slice(pair * LANES, (pair + 1) * LANES)
        q2 = qk_ref[:, psl]
        k2 = qk_ref[:, qw + pair * LANES: qw + (pair + 1) * LANES]
        c2 = c_s[psl, :]
        n2 = n_s[:, psl]
        d_c, d_n, decays = [], [], []
        for half in range(2):
            h = 2 * pair + half
            sl = slice(h * HEAD_DIM, (h + 1) * HEAD_DIM)
            mine = (lane >= half * MLSTM_DK) & (lane < (half + 1) * MLSTM_DK)
            li_row = gt_ref[h:h + 1, :] + bgc_ref[h:h + 1, :]
            lf_row = _log_sigmoid(gt_ref[heads + h:heads + h + 1, :] + bgc_ref[heads + h:heads + h + 1, :])
            li_col = gn_ref[:, h:h + 1] + bgr_ref[:, h:h + 1]
            lf_col = _log_sigmoid(gn_ref[:, heads + h:heads + h + 1] + bgr_ref[:, heads + h:heads + h + 1])
            b_col = jnp.sum(jnp.where(causal, lf_row, 0.0), axis=-1, keepdims=True)
            b_row = jnp.sum(jnp.where(upto, lf_col, 0.0), axis=0, keepdims=True)
            m_prev = m_s[h:h + 1, 0:1]
            cm = jnp.where(causal, li_row - b_row, NEG)
            a_col = jnp.maximum(jnp.max(cm, axis=-1, keepdims=True), m_prev)
            w_intra = jnp.exp(cm - a_col)
            w_prev = jnp.exp(m_prev - a_col)
            qm = jnp.where(mine, q2, 0.0) * QK_SCALE
            km = jnp.where(mine, k2, 0.0)
            sqk = _dot_nt(qm, km) * w_intra
            vh = v_ref[:, sl]
            num = _dot(sqk.astype(BF16), vh) + w_prev * _dot(qm, c2.astype(BF16))
            qn = jnp.sum(qm.astype(F32) * n2, axis=-1, keepdims=True)
            den = jnp.sum(sqk, axis=-1, keepdims=True) + w_prev * qn
            m_t = b_col + a_col
            hh = num / jnp.maximum(jnp.abs(den), jnp.exp(-m_t))
            hn = hh * lax.rsqrt(jnp.mean(hh * hh, axis=-1, keepdims=True) + EPS)
            gate = jax.nn.sigmoid(mo_ref[:, sl].astype(F32))
            out_ref[:, sl] = (hn * gm_ref[:, sl] * gate).astype(out_ref.dtype)
            a_last = a_col[t - 1:t, :]
            b_last = b_col[t - 1:t, :]
            m_new = b_last + a_last
            w_k = jnp.exp((li_col - b_col) - a_last)
            decay = jnp.exp(b_last + m_prev - m_new)
            kw = km.astype(F32) * w_k
            d_c.append(_dot_tn(kw.astype(BF16), vh))
            d_n.append(jnp.sum(kw, axis=0, keepdims=True))
            decays.append(decay)
            m_s[h:h + 1, :] = jnp.broadcast_to(m_new, (1, LANES))
        dec_col = jnp.where(subl < MLSTM_DK, decays[0], decays[1])
        dec_row = jnp.where(lane < MLSTM_DK, decays[0], decays[1])
        c_s[psl, :] = dec_col * c2 + d_c[0] + d_c[1]
        n_s[:, psl] = dec_row * n2 + d_n[0] + d_n[1]

    @pl.when(step == pl.num_programs(0) - 1)
    def _():
        c_out[...] = c_s[...]
        n_out[...] = n_s[...]
        m_out[...] = m_s[...]


def _mlstm_prompt(z6, gates, b_gate, g_mlstm, heads):
    _, s, w = z6.shape
    t = MLSTM_CHUNK
    gt = gates[:, :2 * heads].T
    bg_row = jnp.zeros((1, LANES), F32).at[0, :2 * heads].set(b_gate)
    bg_col = b_gate.reshape(2 * heads, 1)
    qw = heads * MLSTM_DK
    c0 = jnp.zeros((qw, HEAD_DIM), F32)
    n0 = jnp.zeros((1, qw), F32)
    m0 = jnp.zeros((heads, LANES), F32)
    const = lambda c: (0, 0)
    return pl.pallas_call(
        functools.partial(_mlstm_kernel, heads),
        out_shape=(jax.ShapeDtypeStruct((s, w), BF16),
                   jax.ShapeDtypeStruct((qw, HEAD_DIM), F32),
                   jax.ShapeDtypeStruct((1, qw), F32),
                   jax.ShapeDtypeStruct((heads, LANES), F32)),
        grid=(s // t,),
        in_specs=[pl.BlockSpec((None, t, w), lambda c: (3, c, 0)),
                  pl.BlockSpec((None, t, w), lambda c: (4, c, 0)),
                  pl.BlockSpec((None, t, w), lambda c: (5, c, 0)),
                  pl.BlockSpec((t, LANES), lambda c: (c, 0)),
                  pl.BlockSpec((2 * heads, t), lambda c: (0, c)),
                  pl.BlockSpec((1, LANES), const),
                  pl.BlockSpec((2 * heads, 1), const),
                  pl.BlockSpec((1, w), const),
                  pl.BlockSpec((qw, HEAD_DIM), const),
                  pl.BlockSpec((1, qw), const),
                  pl.BlockSpec((heads, LANES), const)],
        out_specs=(pl.BlockSpec((t, w), lambda c: (c, 0)),
                   pl.BlockSpec((qw, HEAD_DIM), const),
                   pl.BlockSpec((1, qw), const),
                   pl.BlockSpec((heads, LANES), const)),
        scratch_shapes=[pltpu.VMEM((qw, HEAD_DIM), F32),
                        pltpu.VMEM((1, qw), F32),
                        pltpu.VMEM((heads, LANES), F32)],
        compiler_params=_params(("arbitrary",)),
        name="mlstm_prompt",
    )(z6, z6, z6, gates, gt, bg_row, bg_col, g_mlstm.reshape(1, w), c0, n0, m0)


def _out_proj_kernel(merge, heads, *refs):
    if merge:
        (o1, o2, o3, l1, l2, l3, mo_ref, x_ref, wa_ref, wm_ref, gpost_ref, gpre_ref, wq_ref,
         x1_ref, qx_ref, att_s) = refs
        la, lb, lc = l1[...], l2[...], l3[...]
        mx = jnp.maximum(jnp.maximum(la, lb), lc)
        ea, eb, ec = jnp.exp(la - mx), jnp.exp(lb - mx), jnp.exp(lc - mx)
        tot = ea + eb + ec
        wa_, wb_, wc_ = ea / tot, eb / tot, ec / tot
        for h in range(heads):
            sl = slice(h * HEAD_DIM, (h + 1) * HEAD_DIM)
            mix = wa_[:, h:h + 1] * o1[h] + wb_[:, h:h + 1] * o2[h] + wc_[:, h:h + 1] * o3[h]
            att_s[:, sl] = mix.astype(BF16)
        att = att_s[...]
    else:
        att_ref, mo_ref, x_ref, wa_ref, wm_ref, gpost_ref, gpre_ref, wq_ref, x1_ref, qx_ref = refs
        att = att_ref[...]
    y = _dot(att, wa_ref[...]) + _dot(mo_ref[...], wm_ref[...])
    x1 = x_ref[...] + _rms(y, gpost_ref[...])
    x1_ref[...] = x1
    u = _rms(x1, gpre_ref[...]).astype(BF16)
    qx_ref[...] = _dot(u, wq_ref[...]).astype(qx_ref.dtype)


def _out_proj(att_parts, m_out, x, w_att, w_ml, g_post, g_pre, w_xq, tm, heads):
    m, d = x.shape
    wa = w_att.shape[0]
    wm = w_ml.shape[0]
    nq = w_xq.shape[1]
    merge = len(att_parts) == 6
    row = lambda i: (i, 0)
    const = lambda i: (0, 0)
    if merge:
        part_specs = ([pl.BlockSpec((heads, tm, HEAD_DIM), lambda i: (0, i, 0))] * 3
                      + [pl.BlockSpec((tm, LANES), row)] * 3)
    else:
        part_specs = [pl.BlockSpec((tm, wa), row)]
    return pl.pallas_call(
        functools.partial(_out_proj_kernel, merge, heads),
        out_shape=(jax.ShapeDtypeStruct((m, d), F32), jax.ShapeDtypeStruct((m, nq), BF16)),
        grid=(m // tm,),
        in_specs=part_specs + [pl.BlockSpec((tm, wm), row),
                               pl.BlockSpec((tm, d), row),
                               pl.BlockSpec((wa, d), const),
                               pl.BlockSpec((wm, d), const),
                               pl.BlockSpec((1, d), const),
                               pl.BlockSpec((1, d), const),
                               pl.BlockSpec((d, nq), const)],
        out_specs=(pl.BlockSpec((tm, d), row), pl.BlockSpec((tm, nq), row)),
        scratch_shapes=[pltpu.VMEM((tm, wa), BF16)] if merge else [],
        compiler_params=_params(("parallel",)),
        name="out_proj_merge" if merge else "out_proj",
    )(*att_parts, m_out, x, w_att, w_ml, g_post, g_pre, w_xq)


def _xattn_kernel(heads, q_ref, mk_ref, mv_ref, o_ref):
    for h in range(heads):
        sl = slice(h * HEAD_DIM, (h + 1) * HEAD_DIM)
        s = _dot_nt(q_ref[:, sl], mk_ref[:, sl].astype(BF16)) * ATT_SCALE
        m = jnp.max(s, axis=-1, keepdims=True)
        p = jnp.exp(s - m)
        l = jnp.sum(p, axis=-1, keepdims=True)
        o = _dot(p.astype(BF16), mv_ref[:, sl].astype(BF16))
        o_ref[:, sl] = (o / l).astype(o_ref.dtype)


def _xattn(qx, mem_k, mem_v, tm, heads):
    m, w = qx.shape
    n_mem = mem_k.shape[0]
    return pl.pallas_call(
        functools.partial(_xattn_kernel, heads),
        out_shape=jax.ShapeDtypeStruct((m, w), BF16),
        grid=(m // tm,),
        in_specs=[pl.BlockSpec((tm, w), lambda i: (i, 0)),
                  pl.BlockSpec((n_mem, w), lambda i: (0, 0)),
                  pl.BlockSpec((n_mem, w), lambda i: (0, 0))],
        out_specs=pl.BlockSpec((tm, w), lambda i: (i, 0)),
        compiler_params=_params(("parallel",)),
        name="xattn_prompt",
    )(qx, mem_k, mem_v)


def _proj_post_kernel(a_ref, x_ref, w_ref, g_ref, y_ref):
    y_ref[...] = x_ref[...] + _rms(_dot(a_ref[...], w_ref[...]), g_ref[...])


def _proj_post(a, x, w, g, tm):
    m, d = x.shape
    k = a.shape[1]
    return pl.pallas_call(
        _proj_post_kernel,
        out_shape=jax.ShapeDtypeStruct((m, d), F32),
        grid=(m // tm,),
        in_specs=[pl.BlockSpec((tm, k), lambda i: (i, 0)),
                  pl.BlockSpec((tm, d), lambda i: (i, 0)),
                  pl.BlockSpec((k, d), lambda i: (0, 0)),
                  pl.BlockSpec((1, d), lambda i: (0, 0))],
        out_specs=pl.BlockSpec((tm, d), lambda i: (i, 0)),
        compiler_params=_params(("parallel",)),
        name="proj_post",
    )(a, x, w, g)


def _mlp_kernel(x_ref, gpre_ref, wu_ref, wd_ref, gpost_ref, y_ref, xn_s, acc_s):
    j = pl.program_id(1)

    @pl.when(j == 0)
    def _():
        xn_s[...] = _rms(x_ref[...], gpre_ref[...]).astype(BF16)
        acc_s[...] = jnp.zeros(acc_s.shape, F32)

    hidden = jnp.square(jnp.maximum(_dot(xn_s[...], wu_ref[...]), 0.0))
    acc_s[...] += _dot(hidden.astype(BF16), wd_ref[...])

    @pl.when(j == pl.num_programs(1) - 1)
    def _():
        y_ref[...] = x_ref[...] + _rms(acc_s[...], gpost_ref[...])


def _mlp(x, g_pre, w_up, w_down, g_post, tm, tf):
    m, d = x.shape
    ff = w_up.shape[1]
    return pl.pallas_call(
        _mlp_kernel,
        out_shape=jax.ShapeDtypeStruct((m, d), F32),
        grid=(m // tm, ff // tf),
        in_specs=[pl.BlockSpec((tm, d), lambda i, j: (i, 0)),
                  pl.BlockSpec((1, d), lambda i, j: (0, 0)),
                  pl.BlockSpec((d, tf), lambda i, j: (0, j)),
                  pl.BlockSpec((tf, d), lambda i, j: (j, 0)),
                  pl.BlockSpec((1, d), lambda i, j: (0, 0))],
        out_specs=pl.BlockSpec((tm, d), lambda i, j: (i, 0)),
        scratch_shapes=[pltpu.VMEM((tm, d), BF16), pltpu.VMEM((tm, d), F32)],
        compiler_params=_params(("parallel", "arbitrary")),
        name="mlp",
    )(x, g_pre, w_up, w_down, g_post)


def _single_query_attention(q, groups, new):
    scores = []
    for k3, _, b3 in groups:
        s = jnp.sum(k3 * q[None], axis=-1, keepdims=True) * ATT_SCALE
        scores.append(s if b3 is None else s + b3)
    mx = jnp.max(scores[0], axis=0)
    for s in scores[1:]:
        mx = jnp.maximum(mx, jnp.max(s, axis=0))
    if new is not None:
        k_new, v_new, b_new, count = new
        s_new = jnp.sum(k_new * q, axis=-1, keepdims=True) * ATT_SCALE + b_new
        mx = jnp.maximum(mx, s_new)
    den = jnp.zeros(mx.shape, F32)
    acc = jnp.zeros(q.shape, F32)
    for s, (_, v3, _) in zip(scores, groups):
        p = jnp.exp(s - mx[None])
        den = den + jnp.sum(p, axis=0)
        acc = acc + jnp.sum(p * v3, axis=0)
    if new is not None:
        p_new = jnp.exp(s_new - mx) * count
        den = den + p_new
        acc = acc + p_new * v_new
    return acc / den


def _sample_win_attn_kernel(tb, rel_ref, bkt_ref, q_ref, kn_ref, vn_ref,
                            k1, k4, k16, v1, v4, v16, o_ref, bias_s):
    @pl.when(pl.program_id(0) == 0)
    def _():
        for p in range(len(DILATIONS)):
            bkt = bkt_ref[p]
            acc = jnp.zeros(bkt.shape, F32)
            for k in range(REL_BUCKETS):
                acc = jnp.where(bkt == k, rel_ref[k][None], acc)
            bias_s[p] = acc

    for b in range(tb):
        groups = [(k[b], v[b], bias_s[p]) for p, (k, v) in enumerate(((k1, v1), (k4, v4), (k16, v16)))]
        new = (kn_ref[b], vn_ref[b], rel_ref[0], float(len(DILATIONS)))
        o_ref[b] = _single_query_attention(q_ref[b], groups, new)


def _sample_win_attn(q, k_new, v_new, cache_k, cache_v, rel_bias, tb):
    nb, heads, hd = q.shape
    buf = cache_k.shape[1]
    r = SUB_WINDOW
    assert buf == r * DILATIONS[-1], "window buffer must hold exactly the widest dilated pattern"
    j = r - np.arange(r)
    bkt = np.stack([_rel_bucket(j * d) for d in DILATIONS]).astype(np.int32)
    bkt = np.broadcast_to(bkt[:, :, None, None], (len(DILATIONS), r, heads, 1))
    views, specs = [], []
    for cache in (cache_k, cache_v):
        for d in DILATIONS:
            views.append(cache.reshape(nb, buf // d, d, heads, hd))
            specs.append(pl.BlockSpec((tb, r, None, heads, hd), functools.partial(
                lambda i, blk: (i, blk, 0, 0, 0), blk=buf // d // r - 1)))
    tok = pl.BlockSpec((tb, heads, hd), lambda i: (i, 0, 0))
    return pl.pallas_call(
        functools.partial(_sample_win_attn_kernel, tb),
        out_shape=jax.ShapeDtypeStruct((nb, heads, hd), F32),
        grid=(nb // tb,),
        in_specs=[pl.BlockSpec((REL_BUCKETS, heads, 1), lambda i: (0, 0, 0)),
                  pl.BlockSpec((len(DILATIONS), r, heads, 1), lambda i: (0, 0, 0, 0)),
                  tok, tok, tok] + specs,
        out_specs=tok,
        scratch_shapes=[pltpu.VMEM((len(DILATIONS), r, heads, 1), F32)],
        compiler_params=_params(("arbitrary",)),
        name="sample_win_attn",
    )(rel_bias.reshape(REL_BUCKETS, heads, 1), jnp.asarray(bkt), q, k_new, v_new, *views)


def _sample_xattn_kernel(tb, q_ref, k_ref, v_ref, o_ref):
    for b in range(tb):
        o_ref[b] = _single_query_attention(q_ref[b], [(k_ref[b], v_ref[b], None)], None)


def _sample_xattn(qx, mem_k, mem_v, tb):
    nb, heads, hd = qx.shape
    n_mem = mem_k.shape[1]
    tok = pl.BlockSpec((tb, heads, hd), lambda i: (i, 0, 0))
    mem = pl.BlockSpec((tb, n_mem, heads, hd), lambda i: (i, 0, 0, 0))
    return pl.pallas_call(
        functools.partial(_sample_xattn_kernel, tb),
        out_shape=jax.ShapeDtypeStruct((nb, heads, hd), F32),
        grid=(nb // tb,),
        in_specs=[tok, mem, mem],
        out_specs=tok,
        compiler_params=_params(("parallel",)),
        name="sample_xattn",
    )(qx, mem_k, mem_v)


def _sample_mlstm_kernel(heads, tb, q_ref, k_ref, v_ref, mo_ref, li_ref, lf_ref, bi_ref, bf_ref,
                         gm_ref, c_ref, n_ref, m_ref, out_ref, c_out, n_out, m_out):
    qw = heads * MLSTM_DK
    hrow = lax.broadcasted_iota(jnp.int32, (LANES, qw), 0)
    hcol = lax.broadcasted_iota(jnp.int32, (LANES, qw), 1) // MLSTM_DK
    diag = hrow == hcol
    rows8 = slice(0, heads)
    for b in range(tb):
        q_bd = jnp.where(diag, q_ref[b:b + 1, :], 0.0) * QK_SCALE
        k_bd = jnp.where(diag, k_ref[b:b + 1, :], 0.0)
        c_b = c_ref[b]
        n_row = n_ref[b:b + 1, :]
        li = li_ref[b] + bi_ref[...]
        lf = _log_sigmoid(lf_ref[b] + bf_ref[...])
        m_prev = m_ref[b]
        qk = jnp.sum(q_bd * k_ref[b:b + 1, :], axis=-1, keepdims=True)[rows8]
        qn = jnp.sum(q_bd * n_row, axis=-1, keepdims=True)[rows8]
        q_c = _dot(q_bd.astype(BF16), c_b.astype(BF16))[rows8]
        inter = lf + m_prev
        m_t = jnp.maximum(inter, li)
        w_intra = jnp.exp(li - m_t)
        w_prev = jnp.exp(inter - m_t)
        sqk = qk * w_intra
        v_b = v_ref[b]
        num = sqk * v_b + w_prev * q_c
        den = sqk + w_prev * qn
        hh = num / jnp.maximum(jnp.abs(den), jnp.exp(-m_t))
        hn = hh * lax.rsqrt(jnp.mean(hh * hh, axis=-1, keepdims=True) + EPS)
        out_ref[b] = hn * gm_ref[...] * jax.nn.sigmoid(mo_ref[b])
        wv = jnp.concatenate([w_intra * v_b, jnp.zeros((LANES - heads, HEAD_DIM), F32)], axis=0)
        d_c = _dot_tn(k_bd.astype(BF16), wv.astype(BF16))
        for h in range(heads):
            hs = slice(h * MLSTM_DK, (h + 1) * MLSTM_DK)
            c_out[b, hs, :] = w_prev[h:h + 1, :] * c_b[hs, :] + d_c[hs, :]
        k_w = jnp.sum(k_bd[rows8] * w_intra, axis=0, keepdims=True)
        dec_row = jnp.sum(jnp.where(diag[rows8], w_prev, 0.0), axis=0, keepdims=True)
        n_out[b:b + 1, :] = dec_row * n_row + k_w
        m_out[b] = m_t


def _sample_mlstm(z6, gates, b_gate, g_mlstm, state_c, state_n, state_m, heads, tb):
    _, nb, w = z6.shape
    qw = heads * MLSTM_DK
    q = z6[3, :, :qw].astype(F32)
    k = z6[3, :, qw:].astype(F32)
    v = z6[4].astype(F32).reshape(nb, heads, HEAD_DIM)
    mo = z6[5].astype(F32).reshape(nb, heads, HEAD_DIM)
    li = gates[:, :heads].reshape(nb, heads, 1)
    lf = gates[:, heads:2 * heads].reshape(nb, heads, 1)
    row = lambda i: (i, 0)
    blk3 = lambda i: (i, 0, 0)
    const = lambda i: (0, 0)
    return pl.pallas_call(
        functools.partial(_sample_mlstm_kernel, heads, tb),
        out_shape=(jax.ShapeDtypeStruct((nb, heads, HEAD_DIM), F32),
                   jax.ShapeDtypeStruct((nb, qw, HEAD_DIM), F32),
                   jax.ShapeDtypeStruct((nb, qw), F32),
                   jax.ShapeDtypeStruct((nb, heads, 1), F32)),
        grid=(nb // tb,),
        in_specs=[pl.BlockSpec((tb, qw), row),
                  pl.BlockSpec((tb, qw), row),
                  pl.BlockSpec((tb, heads, HEAD_DIM), blk3),
                  pl.BlockSpec((tb, heads, HEAD_DIM), blk3),
                  pl.BlockSpec((tb, heads, 1), blk3),
                  pl.BlockSpec((tb, heads, 1), blk3),
                  pl.BlockSpec((heads, 1), const),
                  pl.BlockSpec((heads, 1), const),
                  pl.BlockSpec((heads, HEAD_DIM), const),
                  pl.BlockSpec((tb, qw, HEAD_DIM), blk3),
                  pl.BlockSpec((tb, qw), row),
                  pl.BlockSpec((tb, heads, 1), blk3)],
        out_specs=(pl.BlockSpec((tb, heads, HEAD_DIM), blk3),
                   pl.BlockSpec((tb, qw, HEAD_DIM), blk3),
                   pl.BlockSpec((tb, qw), row),
                   pl.BlockSpec((tb, heads, 1), blk3)),
        compiler_params=_params(("parallel",)),
        name="sample_mlstm",
    )(q, k, v, mo, li, lf, b_gate[:heads].reshape(heads, 1), b_gate[heads:].reshape(heads, 1),
      g_mlstm.reshape(heads, HEAD_DIM), state_c.reshape(nb, qw, HEAD_DIM), state_n.reshape(nb, qw),
      state_m.reshape(nb, heads, 1))


def _tile(m, cap):
    return min(m, cap)


def kernel(x_prompt, x_sample, mem_prompt, cache_win_k, cache_win_v, state_mlstm_C, state_mlstm_n,
           state_mlstm_m, cache_mem_k, cache_mem_v, rel_bias, g_pre_mix, g_post_mix, w_in, b_gate,
           g_mlstm, w_out, g_pre_xatt, g_post_xatt, g_mem, w_xq, w_xkv, w_xo, g_pre_mlp, g_post_mlp,
           w_up, w_down):
    depth = w_in.shape[0]
    assert depth == 1 and x_prompt.shape[0] == 1 and x_sample.shape[1] == 1
    _, seq, d_model = x_prompt.shape
    n_dec = x_sample.shape[0]
    att_heads = cache_win_k.shape[3]
    ml_heads = state_mlstm_C.shape[2]
    x_heads = cache_mem_k.shape[3]
    att_w = att_heads * HEAD_DIM
    ml_w = ml_heads * HEAD_DIM
    x_w = x_heads * HEAD_DIM
    n_mem = mem_prompt.shape[1]
    keep = cache_win_k.shape[2]
    n_main = 3 * att_w + 2 * ml_heads * MLSTM_DK + 2 * ml_w
    assert att_w == ml_w == 1024 and n_main % 1024 == 0 and seq >= keep

    def row(g):
        return g[0].reshape(1, -1)

    w6 = w_in[0, :, :n_main].astype(BF16)
    wg = jnp.zeros((d_model, LANES), F32).at[:, :2 * ml_heads].set(w_in[0, :, n_main:]).astype(BF16)
    w_att = w_out[0, :att_w].astype(BF16)
    w_ml = w_out[0, att_w:].astype(BF16)
    wq = w_xq[0].astype(BF16)
    wkv = w_xkv[0].astype(BF16)
    wo = w_xo[0].astype(BF16)
    wu = w_up[0].astype(BF16)
    wd = w_down[0].astype(BF16)
    rel_flat = rel_bias.reshape(-1)

    def tail(x1, qx_att, mem_attend, tm):
        x2 = _proj_post(mem_attend(qx_att), x1, wo, row(g_post_xatt), tm)
        return _mlp(x2, row(g_pre_mlp), wu, wd, row(g_post_mlp), tm, 1024)

    xp = x_prompt[0]
    mem_kv = _norm_matmul(mem_prompt[0], row(g_mem), wkv)
    mk_p, mv_p = mem_kv[:, :x_w], mem_kv[:, x_w:]
    z6, kv32, gates, *zd = _in_proj(xp, row(g_pre_mix), w6, wg, _tile(seq, 512), DILATIONS[1:])
    zd = [z6.reshape(z6.shape[0], 1, seq, att_w)] + zd
    parts = [_dil_attn(z, rel_flat, d, att_heads) for z, d in zip(zd, DILATIONS)]
    m_out, c_p, n_p, m_p = _mlstm_prompt(z6, gates, b_gate[0], g_mlstm[0], ml_heads)
    x1, qx = _out_proj(tuple(p[0] for p in parts) + tuple(p[1] for p in parts), m_out, xp, w_att, w_ml,
                       row(g_post_mix), row(g_pre_xatt), wq, _tile(seq, 256), att_heads)
    tm = _tile(seq, 512)
    y_prompt = tail(x1, qx, lambda q: _xattn(q, mk_p, mv_p, tm, x_heads), tm)

    xs = x_sample[:, 0]
    z6s, kv32s, gates_s = _in_proj(xs, row(g_pre_mix), w6, wg, n_dec, ())
    per_head = (n_dec, att_heads, HEAD_DIM)
    att_s = _sample_win_attn(z6s[0].astype(F32).reshape(per_head), kv32s[0].reshape(per_head),
                             kv32s[1].reshape(per_head), cache_win_k[0], cache_win_v[0], rel_bias, 4)
    mo_s, c_s, n_s, m_s = _sample_mlstm(z6s, gates_s, b_gate[0], g_mlstm[0], state_mlstm_C[0],
                                        state_mlstm_n[0], state_mlstm_m[0], ml_heads, 8)
    x1s, qxs = _out_proj((att_s.reshape(n_dec, att_w).astype(BF16),), mo_s.reshape(n_dec, ml_w).astype(BF16),
                         xs, w_att, w_ml, row(g_post_mix), row(g_pre_xatt), wq, n_dec, att_heads)

    def sample_mem_attend(q):
        q3 = q.astype(F32).reshape(n_dec, x_heads, HEAD_DIM)
        return _sample_xattn(q3, cache_mem_k[0], cache_mem_v[0], 4).reshape(n_dec, x_w).astype(BF16)

    y_sample = tail(x1s, qxs, sample_mem_attend, n_dec)

    dk = MLSTM_DK
    return (y_prompt[None],
            y_sample[:, None],
            kv32[0, seq - keep:].reshape(1, 1, keep, att_heads, HEAD_DIM),
            kv32[1, seq - keep:].reshape(1, 1, keep, att_heads, HEAD_DIM),
            c_p.reshape(1, 1, ml_heads, dk, HEAD_DIM),
            n_p.reshape(1, 1, ml_heads, dk),
            m_p[:, 0].reshape(1, 1, ml_heads),
            mk_p.reshape(1, 1, n_mem, x_heads, HEAD_DIM),
            mv_p.reshape(1, 1, n_mem, x_heads, HEAD_DIM),
            kv32s[0].reshape(1, n_dec, 1, att_heads, HEAD_DIM),
            kv32s[1].reshape(1, n_dec, 1, att_heads, HEAD_DIM),
            c_s.reshape(1, n_dec, ml_heads, dk, HEAD_DIM),
            n_s.reshape(1, n_dec, ml_heads, dk),
            m_s.reshape(1, n_dec, ml_heads))
```

```python
import functools
import math

import numpy as np
import jax
import jax.numpy as jnp
from jax import lax
from jax.experimental import pallas as pl
from jax.experimental.pallas import tpu as pltpu

F32 = jnp.float32
BF16 = jnp.bfloat16

EPS = 1e-6
HEAD_DIM = 128
MLSTM_DK = 64
LANES = 128
DILATIONS = (1, 4, 16)
SUB_WINDOW = 128
ATT_BLOCK = 128
MLSTM_CHUNK = 128
REL_BUCKETS = 32
REL_MAX_DIST = 2048
NEG = -1e30
ATT_SCALE = HEAD_DIM ** -0.5
QK_SCALE = MLSTM_DK ** -0.5
VMEM_LIMIT = 56 * 2 ** 20


def _params(semantics):
    return pltpu.CompilerParams(dimension_semantics=semantics, vmem_limit_bytes=VMEM_LIMIT)


def _rms(xf, g):
    ms = jnp.mean(xf * xf, axis=-1, keepdims=True)
    return xf * lax.rsqrt(ms + EPS) * g


def _dot(a, b):
    return jnp.dot(a, b, preferred_element_type=F32)


def _dot_nt(a, b):
    return lax.dot_general(a, b, (((1,), (1,)), ((), ())), preferred_element_type=F32)


def _dot_tn(a, b):
    return lax.dot_general(a, b, (((0,), (0,)), ((), ())), preferred_element_type=F32)


def _log_sigmoid(x):
    return jnp.minimum(x, 0.0) - jnp.log(1.0 + jnp.exp(-jnp.abs(x)))


def _rel_bucket(dist):
    dist = np.asarray(dist)
    exact = REL_BUCKETS // 2
    far = exact + (np.log(np.maximum(dist, exact) / exact) / math.log(REL_MAX_DIST / exact)
                   * (REL_BUCKETS - exact)).astype(np.int32)
    return np.where(dist < exact, dist, np.minimum(far, REL_BUCKETS - 1)).astype(np.int32)


def _norm_matmul_kernel(x_ref, g_ref, w_ref, o_ref):
    xn = _rms(x_ref[...], g_ref[...]).astype(BF16)
    o_ref[...] = _dot(xn, w_ref[...])


def _norm_matmul(x, g, w):
    m, d = x.shape
    n = w.shape[1]
    return pl.pallas_call(
        _norm_matmul_kernel,
        out_shape=jax.ShapeDtypeStruct((m, n), F32),
        grid=(1,),
        in_specs=[pl.BlockSpec((m, d), lambda i: (0, 0)),
                  pl.BlockSpec((1, d), lambda i: (0, 0)),
                  pl.BlockSpec((d, n), lambda i: (0, 0))],
        out_specs=pl.BlockSpec((m, n), lambda i: (0, 0)),
        compiler_params=_params(("arbitrary",)),
        name="norm_matmul",
    )(x, g, w)


def _in_proj_kernel(dilated, x_ref, g_ref, w_ref, wg_ref, wgt_ref, *rest):
    if dilated:
        (wkt_ref, z_ref, kv_ref, gate_ref, gate_t_ref, kt_ref, zd_mid, zd_wide,
         xn_ref, slab, slab_mid) = rest
    else:
        z_ref, kv_ref, gate_ref, gate_t_ref, xn_ref = rest
    j = pl.program_id(1)

    @pl.when(j == 0)
    def _():
        xn = _rms(x_ref[...], g_ref[...]).astype(BF16)
        xn_ref[...] = xn
        gate_ref[...] = _dot(xn, wg_ref[...])
        gate_t_ref[...] = _dot_nt(wgt_ref[...], xn)
        if dilated:
            kt_ref[...] = _dot_nt(wkt_ref[...], xn).astype(BF16)

    acc = _dot(xn_ref[...], w_ref[...])
    z_ref[...] = acc.astype(BF16)

    @pl.when((j == 1) | (j == 2))
    def _():
        kv_ref[...] = acc

    if dilated:
        tm, tn = acc.shape
        mid = DILATIONS[1]
        ratio = DILATIONS[2] // mid

        @pl.when(j <= 2)
        def _():
            for h in range(tn // LANES):
                hl = slice(h * LANES, (h + 1) * LANES)
                slab[h] = acc[:, hl]
                for r in range(mid):
                    rows = slab[h, pl.ds(r, tm // mid, stride=mid), :]
                    zd_mid[r, :, hl] = rows.astype(BF16)
                    slab_mid[h, r] = rows
                for r in range(mid):
                    for q in range(ratio):
                        rows = slab_mid[h, r, pl.ds(q, tm // (mid * ratio), stride=ratio), :]
                        zd_wide[r + mid * q, :, hl] = rows.astype(BF16)


def _in_proj(x, g, w6, wg, tm, w_kt=None):
    m, d = x.shape
    tn = 1024
    nblk = w6.shape[1] // tn
    n_gate = wg.shape[1]
    wg_pad = jnp.zeros((d, LANES), BF16).at[:, :n_gate].set(wg)
    dilated = w_kt is not None
    dils = DILATIONS[1:] if dilated else ()
    assert DILATIONS[2] % DILATIONS[1] == 0
    const = lambda i, j: (0, 0)
    qkv = lambda i, j: (jnp.minimum(j, 2), 0, i, 0)
    extra_in, extra_out, extra_out_specs, extra_scratch = [], [], [], []
    if dilated:
        n_k = w_kt.shape[0]
        extra_in = [w_kt]
        extra_out = [jax.ShapeDtypeStruct((n_k, m), BF16)] + [
            jax.ShapeDtypeStruct((3, dl, m // dl, tn), BF16) for dl in dils]
        extra_out_specs = [pl.BlockSpec((n_k, tm), lambda i, j: (0, i))] + [
            pl.BlockSpec((None, dl, tm // dl, tn), qkv) for dl in dils]
        extra_scratch = [pltpu.VMEM((tn // LANES, tm, LANES), F32),
                         pltpu.VMEM((tn // LANES, DILATIONS[1], tm // DILATIONS[1], LANES), F32)]
    return pl.pallas_call(
        functools.partial(_in_proj_kernel, dilated),
        out_shape=tuple([jax.ShapeDtypeStruct((nblk, m, tn), BF16),
                         jax.ShapeDtypeStruct((2, m, tn), F32),
                         jax.ShapeDtypeStruct((m, LANES), F32),
                         jax.ShapeDtypeStruct((n_gate, m), F32)] + extra_out),
        grid=(m // tm, nblk),
        in_specs=[pl.BlockSpec((tm, d), lambda i, j: (i, 0)),
                  pl.BlockSpec((1, d), const),
                  pl.BlockSpec((d, tn), lambda i, j: (0, j)),
                  pl.BlockSpec((d, LANES), const),
                  pl.BlockSpec((n_gate, d), const)]
        + [pl.BlockSpec(w.shape, const) for w in extra_in],
        out_specs=tuple([pl.BlockSpec((None, tm, tn), lambda i, j: (j, i, 0)),
                         pl.BlockSpec((None, tm, tn), lambda i, j: (jnp.clip(j - 1, 0, 1), i, 0)),
                         pl.BlockSpec((tm, LANES), lambda i, j: (i, 0)),
                         pl.BlockSpec((n_gate, tm), lambda i, j: (0, i))] + extra_out_specs),
        scratch_shapes=[pltpu.VMEM((tm, d), BF16)] + extra_scratch,
        compiler_params=_params(("parallel", "arbitrary")),
        name="in_proj_dilated" if dilated else "in_proj",
    )(x, g, w6, wg_pad, wg.T, *extra_in)


def _dil_attn_kernel(heads, d, rb_ref, bkt_c_ref, bkt_p_ref, q_ref, kc_ref, kp_ref, vc_ref, vp_ref,
                     o_ref, lse_ref, bias_c, bias_p, s_scr, p_scr):
    n = pl.program_id(0)
    r = pl.program_id(1)

    @pl.when((n == 0) & (r == 0))
    def _():
        for bkt_ref, dst in ((bkt_c_ref, bias_c), (bkt_p_ref, bias_p)):
            bkt = bkt_ref[...]
            for h in range(heads):
                acc = jnp.full(bkt.shape, NEG, F32)
                for k in range(REL_BUCKETS):
                    acc = jnp.where(bkt == k, rb_ref[k * heads + h], acc)
                dst[h] = acc

    no_prev = n == 0
    blk = ATT_BLOCK
    for h in range(heads):
        sl = slice(h * HEAD_DIM, (h + 1) * HEAD_DIM)
        qh = q_ref[:, sl]
        s_scr[h, :, :blk] = _dot_nt(qh, kc_ref[:, sl]) * ATT_SCALE + bias_c[h]
        sp = _dot_nt(qh, kp_ref[:, sl]) * ATT_SCALE + bias_p[h]
        s_scr[h, :, blk:] = jnp.where(no_prev, NEG, sp)
    lane = lax.broadcasted_iota(jnp.int32, (blk, LANES), 1)
    lse_tile = jnp.zeros((blk, LANES), F32)
    for h in range(heads):
        s = s_scr[h]
        m = jnp.max(s, axis=-1, keepdims=True)
        p_scr[h] = jnp.exp(s - m).astype(BF16)
        lse_tile = jnp.where(lane == h, m, lse_tile)
    rows = pl.ds(r, blk, stride=d)
    ones = jnp.ones((blk, HEAD_DIM), BF16)
    for h in range(heads):
        sl = slice(h * HEAD_DIM, (h + 1) * HEAD_DIM)
        o = (_dot(p_scr[h, :, :blk], jnp.concatenate([vc_ref[:, sl], ones], axis=1))
             + _dot(p_scr[h, :, blk:], jnp.concatenate([vp_ref[:, sl], ones], axis=1)))
        l = o[:, HEAD_DIM:]
        o_ref[h, rows, :] = o[:, :HEAD_DIM] / l
        lse_tile = jnp.where(lane == h, lse_tile + jnp.log(l), lse_tile)
    lse_ref[rows, :] = lse_tile


def _dil_attn(zd, rel_flat, dilation, heads):
    d = dilation
    sub_len, w = zd.shape[2], zd.shape[3]
    s = sub_len * d
    nb = sub_len // ATT_BLOCK
    qi = np.arange(ATT_BLOCK)[:, None]
    ki = np.arange(ATT_BLOCK)[None, :]
    dist_c = qi - ki
    dist_p = ATT_BLOCK + qi - ki
    bkt_c = np.where(dist_c >= 0, _rel_bucket(np.maximum(dist_c, 0) * d), -1).astype(np.int32)
    bkt_p = np.where(dist_p <= SUB_WINDOW, _rel_bucket(dist_p * d), -1).astype(np.int32)
    blk = (None, None, ATT_BLOCK, w)
    const = lambda n, r: (0, 0)

    def cur(c):
        return pl.BlockSpec(blk, lambda n, r: (c, r, n, 0))

    def prev(c):
        return pl.BlockSpec(blk, lambda n, r: (c, r, jnp.maximum(n - 1, 0), 0))

    return pl.pallas_call(
        functools.partial(_dil_attn_kernel, heads, d),
        out_shape=(jax.ShapeDtypeStruct((heads, s, HEAD_DIM), F32),
                   jax.ShapeDtypeStruct((s, LANES), F32)),
        grid=(nb, d),
        in_specs=[pl.BlockSpec(memory_space=pltpu.SMEM),
                  pl.BlockSpec((ATT_BLOCK, ATT_BLOCK), const),
                  pl.BlockSpec((ATT_BLOCK, ATT_BLOCK), const),
                  cur(0), cur(1), prev(1), cur(2), prev(2)],
        out_specs=(pl.BlockSpec((heads, ATT_BLOCK * d, HEAD_DIM), lambda n, r: (0, n, 0)),
                   pl.BlockSpec((ATT_BLOCK * d, LANES), lambda n, r: (n, 0))),
        scratch_shapes=[pltpu.VMEM((heads, ATT_BLOCK, ATT_BLOCK), F32),
                        pltpu.VMEM((heads, ATT_BLOCK, ATT_BLOCK), F32),
                        pltpu.VMEM((heads, ATT_BLOCK, 2 * ATT_BLOCK), F32),
                        pltpu.VMEM((heads, ATT_BLOCK, 2 * ATT_BLOCK), BF16)],
        compiler_params=_params(("arbitrary", "arbitrary")),
        name=f"dil_attn_d{d}",
    )(rel_flat, jnp.asarray(bkt_c), jnp.asarray(bkt_p), zd, zd, zd, zd, zd)


def _split_bf16(x):
    hi = x.astype(BF16)
    return hi, (x - hi.astype(F32)).astype(BF16)


def _mlstm_kernel(heads, qk_ref, kt_ref, v_ref, mo_ref, gt_ref, bgc_ref, gm_ref,
                  ce0_ref, m0_ref, out_ref, ce_out, m_out, ce_s, m_s, sqk_s, tile_s):
    step = pl.program_id(0)
    t = MLSTM_CHUNK
    qw = heads * MLSTM_DK
    dk = MLSTM_DK

    @pl.when(step == 0)
    def _():
        ce_s[...] = ce0_ref[...]
        m_s[...] = m0_ref[...]

    row = lax.broadcasted_iota(jnp.int32, (t, t), 0)
    col = lax.broadcasted_iota(jnp.int32, (t, t), 1)
    causal = col <= row
    eye = col == row
    upper = jnp.where(row <= col, 1.0, 0.0).astype(BF16)
    lane = lax.broadcasted_iota(jnp.int32, (1, LANES), 1)
    lane_h = lax.broadcasted_iota(jnp.int32, (heads, t), 1)
    blk_r = lax.broadcasted_iota(jnp.int32, (2 * t, 2 * t), 0) // t
    blk_c = lax.broadcasted_iota(jnp.int32, (2 * t, 2 * t), 1) // t
    ones_bd = jnp.where(blk_r == blk_c, 1.0, 0.0).astype(BF16)
    ones = jnp.ones((t, HEAD_DIM), BF16)

    gt = gt_ref[...] + bgc_ref[...]
    hi, lo = _split_bf16(_log_sigmoid(gt[heads:]))
    b2 = _dot(jnp.concatenate([hi, lo], axis=0), upper)
    b = b2[:heads] + b2[heads:]
    c = gt[:heads] - b
    m_prev = m_s[...]
    cmax = c
    for shift in (1, 2, 4, 8, 16, 32, 64):
        cmax = jnp.maximum(cmax, jnp.where(lane_h >= shift, pltpu.roll(cmax, shift, axis=1), NEG))
    a = jnp.maximum(cmax, m_prev)
    m_t = b + a
    a_last = a[:, t - 1:t]
    b_last = b[:, t - 1:t]
    m_new = b_last + a_last
    w_k = jnp.exp(c - a_last)
    decay = jnp.exp(b_last + m_prev - m_new)
    a_hi, a_lo = _split_bf16(a)
    mt_hi, mt_lo = _split_bf16(m_t)

    def diag(x_row):
        return jnp.where(eye, x_row.astype(F32), 0.0).astype(BF16)

    qms, dens = [], []
    for h in range(heads):
        pair, half = divmod(h, 2)
        hs = slice(h, h + 1)
        tile = (_dot(jnp.concatenate([diag(a_hi[hs]), diag(mt_hi[hs])], axis=1), ones_bd)
                + _dot(jnp.concatenate([diag(a_lo[hs]), diag(mt_lo[hs])], axis=1), ones_bd))
        tile_s[h] = tile
        w_intra = jnp.exp(jnp.where(causal, c[hs] - tile[:, :t], NEG))
        mine = (lane >= half * dk) & (lane < (half + 1) * dk)
        qm = jnp.where(mine, qk_ref[:, pair * LANES:(pair + 1) * LANES], 0.0) * QK_SCALE
        km = jnp.where(mine, qk_ref[:, qw + pair * LANES:qw + (pair + 1) * LANES], 0.0)
        sqk = _dot_nt(qm, km) * w_intra
        sqk_s[h] = sqk.astype(BF16)
        dens.append(jnp.sum(sqk, axis=-1, keepdims=True))
        qms.append(qm)

    for h in range(heads):
        pair = h // 2
        sl = slice(h * HEAD_DIM, (h + 1) * HEAD_DIM)
        tile = tile_s[h]
        w_prev = jnp.exp(m_prev[h:h + 1, :] - tile[:, :t])
        q_ce = _dot(qms[h], ce_s[pair * LANES:(pair + 1) * LANES, :].astype(BF16))
        num = _dot(sqk_s[h], v_ref[:, sl]) + w_prev * q_ce[:, :HEAD_DIM]
        den = dens[h] + w_prev * q_ce[:, HEAD_DIM:]
        hh = num / jnp.maximum(jnp.abs(den), jnp.exp(-tile[:, t:]))
        hn = hh * lax.rsqrt(jnp.mean(hh * hh, axis=-1, keepdims=True) + EPS)
        gate = jax.nn.sigmoid(mo_ref[:, sl].astype(F32))
        out_ref[:, sl] = (hn * gm_ref[:, sl] * gate).astype(out_ref.dtype)

    for h in range(heads):
        hs = slice(h * dk, (h + 1) * dk)
        kw = (kt_ref[hs, :].astype(F32) * w_k[h:h + 1, :]).astype(BF16)
        v_ext = jnp.concatenate([v_ref[:, h * HEAD_DIM:(h + 1) * HEAD_DIM], ones], axis=1)
        dec = jnp.concatenate([decay[h:h + 1, :], decay[h:h + 1, :]], axis=1)
        ce_s[hs, :] = dec * ce_s[hs, :] + _dot(kw, v_ext)
    m_s[...] = jnp.broadcast_to(m_new, (heads, LANES))

    @pl.when(step == pl.num_programs(0) - 1)
    def _():
        ce_out[...] = ce_s[...]
        m_out[...] = m_s[...]


def _mlstm_prompt(z6, k_t, gates_t, b_gate, g_mlstm, heads):
    _, s, w = z6.shape
    t = MLSTM_CHUNK
    qw = heads * MLSTM_DK
    ce0 = jnp.zeros((qw, 2 * HEAD_DIM), F32)
    m0 = jnp.zeros((heads, LANES), F32)
    const = lambda c: (0, 0)
    return pl.pallas_call(
        functools.partial(_mlstm_kernel, heads),
        out_shape=(jax.ShapeDtypeStruct((s, w), BF16),
                   jax.ShapeDtypeStruct((qw, 2 * HEAD_DIM), F32),
                   jax.ShapeDtypeStruct((heads, LANES), F32)),
        grid=(s // t,),
        in_specs=[pl.BlockSpec((None, t, w), lambda c: (3, c, 0)),
                  pl.BlockSpec((qw, t), lambda c: (0, c)),
                  pl.BlockSpec((None, t, w), lambda c: (4, c, 0)),
                  pl.BlockSpec((None, t, w), lambda c: (5, c, 0)),
                  pl.BlockSpec((2 * heads, t), lambda c: (0, c)),
                  pl.BlockSpec((2 * heads, 1), const),
                  pl.BlockSpec((1, w), const),
                  pl.BlockSpec((qw, 2 * HEAD_DIM), const),
                  pl.BlockSpec((heads, LANES), const)],
        out_specs=(pl.BlockSpec((t, w), lambda c: (c, 0)),
                   pl.BlockSpec((qw, 2 * HEAD_DIM), const),
                   pl.BlockSpec((heads, LANES), const)),
        scratch_shapes=[pltpu.VMEM((qw, 2 * HEAD_DIM), F32),
                        pltpu.VMEM((heads, LANES), F32),
                        pltpu.VMEM((heads, t, t), BF16),
                        pltpu.VMEM((heads, t, 2 * t), F32)],
        compiler_params=_params(("arbitrary",)),
        name="mlstm_prompt",
    )(z6, k_t, z6, z6, gates_t, b_gate.reshape(2 * heads, 1), g_mlstm.reshape(1, w), ce0, m0)


def _out_proj_kernel(merge, heads, *refs):
    if merge:
        (o1, o2, o3, l1, l2, l3, mo_ref, x_ref, wa_ref, wm_ref, gpost_ref, gpre_ref, wq_ref,
         x1_ref, qx_ref, att_s) = refs
        la, lb, lc = l1[...], l2[...], l3[...]
        mx = jnp.maximum(jnp.maximum(la, lb), lc)
        ea, eb, ec = jnp.exp(la - mx), jnp.exp(lb - mx), jnp.exp(lc - mx)
        tot = ea + eb + ec
        wa_, wb_, wc_ = ea / tot, eb / tot, ec / tot
        for h in range(heads):
            sl = slice(h * HEAD_DIM, (h + 1) * HEAD_DIM)
            mix = wa_[:, h:h + 1] * o1[h] + wb_[:, h:h + 1] * o2[h] + wc_[:, h:h + 1] * o3[h]
            att_s[:, sl] = mix.astype(BF16)
        att = att_s[...]
    else:
        att_ref, mo_ref, x_ref, wa_ref, wm_ref, gpost_ref, gpre_ref, wq_ref, x1_ref, qx_ref = refs
        att = att_ref[...]
    y = _dot(att, wa_ref[...]) + _dot(mo_ref[...], wm_ref[...])
    x1 = x_ref[...] + _rms(y, gpost_ref[...])
    x1_ref[...] = x1
    u = _rms(x1, gpre_ref[...]).astype(BF16)
    qx_ref[...] = _dot(u, wq_ref[...]).astype(qx_ref.dtype)


def _out_proj(att_parts, m_out, x, w_att, w_ml, g_post, g_pre, w_xq, tm, heads):
    m, d = x.shape
    wa = w_att.shape[0]
    wm = w_ml.shape[0]
    nq = w_xq.shape[1]
    merge = len(att_parts) == 6
    row = lambda i: (i, 0)
    const = lambda i: (0, 0)
    if merge:
        part_specs = ([pl.BlockSpec((heads, tm, HEAD_DIM), lambda i: (0, i, 0))] * 3
                      + [pl.BlockSpec((tm, LANES), row)] * 3)
    else:
        part_specs = [pl.BlockSpec((tm, wa), row)]
    return pl.pallas_call(
        functools.partial(_out_proj_kernel, merge, heads),
        out_shape=(jax.ShapeDtypeStruct((m, d), F32), jax.ShapeDtypeStruct((m, nq), BF16)),
        grid=(m // tm,),
        in_specs=part_specs + [pl.BlockSpec((tm, wm), row),
                               pl.BlockSpec((tm, d), row),
                               pl.BlockSpec((wa, d), const),
                               pl.BlockSpec((wm, d), const),
                               pl.BlockSpec((1, d), const),
                               pl.BlockSpec((1, d), const),
                               pl.BlockSpec((d, nq), const)],
        out_specs=(pl.BlockSpec((tm, d), row), pl.BlockSpec((tm, nq), row)),
        scratch_shapes=[pltpu.VMEM((tm, wa), BF16)] if merge else [],
        compiler_params=_params(("parallel",)),
        name="out_proj_merge" if merge else "out_proj",
    )(*att_parts, m_out, x, w_att, w_ml, g_post, g_pre, w_xq)


def _xattn_kernel(heads, q_ref, mk_ref, mv_ref, o_ref):
    for h in range(heads):
        sl = slice(h * HEAD_DIM, (h + 1) * HEAD_DIM)
        s = _dot_nt(q_ref[:, sl], mk_ref[:, sl].astype(BF16)) * ATT_SCALE
        m = jnp.max(s, axis=-1, keepdims=True)
        p = jnp.exp(s - m)
        l = jnp.sum(p, axis=-1, keepdims=True)
        o = _dot(p.astype(BF16), mv_ref[:, sl].astype(BF16))
        o_ref[:, sl] = (o / l).astype(o_ref.dtype)


def _xattn(qx, mem_k, mem_v, tm, heads):
    m, w = qx.shape
    n_mem = mem_k.shape[0]
    return pl.pallas_call(
        functools.partial(_xattn_kernel, heads),
        out_shape=jax.ShapeDtypeStruct((m, w), BF16),
        grid=(m // tm,),
        in_specs=[pl.BlockSpec((tm, w), lambda i: (i, 0)),
                  pl.BlockSpec((n_mem, w), lambda i: (0, 0)),
                  pl.BlockSpec((n_mem, w), lambda i: (0, 0))],
        out_specs=pl.BlockSpec((tm, w), lambda i: (i, 0)),
        compiler_params=_params(("parallel",)),
        name="xattn_prompt",
    )(qx, mem_k, mem_v)


def _proj_post_kernel(a_ref, x_ref, w_ref, g_ref, y_ref):
    y_ref[...] = x_ref[...] + _rms(_dot(a_ref[...], w_ref[...]), g_ref[...])


def _proj_post(a, x, w, g, tm):
    m, d = x.shape
    k = a.shape[1]
    return pl.pallas_call(
        _proj_post_kernel,
        out_shape=jax.ShapeDtypeStruct((m, d), F32),
        grid=(m // tm,),
        in_specs=[pl.BlockSpec((tm, k), lambda i: (i, 0)),
                  pl.BlockSpec((tm, d), lambda i: (i, 0)),
                  pl.BlockSpec((k, d), lambda i: (0, 0)),
                  pl.BlockSpec((1, d), lambda i: (0, 0))],
        out_specs=pl.BlockSpec((tm, d), lambda i: (i, 0)),
        compiler_params=_params(("parallel",)),
        name="proj_post",
    )(a, x, w, g)


def _mlp_kernel(x_ref, gpre_ref, wu_ref, wd_ref, gpost_ref, y_ref, xn_s, acc_s):
    j = pl.program_id(1)

    @pl.when(j == 0)
    def _():
        xn_s[...] = _rms(x_ref[...], gpre_ref[...]).astype(BF16)
        acc_s[...] = jnp.zeros(acc_s.shape, F32)

    hidden = jnp.square(jnp.maximum(_dot(xn_s[...], wu_ref[...]), 0.0))
    acc_s[...] += _dot(hidden.astype(BF16), wd_ref[...])

    @pl.when(j == pl.num_programs(1) - 1)
    def _():
        y_ref[...] = x_ref[...] + _rms(acc_s[...], gpost_ref[...])


def _mlp(x, g_pre, w_up, w_down, g_post, tm, tf):
    m, d = x.shape
    ff = w_up.shape[1]
    return pl.pallas_call(
        _mlp_kernel,
        out_shape=jax.ShapeDtypeStruct((m, d), F32),
        grid=(m // tm, ff // tf),
        in_specs=[pl.BlockSpec((tm, d), lambda i, j: (i, 0)),
                  pl.BlockSpec((1, d), lambda i, j: (0, 0)),
                  pl.BlockSpec((d, tf), lambda i, j: (0, j)),
                  pl.BlockSpec((tf, d), lambda i, j: (j, 0)),
                  pl.BlockSpec((1, d), lambda i, j: (0, 0))],
        out_specs=pl.BlockSpec((tm, d), lambda i, j: (i, 0)),
        scratch_shapes=[pltpu.VMEM((tm, d), BF16), pltpu.VMEM((tm, d), F32)],
        compiler_params=_params(("parallel", "arbitrary")),
        name="mlp",
    )(x, g_pre, w_up, w_down, g_post)


def _single_query_attention(q, groups, new):
    scores = []
    for k3, _, b3 in groups:
        s = jnp.sum(k3 * q[None], axis=-1, keepdims=True) * ATT_SCALE
        scores.append(s if b3 is None else s + b3)
    mx = jnp.max(scores[0], axis=0)
    for s in scores[1:]:
        mx = jnp.maximum(mx, jnp.max(s, axis=0))
    if new is not None:
        k_new, v_new, b_new, count = new
        s_new = jnp.sum(k_new * q, axis=-1, keepdims=True) * ATT_SCALE + b_new
        mx = jnp.maximum(mx, s_new)
    den = jnp.zeros(mx.shape, F32)
    acc = jnp.zeros(q.shape, F32)
    for s, (_, v3, _) in zip(scores, groups):
        p = jnp.exp(s - mx[None])
        den = den + jnp.sum(p, axis=0)
        acc = acc + jnp.sum(p * v3, axis=0)
    if new is not None:
        p_new = jnp.exp(s_new - mx) * count
        den = den + p_new
        acc = acc + p_new * v_new
    return acc / den


def _sample_win_attn_kernel(tb, rel_ref, bkt_ref, q_ref, kn_ref, vn_ref,
                            k1, k4, k16, v1, v4, v16, o_ref, bias_s):
    @pl.when(pl.program_id(0) == 0)
    def _():
        for p in range(len(DILATIONS)):
            bkt = bkt_ref[p]
            acc = jnp.zeros(bkt.shape, F32)
            for k in range(REL_BUCKETS):
                acc = jnp.where(bkt == k, rel_ref[k][None], acc)
            bias_s[p] = acc

    for b in range(tb):
        groups = [(k[b], v[b], bias_s[p]) for p, (k, v) in enumerate(((k1, v1), (k4, v4), (k16, v16)))]
        new = (kn_ref[b], vn_ref[b], rel_ref[0], float(len(DILATIONS)))
        o_ref[b] = _single_query_attention(q_ref[b], groups, new)


def _sample_win_attn(q, k_new, v_new, cache_k, cache_v, rel_bias, tb):
    nb, heads, hd = q.shape
    buf = cache_k.shape[1]
    r = SUB_WINDOW
    assert buf == r * DILATIONS[-1], "window buffer must hold exactly the widest dilated pattern"
    j = r - np.arange(r)
    bkt = np.stack([_rel_bucket(j * d) for d in DILATIONS]).astype(np.int32)
    bkt = np.broadcast_to(bkt[:, :, None, None], (len(DILATIONS), r, heads, 1))
    views, specs = [], []
    for cache in (cache_k, cache_v):
        for d in DILATIONS:
            views.append(cache.reshape(nb, buf // d, d, heads, hd))
            specs.append(pl.BlockSpec((tb, r, None, heads, hd), functools.partial(
                lambda i, blk: (i, blk, 0, 0, 0), blk=buf // d // r - 1)))
    tok = pl.BlockSpec((tb, heads, hd), lambda i: (i, 0, 0))
    return pl.pallas_call(
        functools.partial(_sample_win_attn_kernel, tb),
        out_shape=jax.ShapeDtypeStruct((nb, heads, hd), F32),
        grid=(nb // tb,),
        in_specs=[pl.BlockSpec((REL_BUCKETS, heads, 1), lambda i: (0, 0, 0)),
                  pl.BlockSpec((len(DILATIONS), r, heads, 1), lambda i: (0, 0, 0, 0)),
                  tok, tok, tok] + specs,
        out_specs=tok,
        scratch_shapes=[pltpu.VMEM((len(DILATIONS), r, heads, 1), F32)],
        compiler_params=_params(("arbitrary",)),
        name="sample_win_attn",
    )(rel_bias.reshape(REL_BUCKETS, heads, 1), jnp.asarray(bkt), q, k_new, v_new, *views)


def _sample_xattn_kernel(tb, q_ref, k_ref, v_ref, o_ref):
    for b in range(tb):
        o_ref[b] = _single_query_attention(q_ref[b], [(k_ref[b], v_ref[b], None)], None)


def _sample_xattn(qx, mem_k, mem_v, tb):
    nb, heads, hd = qx.shape
    n_mem = mem_k.shape[1]
    tok = pl.BlockSpec((tb, heads, hd), lambda i: (i, 0, 0))
    mem = pl.BlockSpec((tb, n_mem, heads, hd), lambda i: (i, 0, 0, 0))
    return pl.pallas_call(
        functools.partial(_sample_xattn_kernel, tb),
        out_shape=jax.ShapeDtypeStruct((nb, heads, hd), F32),
        grid=(nb // tb,),
        in_specs=[tok, mem, mem],
        out_specs=tok,
        compiler_params=_params(("parallel",)),
        name="sample_xattn",
    )(qx, mem_k, mem_v)


def _sample_mlstm_kernel(heads, tb, q_ref, k_ref, v_ref, mo_ref, li_ref, lf_ref, bi_ref, bf_ref,
                         gm_ref, c_ref, n_ref, m_ref, out_ref, c_out, n_out, m_out):
    qw = heads * MLSTM_DK
    hrow = lax.broadcasted_iota(jnp.int32, (LANES, qw), 0)
    hcol = lax.broadcasted_iota(jnp.int32, (LANES, qw), 1) // MLSTM_DK
    diag = hrow == hcol
    rows8 = slice(0, heads)
    for b in range(tb):
        q_bd = jnp.where(diag, q_ref[b:b + 1, :], 0.0) * QK_SCALE
        k_bd = jnp.where(diag, k_ref[b:b + 1, :], 0.0)
        c_b = c_ref[b]
        n_row = n_ref[b:b + 1, :]
        li = li_ref[b] + bi_ref[...]
        lf = _log_sigmoid(lf_ref[b] + bf_ref[...])
        m_prev = m_ref[b]
        qk = jnp.sum(q_bd * k_ref[b:b + 1, :], axis=-1, keepdims=True)[rows8]
        qn = jnp.sum(q_bd * n_row, axis=-1, keepdims=True)[rows8]
        q_c = _dot(q_bd.astype(BF16), c_b.astype(BF16))[rows8]
        inter = lf + m_prev
        m_t = jnp.maximum(inter, li)
        w_intra = jnp.exp(li - m_t)
        w_prev = jnp.exp(inter - m_t)
        sqk = qk * w_intra
        v_b = v_ref[b]
        num = sqk * v_b + w_prev * q_c
        den = sqk + w_prev * qn
        hh = num / jnp.maximum(jnp.abs(den), jnp.exp(-m_t))
        hn = hh * lax.rsqrt(jnp.mean(hh * hh, axis=-1, keepdims=True) + EPS)
        out_ref[b] = hn * gm_ref[...] * jax.nn.sigmoid(mo_ref[b])
        wv = jnp.concatenate([w_intra * v_b, jnp.zeros((LANES - heads, HEAD_DIM), F32)], axis=0)
        d_c = _dot_tn(k_bd.astype(BF16), wv.astype(BF16))
        for h in range(heads):
            hs = slice(h * MLSTM_DK, (h + 1) * MLSTM_DK)
            c_out[b, hs, :] = w_prev[h:h + 1, :] * c_b[hs, :] + d_c[hs, :]
        k_w = jnp.sum(k_bd[rows8] * w_intra, axis=0, keepdims=True)
        dec_row = jnp.sum(jnp.where(diag[rows8], w_prev, 0.0), axis=0, keepdims=True)
        n_out[b:b + 1, :] = dec_row * n_row + k_w
        m_out[b] = m_t


def _sample_mlstm(z6, gates, b_gate, g_mlstm, state_c, state_n, state_m, heads, tb):
    _, nb, w = z6.shape
    qw = heads * MLSTM_DK
    q = z6[3, :, :qw].astype(F32)
    k = z6[3, :, qw:].astype(F32)
    v = z6[4].astype(F32).reshape(nb, heads, HEAD_DIM)
    mo = z6[5].astype(F32).reshape(nb, heads, HEAD_DIM)
    li = gates[:, :heads].reshape(nb, heads, 1)
    lf = gates[:, heads:2 * heads].reshape(nb, heads, 1)
    row = lambda i: (i, 0)
    blk3 = lambda i: (i, 0, 0)
    const = lambda i: (0, 0)
    return pl.pallas_call(
        functools.partial(_sample_mlstm_kernel, heads, tb),
        out_shape=(jax.ShapeDtypeStruct((nb, heads, HEAD_DIM), F32),
                   jax.ShapeDtypeStruct((nb, qw, HEAD_DIM), F32),
                   jax.ShapeDtypeStruct((nb, qw), F32),
                   jax.ShapeDtypeStruct((nb, heads, 1), F32)),
        grid=(nb // tb,),
        in_specs=[pl.BlockSpec((tb, qw), row),
                  pl.BlockSpec((tb, qw), row),
                  pl.BlockSpec((tb, heads, HEAD_DIM), blk3),
                  pl.BlockSpec((tb, heads, HEAD_DIM), blk3),
                  pl.BlockSpec((tb, heads, 1), blk3),
                  pl.BlockSpec((tb, heads, 1), blk3),
                  pl.BlockSpec((heads, 1), const),
                  pl.BlockSpec((heads, 1), const),
                  pl.BlockSpec((heads, HEAD_DIM), const),
                  pl.BlockSpec((tb, qw, HEAD_DIM), blk3),
                  pl.BlockSpec((tb, qw), row),
                  pl.BlockSpec((tb, heads, 1), blk3)],
        out_specs=(pl.BlockSpec((tb, heads, HEAD_DIM), blk3),
                   pl.BlockSpec((tb, qw, HEAD_DIM), blk3),
                   pl.BlockSpec((tb, qw), row),
                   pl.BlockSpec((tb, heads, 1), blk3)),
        compiler_params=_params(("parallel",)),
        name="sample_mlstm",
    )(q, k, v, mo, li, lf, b_gate[:heads].reshape(heads, 1), b_gate[heads:].reshape(heads, 1),
      g_mlstm.reshape(heads, HEAD_DIM), state_c.reshape(nb, qw, HEAD_DIM), state_n.reshape(nb, qw),
      state_m.reshape(nb, heads, 1))


def _tile(m, cap):
    return min(m, cap)


def kernel(x_prompt, x_sample, mem_prompt, cache_win_k, cache_win_v, state_mlstm_C, state_mlstm_n,
           state_mlstm_m, cache_mem_k, cache_mem_v, rel_bias, g_pre_mix, g_post_mix, w_in, b_gate,
           g_mlstm, w_out, g_pre_xatt, g_post_xatt, g_mem, w_xq, w_xkv, w_xo, g_pre_mlp, g_post_mlp,
           w_up, w_down):
    depth = w_in.shape[0]
    assert depth == 1 and x_prompt.shape[0] == 1 and x_sample.shape[1] == 1
    _, seq, d_model = x_prompt.shape
    n_dec = x_sample.shape[0]
    att_heads = cache_win_k.shape[3]
    ml_heads = state_mlstm_C.shape[2]
    x_heads = cache_mem_k.shape[3]
    att_w = att_heads * HEAD_DIM
    ml_w = ml_heads * HEAD_DIM
    x_w = x_heads * HEAD_DIM
    n_mem = mem_prompt.shape[1]
    keep = cache_win_k.shape[2]
    n_main = 3 * att_w + 2 * ml_heads * MLSTM_DK + 2 * ml_w
    assert att_w == ml_w == 1024 and n_main % 1024 == 0 and seq >= keep

    def row(g):
        return g[0].reshape(1, -1)

    w6 = w_in[0, :, :n_main].astype(BF16)
    wg = w_in[0, :, n_main:].astype(BF16)
    w_att = w_out[0, :att_w].astype(BF16)
    w_ml = w_out[0, att_w:].astype(BF16)
    wq = w_xq[0].astype(BF16)
    wkv = w_xkv[0].astype(BF16)
    wo = w_xo[0].astype(BF16)
    wu = w_up[0].astype(BF16)
    wd = w_down[0].astype(BF16)
    rel_flat = rel_bias.reshape(-1)

    def tail(x1, qx_att, mem_attend, tm):
        x2 = _proj_post(mem_attend(qx_att), x1, wo, row(g_post_xatt), tm)
        return _mlp(x2, row(g_pre_mlp), wu, wd, row(g_post_mlp), tm, 1024)

    xp = x_prompt[0]
    mem_kv = _norm_matmul(mem_prompt[0], row(g_mem), wkv)
    mk_p, mv_p = mem_kv[:, :x_w], mem_kv[:, x_w:]
    k_lo = 3 * att_w + ml_heads * MLSTM_DK
    w_kt = w_in[0, :, k_lo:k_lo + ml_heads * MLSTM_DK].T.astype(BF16)
    z6, kv32, _, gates_t, k_t, *zd = _in_proj(xp, row(g_pre_mix), w6, wg, _tile(seq, 512), w_kt)
    zd = [z6.reshape(z6.shape[0], 1, seq, att_w)] + zd
    parts = [_dil_attn(z, rel_flat, d, att_heads) for z, d in zip(zd, DILATIONS)]
    m_out, ce_p, m_p = _mlstm_prompt(z6, k_t, gates_t, b_gate[0], g_mlstm[0], ml_heads)
    c_p, n_p = ce_p[:, :HEAD_DIM], ce_p[:, HEAD_DIM]
    x1, qx = _out_proj(tuple(p[0] for p in parts) + tuple(p[1] for p in parts), m_out, xp, w_att, w_ml,
                       row(g_post_mix), row(g_pre_xatt), wq, _tile(seq, 256), att_heads)
    tm = _tile(seq, 512)
    y_prompt = tail(x1, qx, lambda q: _xattn(q, mk_p, mv_p, tm, x_heads), tm)

    xs = x_sample[:, 0]
    z6s, kv32s, gates_s, _ = _in_proj(xs, row(g_pre_mix), w6, wg, n_dec)
    per_head = (n_dec, att_heads, HEAD_DIM)
    att_s = _sample_win_attn(z6s[0].astype(F32).reshape(per_head), kv32s[0].reshape(per_head),
                             kv32s[1].reshape(per_head), cache_win_k[0], cache_win_v[0], rel_bias, 4)
    mo_s, c_s, n_s, m_s = _sample_mlstm(z6s, gates_s, b_gate[0], g_mlstm[0], state_mlstm_C[0],
                                        state_mlstm_n[0], state_mlstm_m[0], ml_heads, 8)
    x1s, qxs = _out_proj((att_s.reshape(n_dec, att_w).astype(BF16),), mo_s.reshape(n_dec, ml_w).astype(BF16),
                         xs, w_att, w_ml, row(g_post_mix), row(g_pre_xatt), wq, n_dec, att_heads)

    def sample_mem_attend(q):
        q3 = q.astype(F32).reshape(n_dec, x_heads, HEAD_DIM)
        return _sample_xattn(q3, cache_mem_k[0], cache_mem_v[0], 4).reshape(n_dec, x_w).astype(BF16)

    y_sample = tail(x1s, qxs, sample_mem_attend, n_dec)

    dk = MLSTM_DK
    return (y_prompt[None],
            y_sample[:, None],
            kv32[0, seq - keep:].reshape(1, 1, keep, att_heads, HEAD_DIM),
            kv32[1, seq - keep:].reshape(1, 1, keep, att_heads, HEAD_DIM),
            c_p.reshape(1, 1, ml_heads, dk, HEAD_DIM),
            n_p.reshape(1, 1, ml_heads, dk),
            m_p[:, 0].reshape(1, 1, ml_heads),
            mk_p.reshape(1, 1, n_mem, x_heads, HEAD_DIM),
            mv_p.reshape(1, 1, n_mem, x_heads, HEAD_DIM),
            kv32s[0].reshape(1, n_dec, 1, att_heads, HEAD_DIM),
            kv32s[1].reshape(1, n_dec, 1, att_heads, HEAD_DIM),
            c_s.reshape(1, n_dec, ml_heads, dk, HEAD_DIM),
            n_s.reshape(1, n_dec, ml_heads, dk),
            m_s.reshape(1, n_dec, ml_heads))
```

```python
import functools
import math

import numpy as np
import jax
import jax.numpy as jnp
from jax import lax
from jax.experimental import pallas as pl
from jax.experimental.pallas import tpu as pltpu

F32 = jnp.float32
BF16 = jnp.bfloat16

EPS = 1e-6
HEAD_DIM = 128
MLSTM_DK = 64
LANES = 128
DILATIONS = (1, 4, 16)
SUB_WINDOW = 128
ATT_BLOCK = 128
MLSTM_CHUNK = 128
REL_BUCKETS = 32
REL_MAX_DIST = 2048
NEG = -1e30
ATT_SCALE = HEAD_DIM ** -0.5
QK_SCALE = MLSTM_DK ** -0.5
VMEM_LIMIT = 56 * 2 ** 20


def _params(semantics):
    return pltpu.CompilerParams(dimension_semantics=semantics, vmem_limit_bytes=VMEM_LIMIT)


def _rms(xf, g):
    ms = jnp.mean(xf * xf, axis=-1, keepdims=True)
    return xf * lax.rsqrt(ms + EPS) * g


def _dot(a, b):
    return jnp.dot(a, b, preferred_element_type=F32)


def _dot_nt(a, b):
    return lax.dot_general(a, b, (((1,), (1,)), ((), ())), preferred_element_type=F32)


def _dot_tn(a, b):
    return lax.dot_general(a, b, (((0,), (0,)), ((), ())), preferred_element_type=F32)


def _log_sigmoid(x):
    return jnp.minimum(x, 0.0) - jnp.log(1.0 + jnp.exp(-jnp.abs(x)))


def _rel_bucket(dist):
    dist = np.asarray(dist)
    exact = REL_BUCKETS // 2
    far = exact + (np.log(np.maximum(dist, exact) / exact) / math.log(REL_MAX_DIST / exact)
                   * (REL_BUCKETS - exact)).astype(np.int32)
    return np.where(dist < exact, dist, np.minimum(far, REL_BUCKETS - 1)).astype(np.int32)


def _norm_matmul_kernel(x_ref, g_ref, w_ref, o_ref):
    xn = _rms(x_ref[...], g_ref[...]).astype(BF16)
    o_ref[...] = _dot(xn, w_ref[...])


def _norm_matmul(x, g, w):
    m, d = x.shape
    n = w.shape[1]
    return pl.pallas_call(
        _norm_matmul_kernel,
        out_shape=jax.ShapeDtypeStruct((m, n), F32),
        grid=(1,),
        in_specs=[pl.BlockSpec((m, d), lambda i: (0, 0)),
                  pl.BlockSpec((1, d), lambda i: (0, 0)),
                  pl.BlockSpec((d, n), lambda i: (0, 0))],
        out_specs=pl.BlockSpec((m, n), lambda i: (0, 0)),
        compiler_params=_params(("arbitrary",)),
        name="norm_matmul",
    )(x, g, w)


def _column_order(j):
    return (j % 2) * 3 + j // 2


def _in_proj_kernel(prompt, x_ref, g_ref, w_ref, wg_ref, wgt_ref, *rest):
    if prompt:
        z_ref, kv_ref, gate_ref, gate_t_ref, kt_ref, zd_mid, zd_wide, xn_ref, slab, slab_mid = rest
    else:
        z_ref, kv_ref, gate_ref, gate_t_ref, xn_ref = rest
    j = pl.program_id(1)

    @pl.when(j == 0)
    def _():
        xn = _rms(x_ref[...], g_ref[...]).astype(BF16)
        xn_ref[...] = xn
        gate_ref[...] = _dot(xn, wg_ref[...])
        gate_t_ref[...] = _dot_nt(wgt_ref[...], xn)

    if not prompt:
        acc = _dot(xn_ref[...], w_ref[...])
        z_ref[...] = acc.astype(BF16)

        @pl.when((j == 1) | (j == 2))
        def _():
            kv_ref[...] = acc
        return

    tm, tn = z_ref.shape
    mid = DILATIONS[1]
    ratio = DILATIONS[2] // mid

    @pl.when(j % 2 == 0)
    def _():
        acc = _dot(xn_ref[...], w_ref[...])
        z_ref[...] = acc.astype(BF16)
        for h in range(tn // LANES):
            slab[h] = acc[:, h * LANES:(h + 1) * LANES]

        @pl.when(j >= 2)
        def _():
            kv_ref[...] = acc

    @pl.when(j % 2 == 1)
    def _():
        acc = _dot(xn_ref[...], w_ref[...])
        z_ref[...] = acc.astype(BF16)
        for h in range(tn // LANES):
            hl = slice(h * LANES, (h + 1) * LANES)
            for r in range(mid):
                rows = slab[h, pl.ds(r, tm // mid, stride=mid), :]
                zd_mid[r, :, hl] = rows.astype(BF16)
                slab_mid[h, r] = rows
            for r in range(mid):
                for q in range(ratio):
                    rows = slab_mid[h, r, pl.ds(q, tm // (mid * ratio), stride=ratio), :]
                    zd_wide[r + mid * q, :, hl] = rows.astype(BF16)

        @pl.when(j == 1)
        def _():
            kt_ref[...] = acc[:, tn // 2:].T.astype(BF16)


def _in_proj(x, g, w_all, n_main, tm, prompt):
    m, d = x.shape
    tn = 1024
    nblk = n_main // tn
    wg = w_all[:, n_main:]
    n_gate = wg.shape[1]
    wg_pad = jnp.zeros((d, LANES), BF16).at[:, :n_gate].set(wg)
    dils = DILATIONS[1:] if prompt else ()
    assert DILATIONS[2] % DILATIONS[1] == 0 and nblk == 6
    const = lambda i, j: (0, 0)
    if prompt:
        col = _column_order
        kv_blk = lambda i, j: (jnp.where(j < 4, 0, 1), i, 0)
    else:
        col = lambda j: j
        kv_blk = lambda i, j: (jnp.clip(j - 1, 0, 1), i, 0)
    extra_out, extra_out_specs, extra_scratch = [], [], []
    if prompt:
        extra_out = [jax.ShapeDtypeStruct((tn // 2, m), BF16)] + [
            jax.ShapeDtypeStruct((3, dl, m // dl, tn), BF16) for dl in dils]
        extra_out_specs = [pl.BlockSpec((tn // 2, tm), lambda i, j: (0, i))] + [
            pl.BlockSpec((None, dl, tm // dl, tn), lambda i, j: (j // 2, 0, i, 0)) for dl in dils]
        extra_scratch = [pltpu.VMEM((tn // LANES, tm, LANES), F32),
                         pltpu.VMEM((tn // LANES, DILATIONS[1], tm // DILATIONS[1], LANES), F32)]
    return pl.pallas_call(
        functools.partial(_in_proj_kernel, prompt),
        out_shape=tuple([jax.ShapeDtypeStruct((nblk, m, tn), BF16),
                         jax.ShapeDtypeStruct((2, m, tn), F32),
                         jax.ShapeDtypeStruct((m, LANES), F32),
                         jax.ShapeDtypeStruct((n_gate, m), F32)] + extra_out),
        grid=(m // tm, nblk),
        in_specs=[pl.BlockSpec((tm, d), lambda i, j: (i, 0)),
                  pl.BlockSpec((1, d), const),
                  pl.BlockSpec((d, tn), lambda i, j: (0, col(j))),
                  pl.BlockSpec((d, LANES), const),
                  pl.BlockSpec((n_gate, d), const)],
        out_specs=tuple([pl.BlockSpec((None, tm, tn), lambda i, j: (col(j), i, 0)),
                         pl.BlockSpec((None, tm, tn), kv_blk),
                         pl.BlockSpec((tm, LANES), lambda i, j: (i, 0)),
                         pl.BlockSpec((n_gate, tm), lambda i, j: (0, i))] + extra_out_specs),
        scratch_shapes=[pltpu.VMEM((tm, d), BF16)] + extra_scratch,
        compiler_params=_params(("parallel", "arbitrary")),
        name="in_proj_dilated" if prompt else "in_proj",
    )(x, g, w_all, wg_pad, wg.T)


def _dil_attn_kernel(heads, d, nq, rb_ref, bkt_ref, q_ref, kc_ref, kp_ref, vc_ref, vp_ref,
                     o_ref, lse_ref, bias, s_scr, p_scr):
    n = pl.program_id(0)
    r = pl.program_id(1)

    @pl.when((n == 0) & (r == 0))
    def _():
        bkt = bkt_ref[...]
        for h in range(heads):
            acc = jnp.full(bkt.shape, NEG, F32)
            for k in range(REL_BUCKETS):
                acc = jnp.where(bkt == k, rb_ref[k * heads + h], acc)
            bias[h] = acc

    blk = ATT_BLOCK
    in_prev = lax.broadcasted_iota(jnp.int32, (blk, 2 * blk), 1) < blk
    for u in range(nq):
        for h in range(heads):
            sl = slice(h * HEAD_DIM, (h + 1) * HEAD_DIM)
            qh = q_ref[u * blk:(u + 1) * blk, sl]
            if u == 0:
                keys = jnp.concatenate([kp_ref[:, sl], kc_ref[:blk, sl]], axis=0)
            else:
                keys = kc_ref[(u - 1) * blk:(u + 1) * blk, sl]
            s = _dot_nt(qh, keys) * ATT_SCALE + bias[h]
            if u == 0:
                s = jnp.where((n == 0) & in_prev, NEG, s)
            s_scr[u * heads + h] = s
    lane = lax.broadcasted_iota(jnp.int32, (blk, LANES), 1)
    m_tiles = []
    for u in range(nq):
        m_tile = jnp.zeros((blk, LANES), F32)
        for h in range(heads):
            s = s_scr[u * heads + h]
            m = jnp.max(s, axis=-1, keepdims=True)
            p_scr[u * heads + h] = jnp.exp(s - m).astype(BF16)
            m_tile = jnp.where(lane == h, m, m_tile)
        m_tiles.append(m_tile)
    ones = jnp.ones((2 * blk, HEAD_DIM), BF16)
    for u in range(nq):
        rows = pl.ds(r + u * blk * d, blk, stride=d)
        lse_tile = m_tiles[u]
        for h in range(heads):
            sl = slice(h * HEAD_DIM, (h + 1) * HEAD_DIM)
            if u == 0:
                vals = jnp.concatenate([vp_ref[:, sl], vc_ref[:blk, sl]], axis=0)
            else:
                vals = vc_ref[(u - 1) * blk:(u + 1) * blk, sl]
            o = _dot(p_scr[u * heads + h], jnp.concatenate([vals, ones], axis=1))
            l = o[:, HEAD_DIM:]
            o_ref[h, rows, :] = o[:, :HEAD_DIM] / l
            lse_tile = jnp.where(lane == h, lse_tile + jnp.log(l), lse_tile)
        lse_ref[rows, :] = lse_tile


DIL_ATTN_OUT_BLOCK_BYTES = 8 * 2 ** 20


def _dil_attn(zd, rel_flat, dilation, heads):
    d = dilation
    sub_len, w = zd.shape[2], zd.shape[3]
    s = sub_len * d
    blk = ATT_BLOCK
    nq = 2 if 2 * blk * d * w * 4 <= DIL_ATTN_OUT_BLOCK_BYTES else 1
    nb = sub_len // (nq * blk)
    qi = np.arange(blk)[:, None]
    ki = np.arange(2 * blk)[None, :]
    dist = blk + qi - ki
    bkt = np.where((dist >= 0) & (dist <= SUB_WINDOW), _rel_bucket(np.maximum(dist, 0) * d), -1).astype(np.int32)

    def cur(c):
        return pl.BlockSpec((None, None, nq * blk, w), lambda n, r: (c, r, n, 0))

    def prev(c):
        return pl.BlockSpec((None, None, blk, w), lambda n, r: (c, r, jnp.maximum(nq * n - 1, 0), 0))

    return pl.pallas_call(
        functools.partial(_dil_attn_kernel, heads, d, nq),
        out_shape=(jax.ShapeDtypeStruct((heads, s, HEAD_DIM), F32),
                   jax.ShapeDtypeStruct((s, LANES), F32)),
        grid=(nb, d),
        in_specs=[pl.BlockSpec(memory_space=pltpu.SMEM),
                  pl.BlockSpec((blk, 2 * blk), lambda n, r: (0, 0)),
                  cur(0), cur(1), prev(1), cur(2), prev(2)],
        out_specs=(pl.BlockSpec((heads, nq * blk * d, HEAD_DIM), lambda n, r: (0, n, 0)),
                   pl.BlockSpec((nq * blk * d, LANES), lambda n, r: (n, 0))),
        scratch_shapes=[pltpu.VMEM((heads, blk, 2 * blk), F32),
                        pltpu.VMEM((nq * heads, blk, 2 * blk), F32),
                        pltpu.VMEM((nq * heads, blk, 2 * blk), BF16)],
        compiler_params=_params(("arbitrary", "arbitrary")),
        name=f"dil_attn_d{d}",
    )(rel_flat, jnp.asarray(bkt), zd, zd, zd, zd, zd)


def _split_bf16(x):
    hi = x.astype(BF16)
    return hi, (x - hi.astype(F32)).astype(BF16)


def _mlstm_kernel(heads, qk_ref, kt_ref, v_ref, mo_ref, gt_ref, bgc_ref, gm_ref,
                  ce0_ref, m0_ref, out_ref, ce_out, m_out, ce_s, m_s, sqk_s, tile_s):
    step = pl.program_id(0)
    t = MLSTM_CHUNK
    qw = heads * MLSTM_DK
    dk = MLSTM_DK

    @pl.when(step == 0)
    def _():
        ce_s[...] = ce0_ref[...]
        m_s[...] = m0_ref[...]

    row = lax.broadcasted_iota(jnp.int32, (t, t), 0)
    col = lax.broadcasted_iota(jnp.int32, (t, t), 1)
    causal = col <= row
    eye = col == row
    upper = jnp.where(row <= col, 1.0, 0.0).astype(BF16)
    lane = lax.broadcasted_iota(jnp.int32, (1, LANES), 1)
    lane_h = lax.broadcasted_iota(jnp.int32, (heads, t), 1)
    blk_r = lax.broadcasted_iota(jnp.int32, (2 * t, 2 * t), 0) // t
    blk_c = lax.broadcasted_iota(jnp.int32, (2 * t, 2 * t), 1) // t
    ones_bd = jnp.where(blk_r == blk_c, 1.0, 0.0).astype(BF16)
    ones = jnp.ones((t, HEAD_DIM), BF16)

    gt = gt_ref[...] + bgc_ref[...]
    hi, lo = _split_bf16(_log_sigmoid(gt[heads:]))
    b2 = _dot(jnp.concatenate([hi, lo], axis=0), upper)
    b = b2[:heads] + b2[heads:]
    c = gt[:heads] - b
    m_prev = m_s[...]
    cmax = c
    for shift in (1, 2, 4, 8, 16, 32, 64):
        cmax = jnp.maximum(cmax, jnp.where(lane_h >= shift, pltpu.roll(cmax, shift, axis=1), NEG))
    a = jnp.maximum(cmax, m_prev)
    m_t = b + a
    a_last = a[:, t - 1:t]
    b_last = b[:, t - 1:t]
    m_new = b_last + a_last
    w_k = jnp.exp(c - a_last)
    decay = jnp.exp(b_last + m_prev - m_new)
    a_hi, a_lo = _split_bf16(a)
    mt_hi, mt_lo = _split_bf16(m_t)

    def diag(x_row):
        return jnp.where(eye, x_row.astype(F32), 0.0).astype(BF16)

    qms, dens = [], []
    for h in range(heads):
        pair, half = divmod(h, 2)
        hs = slice(h, h + 1)
        tile = (_dot(jnp.concatenate([diag(a_hi[hs]), diag(mt_hi[hs])], axis=1), ones_bd)
                + _dot(jnp.concatenate([diag(a_lo[hs]), diag(mt_lo[hs])], axis=1), ones_bd))
        tile_s[h] = tile
        w_intra = jnp.exp(jnp.where(causal, c[hs] - tile[:, :t], NEG))
        mine = (lane >= half * dk) & (lane < (half + 1) * dk)
        qm = jnp.where(mine, qk_ref[:, pair * LANES:(pair + 1) * LANES], 0.0) * QK_SCALE
        km = jnp.where(mine, qk_ref[:, qw + pair * LANES:qw + (pair + 1) * LANES], 0.0)
        sqk = _dot_nt(qm, km) * w_intra
        sqk_s[h] = sqk.astype(BF16)
        dens.append(jnp.sum(sqk, axis=-1, keepdims=True))
        qms.append(qm)

    for h in range(heads):
        pair = h // 2
        sl = slice(h * HEAD_DIM, (h + 1) * HEAD_DIM)
        tile = tile_s[h]
        w_prev = jnp.exp(m_prev[h:h + 1, :] - tile[:, :t])
        q_ce = _dot(qms[h], ce_s[pair * LANES:(pair + 1) * LANES, :].astype(BF16))
        num = _dot(sqk_s[h], v_ref[:, sl]) + w_prev * q_ce[:, :HEAD_DIM]
        den = dens[h] + w_prev * q_ce[:, HEAD_DIM:]
        hh = num / jnp.maximum(jnp.abs(den), jnp.exp(-tile[:, t:]))
        hn = hh * lax.rsqrt(jnp.mean(hh * hh, axis=-1, keepdims=True) + EPS)
        gate = jax.nn.sigmoid(mo_ref[:, sl].astype(F32))
        out_ref[:, sl] = (hn * gm_ref[:, sl] * gate).astype(out_ref.dtype)

    for h in range(heads):
        hs = slice(h * dk, (h + 1) * dk)
        kw = (kt_ref[hs, :].astype(F32) * w_k[h:h + 1, :]).astype(BF16)
        v_ext = jnp.concatenate([v_ref[:, h * HEAD_DIM:(h + 1) * HEAD_DIM], ones], axis=1)
        dec = jnp.concatenate([decay[h:h + 1, :], decay[h:h + 1, :]], axis=1)
        ce_s[hs, :] = dec * ce_s[hs, :] + _dot(kw, v_ext)
    m_s[...] = jnp.broadcast_to(m_new, (heads, LANES))

    @pl.when(step == pl.num_programs(0) - 1)
    def _():
        ce_out[...] = ce_s[...]
        m_out[...] = m_s[...]


def _mlstm_prompt(z6, k_t, gates_t, b_gate, g_mlstm, heads):
    _, s, w = z6.shape
    t = MLSTM_CHUNK
    qw = heads * MLSTM_DK
    ce0 = jnp.zeros((qw, 2 * HEAD_DIM), F32)
    m0 = jnp.zeros((heads, LANES), F32)
    const = lambda c: (0, 0)
    return pl.pallas_call(
        functools.partial(_mlstm_kernel, heads),
        out_shape=(jax.ShapeDtypeStruct((s, w), BF16),
                   jax.ShapeDtypeStruct((qw, 2 * HEAD_DIM), F32),
                   jax.ShapeDtypeStruct((heads, LANES), F32)),
        grid=(s // t,),
        in_specs=[pl.BlockSpec((None, t, w), lambda c: (3, c, 0)),
                  pl.BlockSpec((qw, t), lambda c: (0, c)),
                  pl.BlockSpec((None, t, w), lambda c: (4, c, 0)),
                  pl.BlockSpec((None, t, w), lambda c: (5, c, 0)),
                  pl.BlockSpec((2 * heads, t), lambda c: (0, c)),
                  pl.BlockSpec((2 * heads, 1), const),
                  pl.BlockSpec((1, w), const),
                  pl.BlockSpec((qw, 2 * HEAD_DIM), const),
                  pl.BlockSpec((heads, LANES), const)],
        out_specs=(pl.BlockSpec((t, w), lambda c: (c, 0)),
                   pl.BlockSpec((qw, 2 * HEAD_DIM), const),
                   pl.BlockSpec((heads, LANES), const)),
        scratch_shapes=[pltpu.VMEM((qw, 2 * HEAD_DIM), F32),
                        pltpu.VMEM((heads, LANES), F32),
                        pltpu.VMEM((heads, t, t), BF16),
                        pltpu.VMEM((heads, t, 2 * t), F32)],
        compiler_params=_params(("arbitrary",)),
        name="mlstm_prompt",
    )(z6, k_t, z6, z6, gates_t, b_gate.reshape(2 * heads, 1), g_mlstm.reshape(1, w), ce0, m0)


def _out_proj_kernel(merge, heads, *refs):
    if merge:
        (o1, o2, o3, l1, l2, l3, mo_ref, x_ref, wa_ref, wm_ref, gpost_ref, gpre_ref, wq_ref,
         x1_ref, qx_ref, att_s) = refs
        la, lb, lc = l1[...], l2[...], l3[...]
        mx = jnp.maximum(jnp.maximum(la, lb), lc)
        ea, eb, ec = jnp.exp(la - mx), jnp.exp(lb - mx), jnp.exp(lc - mx)
        tot = ea + eb + ec
        wa_, wb_, wc_ = ea / tot, eb / tot, ec / tot
        for h in range(heads):
            sl = slice(h * HEAD_DIM, (h + 1) * HEAD_DIM)
            mix = wa_[:, h:h + 1] * o1[h] + wb_[:, h:h + 1] * o2[h] + wc_[:, h:h + 1] * o3[h]
            att_s[:, sl] = mix.astype(BF16)
        att = att_s[...]
    else:
        att_ref, mo_ref, x_ref, wa_ref, wm_ref, gpost_ref, gpre_ref, wq_ref, x1_ref, qx_ref = refs
        att = att_ref[...]
    y = _dot(att, wa_ref[...]) + _dot(mo_ref[...], wm_ref[...])
    x1 = x_ref[...] + _rms(y, gpost_ref[...])
    x1_ref[...] = x1
    u = _rms(x1, gpre_ref[...]).astype(BF16)
    qx_ref[...] = _dot(u, wq_ref[...]).astype(qx_ref.dtype)


def _out_proj(att_parts, m_out, x, w_out, g_post, g_pre, w_xq, tm, heads):
    m, d = x.shape
    wm = m_out.shape[1]
    wa = w_out.shape[0] - wm
    assert wa == wm
    nq = w_xq.shape[1]
    merge = len(att_parts) == 6
    row = lambda i: (i, 0)
    const = lambda i: (0, 0)
    if merge:
        part_specs = ([pl.BlockSpec((heads, tm, HEAD_DIM), lambda i: (0, i, 0))] * 3
                      + [pl.BlockSpec((tm, LANES), row)] * 3)
    else:
        part_specs = [pl.BlockSpec((tm, wa), row)]
    return pl.pallas_call(
        functools.partial(_out_proj_kernel, merge, heads),
        out_shape=(jax.ShapeDtypeStruct((m, d), F32), jax.ShapeDtypeStruct((m, nq), BF16)),
        grid=(m // tm,),
        in_specs=part_specs + [pl.BlockSpec((tm, wm), row),
                               pl.BlockSpec((tm, d), row),
                               pl.BlockSpec((wa, d), const),
                               pl.BlockSpec((wm, d), lambda i: (1, 0)),
                               pl.BlockSpec((1, d), const),
                               pl.BlockSpec((1, d), const),
                               pl.BlockSpec((d, nq), const)],
        out_specs=(pl.BlockSpec((tm, d), row), pl.BlockSpec((tm, nq), row)),
        scratch_shapes=[pltpu.VMEM((tm, wa), BF16)] if merge else [],
        compiler_params=_params(("parallel",)),
        name="out_proj_merge" if merge else "out_proj",
    )(*att_parts, m_out, x, w_out, w_out, g_post, g_pre, w_xq)


def _xattn_kernel(heads, q_ref, mk_ref, mv_ref, o_ref):
    for h in range(heads):
        sl = slice(h * HEAD_DIM, (h + 1) * HEAD_DIM)
        s = _dot_nt(q_ref[:, sl], mk_ref[:, sl].astype(BF16)) * ATT_SCALE
        m = jnp.max(s, axis=-1, keepdims=True)
        p = jnp.exp(s - m)
        l = jnp.sum(p, axis=-1, keepdims=True)
        o = _dot(p.astype(BF16), mv_ref[:, sl].astype(BF16))
        o_ref[:, sl] = (o / l).astype(o_ref.dtype)


def _xattn(qx, mem_k, mem_v, tm, heads):
    m, w = qx.shape
    n_mem = mem_k.shape[0]
    return pl.pallas_call(
        functools.partial(_xattn_kernel, heads),
        out_shape=jax.ShapeDtypeStruct((m, w), BF16),
        grid=(m // tm,),
        in_specs=[pl.BlockSpec((tm, w), lambda i: (i, 0)),
                  pl.BlockSpec((n_mem, w), lambda i: (0, 0)),
                  pl.BlockSpec((n_mem, w), lambda i: (0, 0))],
        out_specs=pl.BlockSpec((tm, w), lambda i: (i, 0)),
        compiler_params=_params(("parallel",)),
        name="xattn_prompt",
    )(qx, mem_k, mem_v)


def _proj_post_kernel(a_ref, x_ref, w_ref, g_ref, y_ref):
    y_ref[...] = x_ref[...] + _rms(_dot(a_ref[...], w_ref[...]), g_ref[...])


def _proj_post(a, x, w, g, tm):
    m, d = x.shape
    k = a.shape[1]
    return pl.pallas_call(
        _proj_post_kernel,
        out_shape=jax.ShapeDtypeStruct((m, d), F32),
        grid=(m // tm,),
        in_specs=[pl.BlockSpec((tm, k), lambda i: (i, 0)),
                  pl.BlockSpec((tm, d), lambda i: (i, 0)),
                  pl.BlockSpec((k, d), lambda i: (0, 0)),
                  pl.BlockSpec((1, d), lambda i: (0, 0))],
        out_specs=pl.BlockSpec((tm, d), lambda i: (i, 0)),
        compiler_params=_params(("parallel",)),
        name="proj_post",
    )(a, x, w, g)


def _mlp_kernel(x_ref, gpre_ref, wu_ref, wd_ref, gpost_ref, y_ref, xn_s, acc_s):
    j = pl.program_id(1)

    @pl.when(j == 0)
    def _():
        xn_s[...] = _rms(x_ref[...], gpre_ref[...]).astype(BF16)
        acc_s[...] = jnp.zeros(acc_s.shape, F32)

    hidden = jnp.square(jnp.maximum(_dot(xn_s[...], wu_ref[...]), 0.0))
    acc_s[...] += _dot(hidden.astype(BF16), wd_ref[...])

    @pl.when(j == pl.num_programs(1) - 1)
    def _():
        y_ref[...] = x_ref[...] + _rms(acc_s[...], gpost_ref[...])


def _mlp(x, g_pre, w_up, w_down, g_post, tm, tf):
    m, d = x.shape
    ff = w_up.shape[1]
    return pl.pallas_call(
        _mlp_kernel,
        out_shape=jax.ShapeDtypeStruct((m, d), F32),
        grid=(m // tm, ff // tf),
        in_specs=[pl.BlockSpec((tm, d), lambda i, j: (i, 0)),
                  pl.BlockSpec((1, d), lambda i, j: (0, 0)),
                  pl.BlockSpec((d, tf), lambda i, j: (0, j)),
                  pl.BlockSpec((tf, d), lambda i, j: (j, 0)),
                  pl.BlockSpec((1, d), lambda i, j: (0, 0))],
        out_specs=pl.BlockSpec((tm, d), lambda i, j: (i, 0)),
        scratch_shapes=[pltpu.VMEM((tm, d), BF16), pltpu.VMEM((tm, d), F32)],
        compiler_params=_params(("parallel", "arbitrary")),
        name="mlp",
    )(x, g_pre, w_up, w_down, g_post)


def _single_query_attention(q, groups, new):
    scores = []
    for k3, _, b3 in groups:
        s = jnp.sum(k3 * q[None], axis=-1, keepdims=True) * ATT_SCALE
        scores.append(s if b3 is None else s + b3)
    mx = jnp.max(scores[0], axis=0)
    for s in scores[1:]:
        mx = jnp.maximum(mx, jnp.max(s, axis=0))
    if new is not None:
        k_new, v_new, b_new, count = new
        s_new = jnp.sum(k_new * q, axis=-1, keepdims=True) * ATT_SCALE + b_new
        mx = jnp.maximum(mx, s_new)
    den = jnp.zeros(mx.shape, F32)
    acc = jnp.zeros(q.shape, F32)
    for s, (_, v3, _) in zip(scores, groups):
        p = jnp.exp(s - mx[None])
        den = den + jnp.sum(p, axis=0)
        acc = acc + jnp.sum(p * v3, axis=0)
    if new is not None:
        p_new = jnp.exp(s_new - mx) * count
        den = den + p_new
        acc = acc + p_new * v_new
    return acc / den


def _sample_win_attn_kernel(tb, rel_ref, bkt_ref, q_ref, kn_ref, vn_ref,
                            k1, k4, k16, v1, v4, v16, o_ref, bias_s):
    @pl.when(pl.program_id(0) == 0)
    def _():
        for p in range(len(DILATIONS)):
            bkt = bkt_ref[p]
            acc = jnp.zeros(bkt.shape, F32)
            for k in range(REL_BUCKETS):
                acc = jnp.where(bkt == k, rel_ref[k][None], acc)
            bias_s[p] = acc

    for b in range(tb):
        groups = [(k[b], v[b], bias_s[p]) for p, (k, v) in enumerate(((k1, v1), (k4, v4), (k16, v16)))]
        new = (kn_ref[b], vn_ref[b], rel_ref[0], float(len(DILATIONS)))
        o_ref[b] = _single_query_attention(q_ref[b], groups, new)


def _sample_win_attn(q, k_new, v_new, cache_k, cache_v, rel_bias, tb):
    nb, heads, hd = q.shape
    buf = cache_k.shape[1]
    r = SUB_WINDOW
    assert buf == r * DILATIONS[-1], "window buffer must hold exactly the widest dilated pattern"
    j = r - np.arange(r)
    bkt = np.stack([_rel_bucket(j * d) for d in DILATIONS]).astype(np.int32)
    bkt = np.broadcast_to(bkt[:, :, None, None], (len(DILATIONS), r, heads, 1))
    views, specs = [], []
    for cache in (cache_k, cache_v):
        for d in DILATIONS:
            views.append(cache.reshape(nb, buf // d, d, heads, hd))
            specs.append(pl.BlockSpec((tb, r, None, heads, hd), functools.partial(
                lambda i, blk: (i, blk, 0, 0, 0), blk=buf // d // r - 1)))
    tok = pl.BlockSpec((tb, heads, hd), lambda i: (i, 0, 0))
    return pl.pallas_call(
        functools.partial(_sample_win_attn_kernel, tb),
        out_shape=jax.ShapeDtypeStruct((nb, heads, hd), F32),
        grid=(nb // tb,),
        in_specs=[pl.BlockSpec((REL_BUCKETS, heads, 1), lambda i: (0, 0, 0)),
                  pl.BlockSpec((len(DILATIONS), r, heads, 1), lambda i: (0, 0, 0, 0)),
                  tok, tok, tok] + specs,
        out_specs=tok,
        scratch_shapes=[pltpu.VMEM((len(DILATIONS), r, heads, 1), F32)],
        compiler_params=_params(("arbitrary",)),
        name="sample_win_attn",
    )(rel_bias.reshape(REL_BUCKETS, heads, 1), jnp.asarray(bkt), q, k_new, v_new, *views)


def _sample_xattn_kernel(tb, q_ref, k_ref, v_ref, o_ref):
    for b in range(tb):
        o_ref[b] = _single_query_attention(q_ref[b], [(k_ref[b], v_ref[b], None)], None)


def _sample_xattn(qx, mem_k, mem_v, tb):
    nb, heads, hd = qx.shape
    n_mem = mem_k.shape[1]
    tok = pl.BlockSpec((tb, heads, hd), lambda i: (i, 0, 0))
    mem = pl.BlockSpec((tb, n_mem, heads, hd), lambda i: (i, 0, 0, 0))
    return pl.pallas_call(
        functools.partial(_sample_xattn_kernel, tb),
        out_shape=jax.ShapeDtypeStruct((nb, heads, hd), F32),
        grid=(nb // tb,),
        in_specs=[tok, mem, mem],
        out_specs=tok,
        compiler_params=_params(("parallel",)),
        name="sample_xattn",
    )(qx, mem_k, mem_v)


def _sample_mlstm_kernel(heads, tb, q_ref, k_ref, v_ref, mo_ref, li_ref, lf_ref, bi_ref, bf_ref,
                         gm_ref, c_ref, n_ref, m_ref, out_ref, c_out, n_out, m_out):
    qw = heads * MLSTM_DK
    hrow = lax.broadcasted_iota(jnp.int32, (LANES, qw), 0)
    hcol = lax.broadcasted_iota(jnp.int32, (LANES, qw), 1) // MLSTM_DK
    diag = hrow == hcol
    rows8 = slice(0, heads)
    for b in range(tb):
        q_bd = jnp.where(diag, q_ref[b:b + 1, :], 0.0) * QK_SCALE
        k_bd = jnp.where(diag, k_ref[b:b + 1, :], 0.0)
        c_b = c_ref[b]
        n_row = n_ref[b:b + 1, :]
        li = li_ref[b] + bi_ref[...]
        lf = _log_sigmoid(lf_ref[b] + bf_ref[...])
        m_prev = m_ref[b]
        qk = jnp.sum(q_bd * k_ref[b:b + 1, :], axis=-1, keepdims=True)[rows8]
        qn = jnp.sum(q_bd * n_row, axis=-1, keepdims=True)[rows8]
        q_c = _dot(q_bd.astype(BF16), c_b.astype(BF16))[rows8]
        inter = lf + m_prev
        m_t = jnp.maximum(inter, li)
        w_intra = jnp.exp(li - m_t)
        w_prev = jnp.exp(inter - m_t)
        sqk = qk * w_intra
        v_b = v_ref[b]
        num = sqk * v_b + w_prev * q_c
        den = sqk + w_prev * qn
        hh = num / jnp.maximum(jnp.abs(den), jnp.exp(-m_t))
        hn = hh * lax.rsqrt(jnp.mean(hh * hh, axis=-1, keepdims=True) + EPS)
        out_ref[b] = hn * gm_ref[...] * jax.nn.sigmoid(mo_ref[b])
        wv = jnp.concatenate([w_intra * v_b, jnp.zeros((LANES - heads, HEAD_DIM), F32)], axis=0)
        d_c = _dot_tn(k_bd.astype(BF16), wv.astype(BF16))
        for h in range(heads):
            hs = slice(h * MLSTM_DK, (h + 1) * MLSTM_DK)
            c_out[b, hs, :] = w_prev[h:h + 1, :] * c_b[hs, :] + d_c[hs, :]
        k_w = jnp.sum(k_bd[rows8] * w_intra, axis=0, keepdims=True)
        dec_row = jnp.sum(jnp.where(diag[rows8], w_prev, 0.0), axis=0, keepdims=True)
        n_out[b:b + 1, :] = dec_row * n_row + k_w
        m_out[b] = m_t


def _sample_mlstm(z6, gates, b_gate, g_mlstm, state_c, state_n, state_m, heads, tb):
    _, nb, w = z6.shape
    qw = heads * MLSTM_DK
    q = z6[3, :, :qw].astype(F32)
    k = z6[3, :, qw:].astype(F32)
    v = z6[4].astype(F32).reshape(nb, heads, HEAD_DIM)
    mo = z6[5].astype(F32).reshape(nb, heads, HEAD_DIM)
    li = gates[:, :heads].reshape(nb, heads, 1)
    lf = gates[:, heads:2 * heads].reshape(nb, heads, 1)
    row = lambda i: (i, 0)
    blk3 = lambda i: (i, 0, 0)
    const = lambda i: (0, 0)
    return pl.pallas_call(
        functools.partial(_sample_mlstm_kernel, heads, tb),
        out_shape=(jax.ShapeDtypeStruct((nb, heads, HEAD_DIM), F32),
                   jax.ShapeDtypeStruct((nb, qw, HEAD_DIM), F32),
                   jax.ShapeDtypeStruct((nb, qw), F32),
                   jax.ShapeDtypeStruct((nb, heads, 1), F32)),
        grid=(nb // tb,),
        in_specs=[pl.BlockSpec((tb, qw), row),
                  pl.BlockSpec((tb, qw), row),
                  pl.BlockSpec((tb, heads, HEAD_DIM), blk3),
                  pl.BlockSpec((tb, heads, HEAD_DIM), blk3),
                  pl.BlockSpec((tb, heads, 1), blk3),
                  pl.BlockSpec((tb, heads, 1), blk3),
                  pl.BlockSpec((heads, 1), const),
                  pl.BlockSpec((heads, 1), const),
                  pl.BlockSpec((heads, HEAD_DIM), const),
                  pl.BlockSpec((tb, qw, HEAD_DIM), blk3),
                  pl.BlockSpec((tb, qw), row),
                  pl.BlockSpec((tb, heads, 1), blk3)],
        out_specs=(pl.BlockSpec((tb, heads, HEAD_DIM), blk3),
                   pl.BlockSpec((tb, qw, HEAD_DIM), blk3),
                   pl.BlockSpec((tb, qw), row),
                   pl.BlockSpec((tb, heads, 1), blk3)),
        compiler_params=_params(("parallel",)),
        name="sample_mlstm",
    )(q, k, v, mo, li, lf, b_gate[:heads].reshape(heads, 1), b_gate[heads:].reshape(heads, 1),
      g_mlstm.reshape(heads, HEAD_DIM), state_c.reshape(nb, qw, HEAD_DIM), state_n.reshape(nb, qw),
      state_m.reshape(nb, heads, 1))


def _tile(m, cap):
    return min(m, cap)


def kernel(x_prompt, x_sample, mem_prompt, cache_win_k, cache_win_v, state_mlstm_C, state_mlstm_n,
           state_mlstm_m, cache_mem_k, cache_mem_v, rel_bias, g_pre_mix, g_post_mix, w_in, b_gate,
           g_mlstm, w_out, g_pre_xatt, g_post_xatt, g_mem, w_xq, w_xkv, w_xo, g_pre_mlp, g_post_mlp,
           w_up, w_down):
    depth = w_in.shape[0]
    assert depth == 1 and x_prompt.shape[0] == 1 and x_sample.shape[1] == 1
    _, seq, d_model = x_prompt.shape
    n_dec = x_sample.shape[0]
    att_heads = cache_win_k.shape[3]
    ml_heads = state_mlstm_C.shape[2]
    x_heads = cache_mem_k.shape[3]
    att_w = att_heads * HEAD_DIM
    ml_w = ml_heads * HEAD_DIM
    x_w = x_heads * HEAD_DIM
    n_mem = mem_prompt.shape[1]
    keep = cache_win_k.shape[2]
    n_main = 3 * att_w + 2 * ml_heads * MLSTM_DK + 2 * ml_w
    assert att_w == ml_w == 1024 and n_main % 1024 == 0 and seq >= keep

    def row(g):
        return g[0].reshape(1, -1)

    w_in_b = w_in[0].astype(BF16)
    w_out_b = w_out[0].astype(BF16)
    wq = w_xq[0].astype(BF16)
    wkv = w_xkv[0].astype(BF16)
    wo = w_xo[0].astype(BF16)
    wu = w_up[0].astype(BF16)
    wd = w_down[0].astype(BF16)
    rel_flat = rel_bias.reshape(-1)

    def tail(x1, qx_att, mem_attend, tm):
        x2 = _proj_post(mem_attend(qx_att), x1, wo, row(g_post_xatt), tm)
        return _mlp(x2, row(g_pre_mlp), wu, wd, row(g_post_mlp), tm, 1024)

    xp = x_prompt[0]
    mem_kv = _norm_matmul(mem_prompt[0], row(g_mem), wkv)
    mk_p, mv_p = mem_kv[:, :x_w], mem_kv[:, x_w:]
    z6, kv32, _, gates_t, k_t, *zd = _in_proj(xp, row(g_pre_mix), w_in_b, n_main, _tile(seq, 512), True)
    zd = [z6.reshape(z6.shape[0], 1, seq, att_w)] + zd
    parts = [_dil_attn(z, rel_flat, d, att_heads) for z, d in zip(zd, DILATIONS)]
    m_out, ce_p, m_p = _mlstm_prompt(z6, k_t, gates_t, b_gate[0], g_mlstm[0], ml_heads)
    c_p, n_p = ce_p[:, :HEAD_DIM], ce_p[:, HEAD_DIM]
    x1, qx = _out_proj(tuple(p[0] for p in parts) + tuple(p[1] for p in parts), m_out, xp, w_out_b,
                       row(g_post_mix), row(g_pre_xatt), wq, _tile(seq, 256), att_heads)
    tm = _tile(seq, 512)
    y_prompt = tail(x1, qx, lambda q: _xattn(q, mk_p, mv_p, tm, x_heads), tm)

    xs = x_sample[:, 0]
    z6s, kv32s, gates_s, _ = _in_proj(xs, row(g_pre_mix), w_in_b, n_main, n_dec, False)
    per_head = (n_dec, att_heads, HEAD_DIM)
    att_s = _sample_win_attn(z6s[0].astype(F32).reshape(per_head), kv32s[0].reshape(per_head),
                             kv32s[1].reshape(per_head), cache_win_k[0], cache_win_v[0], rel_bias, 4)
    mo_s, c_s, n_s, m_s = _sample_mlstm(z6s, gates_s, b_gate[0], g_mlstm[0], state_mlstm_C[0],
                                        state_mlstm_n[0], state_mlstm_m[0], ml_heads, 8)
    x1s, qxs = _out_proj((att_s.reshape(n_dec, att_w).astype(BF16),), mo_s.reshape(n_dec, ml_w).astype(BF16),
                         xs, w_out_b, row(g_post_mix), row(g_pre_xatt), wq, n_dec, att_heads)

    def sample_mem_attend(q):
        q3 = q.astype(F32).reshape(n_dec, x_heads, HEAD_DIM)
        return _sample_xattn(q3, cache_mem_k[0], cache_mem_v[0], 4).reshape(n_dec, x_w).astype(BF16)

    y_sample = tail(x1s, qxs, sample_mem_attend, n_dec)

    dk = MLSTM_DK
    return (y_prompt[None],
            y_sample[:, None],
            kv32[0, seq - keep:].reshape(1, 1, keep, att_heads, HEAD_DIM),
            kv32[1, seq - keep:].reshape(1, 1, keep, att_heads, HEAD_DIM),
            c_p.reshape(1, 1, ml_heads, dk, HEAD_DIM),
            n_p.reshape(1, 1, ml_heads, dk),
            m_p[:, 0].reshape(1, 1, ml_heads),
            mk_p.reshape(1, 1, n_mem, x_heads, HEAD_DIM),
            mv_p.reshape(1, 1, n_mem, x_heads, HEAD_DIM),
            kv32s[0].reshape(1, n_dec, 1, att_heads, HEAD_DIM),
            kv32s[1].reshape(1, n_dec, 1, att_heads, HEAD_DIM),
            c_s.reshape(1, n_dec, ml_heads, dk, HEAD_DIM),
            n_s.reshape(1, n_dec, ml_heads, dk),
            m_s.reshape(1, n_dec, ml_heads))
```

```python
import functools
import math

import numpy as np
import jax
import jax.numpy as jnp
from jax import lax
from jax.experimental import pallas as pl
from jax.experimental.pallas import tpu as pltpu

F32 = jnp.float32
BF16 = jnp.bfloat16

EPS = 1e-6
HEAD_DIM = 128
MLSTM_DK = 64
LANES = 128
DILATIONS = (1, 4, 16)
SUB_WINDOW = 128
ATT_BLOCK = 128
MLSTM_CHUNK = 128
REL_BUCKETS = 32
REL_MAX_DIST = 2048
NEG = -1e30
ATT_SCALE = HEAD_DIM ** -0.5
QK_SCALE = MLSTM_DK ** -0.5
VMEM_LIMIT = 56 * 2 ** 20


def _params(semantics):
    return pltpu.CompilerParams(dimension_semantics=semantics, vmem_limit_bytes=VMEM_LIMIT)


def _rms(xf, g):
    ms = jnp.mean(xf * xf, axis=-1, keepdims=True)
    return xf * lax.rsqrt(ms + EPS) * g


def _dot(a, b):
    return jnp.dot(a, b, preferred_element_type=F32)


def _dot_nt(a, b):
    return lax.dot_general(a, b, (((1,), (1,)), ((), ())), preferred_element_type=F32)


def _dot_tn(a, b):
    return lax.dot_general(a, b, (((0,), (0,)), ((), ())), preferred_element_type=F32)


def _log_sigmoid(x):
    return jnp.minimum(x, 0.0) - jnp.log(1.0 + jnp.exp(-jnp.abs(x)))


def _rel_bucket(dist):
    dist = np.asarray(dist)
    exact = REL_BUCKETS // 2
    far = exact + (np.log(np.maximum(dist, exact) / exact) / math.log(REL_MAX_DIST / exact)
                   * (REL_BUCKETS - exact)).astype(np.int32)
    return np.where(dist < exact, dist, np.minimum(far, REL_BUCKETS - 1)).astype(np.int32)


def _norm_matmul_kernel(x_ref, g_ref, w_ref, o_ref):
    xn = _rms(x_ref[...], g_ref[...]).astype(BF16)
    o_ref[...] = _dot(xn, w_ref[...])


def _norm_matmul(x, g, w):
    m, d = x.shape
    n = w.shape[1]
    return pl.pallas_call(
        _norm_matmul_kernel,
        out_shape=jax.ShapeDtypeStruct((m, n), F32),
        grid=(1,),
        in_specs=[pl.BlockSpec((m, d), lambda i: (0, 0)),
                  pl.BlockSpec((1, d), lambda i: (0, 0)),
                  pl.BlockSpec((d, n), lambda i: (0, 0))],
        out_specs=pl.BlockSpec((m, n), lambda i: (0, 0)),
        compiler_params=_params(("arbitrary",)),
        name="norm_matmul",
    )(x, g, w)


def _column_order(j):
    return (j % 2) * 3 + j // 2


def _in_proj_kernel(prompt, x_ref, g_ref, w_ref, wg_ref, wgt_ref, *rest):
    if prompt:
        z_ref, kv_ref, gate_ref, gate_t_ref, kt_ref, zd_mid, zd_wide, xn_ref, slab, slab_mid = rest
    else:
        z_ref, kv_ref, gate_ref, gate_t_ref, xn_ref = rest
    j = pl.program_id(1)

    @pl.when(j == 0)
    def _():
        xn = _rms(x_ref[...], g_ref[...]).astype(BF16)
        xn_ref[...] = xn
        gate_ref[...] = _dot(xn, wg_ref[...])
        gate_t_ref[...] = _dot_nt(wgt_ref[...], xn)

    if not prompt:
        acc = _dot(xn_ref[...], w_ref[j])
        z_ref[...] = acc.astype(BF16)

        @pl.when((j == 1) | (j == 2))
        def _():
            kv_ref[...] = acc
        return

    tm, tn = z_ref.shape
    mid = DILATIONS[1]
    ratio = DILATIONS[2] // mid
    col = _column_order(j)

    @pl.when(j % 2 == 0)
    def _():
        acc = _dot(xn_ref[...], w_ref[col])
        z_ref[...] = acc.astype(BF16)
        for h in range(tn // LANES):
            slab[h] = acc[:, h * LANES:(h + 1) * LANES]

        @pl.when(j >= 2)
        def _():
            kv_ref[...] = acc

    @pl.when(j % 2 == 1)
    def _():
        acc = _dot(xn_ref[...], w_ref[col])
        z_ref[...] = acc.astype(BF16)
        for h in range(tn // LANES):
            hl = slice(h * LANES, (h + 1) * LANES)
            for r in range(mid):
                rows = slab[h, pl.ds(r, tm // mid, stride=mid), :]
                zd_mid[r, :, hl] = rows.astype(BF16)
                slab_mid[h, r] = rows
            for r in range(mid):
                for q in range(ratio):
                    rows = slab_mid[h, r, pl.ds(q, tm // (mid * ratio), stride=ratio), :]
                    zd_wide[r + mid * q, :, hl] = rows.astype(BF16)

        @pl.when(j == 1)
        def _():
            kt_ref[...] = acc[:, tn // 2:].T.astype(BF16)


def _in_proj(x, g, w_blocks, wg, tm, prompt, kv_rows):
    m, d = x.shape
    nblk, _, tn = w_blocks.shape
    n_gate = wg.shape[1]
    wg_pad = jnp.zeros((d, LANES), BF16).at[:, :n_gate].set(wg)
    dils = DILATIONS[1:] if prompt else ()
    assert DILATIONS[2] % DILATIONS[1] == 0 and nblk == 6 and (m - kv_rows) % tm == 0
    const = lambda i, j: (0, 0)
    skip = (m - kv_rows) // tm
    if prompt:
        col = _column_order
        kv_c = lambda j: jnp.where(j < 4, 0, 1)
    else:
        col = lambda j: j
        kv_c = lambda j: jnp.clip(j - 1, 0, 1)
    kv_blk = lambda i, j: (jnp.where(i >= skip, kv_c(j), 0), jnp.maximum(i - skip, 0), 0)
    extra_out, extra_out_specs, extra_scratch = [], [], []
    if prompt:
        extra_out = [jax.ShapeDtypeStruct((tn // 2, m), BF16)] + [
            jax.ShapeDtypeStruct((3, dl, m // dl, tn), BF16) for dl in dils]
        extra_out_specs = [pl.BlockSpec((tn // 2, tm), lambda i, j: (0, i))] + [
            pl.BlockSpec((None, dl, tm // dl, tn), lambda i, j: (j // 2, 0, i, 0)) for dl in dils]
        extra_scratch = [pltpu.VMEM((tn // LANES, tm, LANES), F32),
                         pltpu.VMEM((tn // LANES, DILATIONS[1], tm // DILATIONS[1], LANES), F32)]
    return pl.pallas_call(
        functools.partial(_in_proj_kernel, prompt),
        out_shape=tuple([jax.ShapeDtypeStruct((nblk, m, tn), BF16),
                         jax.ShapeDtypeStruct((2, kv_rows, tn), F32),
                         jax.ShapeDtypeStruct((m, LANES), F32),
                         jax.ShapeDtypeStruct((n_gate, m), F32)] + extra_out),
        grid=(m // tm, nblk),
        in_specs=[pl.BlockSpec((tm, d), lambda i, j: (i, 0)),
                  pl.BlockSpec((1, d), const),
                  pl.BlockSpec((nblk, d, tn), lambda i, j: (0, 0, 0), pipeline_mode=pl.Buffered(1)),
                  pl.BlockSpec((d, LANES), const),
                  pl.BlockSpec((n_gate, d), const)],
        out_specs=tuple([pl.BlockSpec((None, tm, tn), lambda i, j: (col(j), i, 0)),
                         pl.BlockSpec((None, tm, tn), kv_blk),
                         pl.BlockSpec((tm, LANES), lambda i, j: (i, 0)),
                         pl.BlockSpec((n_gate, tm), lambda i, j: (0, i))] + extra_out_specs),
        scratch_shapes=[pltpu.VMEM((tm, d), BF16)] + extra_scratch,
        compiler_params=_params(("parallel", "arbitrary")),
        name="in_proj_dilated" if prompt else "in_proj",
    )(x, g, w_blocks, wg_pad, wg.T)


def _dil_attn_kernel(heads, d, nq, rb_ref, bkt_ref, q_ref, kc_ref, kp_ref, vc_ref, vp_ref,
                     o_ref, lse_ref, bias, s_scr, p_scr):
    n = pl.program_id(0)
    r = pl.program_id(1)

    @pl.when((n == 0) & (r == 0))
    def _():
        bkt = bkt_ref[...]
        for h in range(heads):
            acc = jnp.full(bkt.shape, NEG, F32)
            for k in range(REL_BUCKETS):
                acc = jnp.where(bkt == k, rb_ref[k * heads + h], acc)
            bias[h] = acc

    blk = ATT_BLOCK
    in_prev = lax.broadcasted_iota(jnp.int32, (blk, 2 * blk), 1) < blk
    for u in range(nq):
        for h in range(heads):
            sl = slice(h * HEAD_DIM, (h + 1) * HEAD_DIM)
            qh = q_ref[u * blk:(u + 1) * blk, sl]
            if u == 0:
                keys = jnp.concatenate([kp_ref[:, sl], kc_ref[:blk, sl]], axis=0)
            else:
                keys = kc_ref[(u - 1) * blk:(u + 1) * blk, sl]
            s = _dot_nt(qh, keys) * ATT_SCALE + bias[h]
            if u == 0:
                s = jnp.where((n == 0) & in_prev, NEG, s)
            s_scr[u * heads + h] = s
    lane = lax.broadcasted_iota(jnp.int32, (blk, LANES), 1)
    m_tiles = []
    for u in range(nq):
        m_tile = jnp.zeros((blk, LANES), F32)
        for h in range(heads):
            s = s_scr[u * heads + h]
            m = jnp.max(s, axis=-1, keepdims=True)
            p_scr[u * heads + h] = jnp.exp(s - m).astype(BF16)
            m_tile = jnp.where(lane == h, m, m_tile)
        m_tiles.append(m_tile)
    ones = jnp.ones((2 * blk, HEAD_DIM), BF16)
    for u in range(nq):
        rows = pl.ds(r + u * blk * d, blk, stride=d)
        lse_tile = m_tiles[u]
        for h in range(heads):
            sl = slice(h * HEAD_DIM, (h + 1) * HEAD_DIM)
            if u == 0:
                vals = jnp.concatenate([vp_ref[:, sl], vc_ref[:blk, sl]], axis=0)
            else:
                vals = vc_ref[(u - 1) * blk:(u + 1) * blk, sl]
            o = _dot(p_scr[u * heads + h], jnp.concatenate([vals, ones], axis=1))
            l = o[:, HEAD_DIM:]
            o_ref[h, rows, :] = o[:, :HEAD_DIM] / l
            lse_tile = jnp.where(lane == h, lse_tile + jnp.log(l), lse_tile)
        lse_ref[rows, :] = lse_tile


DIL_ATTN_OUT_BLOCK_BYTES = 16 * 2 ** 20


def _dil_attn(zd, rel_flat, dilation, heads):
    d = dilation
    sub_len, w = zd.shape[2], zd.shape[3]
    s = sub_len * d
    blk = ATT_BLOCK
    nq = 2 if 2 * blk * d * w * 4 <= DIL_ATTN_OUT_BLOCK_BYTES else 1
    nb = sub_len // (nq * blk)
    qi = np.arange(blk)[:, None]
    ki = np.arange(2 * blk)[None, :]
    dist = blk + qi - ki
    bkt = np.where((dist >= 0) & (dist <= SUB_WINDOW), _rel_bucket(np.maximum(dist, 0) * d), -1).astype(np.int32)

    def cur(c):
        return pl.BlockSpec((None, None, nq * blk, w), lambda n, r: (c, r, n, 0))

    def prev(c):
        return pl.BlockSpec((None, None, blk, w), lambda n, r: (c, r, jnp.maximum(nq * n - 1, 0), 0))

    return pl.pallas_call(
        functools.partial(_dil_attn_kernel, heads, d, nq),
        out_shape=(jax.ShapeDtypeStruct((heads, s, HEAD_DIM), F32),
                   jax.ShapeDtypeStruct((s, LANES), F32)),
        grid=(nb, d),
        in_specs=[pl.BlockSpec(memory_space=pltpu.SMEM),
                  pl.BlockSpec((blk, 2 * blk), lambda n, r: (0, 0)),
                  cur(0), cur(1), prev(1), cur(2), prev(2)],
        out_specs=(pl.BlockSpec((heads, nq * blk * d, HEAD_DIM), lambda n, r: (0, n, 0)),
                   pl.BlockSpec((nq * blk * d, LANES), lambda n, r: (n, 0))),
        scratch_shapes=[pltpu.VMEM((heads, blk, 2 * blk), F32),
                        pltpu.VMEM((nq * heads, blk, 2 * blk), F32),
                        pltpu.VMEM((nq * heads, blk, 2 * blk), BF16)],
        compiler_params=_params(("arbitrary", "arbitrary")),
        name=f"dil_attn_d{d}",
    )(rel_flat, jnp.asarray(bkt), zd, zd, zd, zd, zd)


def _split_bf16(x):
    hi = x.astype(BF16)
    return hi, (x - hi.astype(F32)).astype(BF16)


def _mlstm_kernel(heads, qk_ref, kt_ref, v_ref, mo_ref, gt_ref, bgc_ref, gm_ref,
                  ce0_ref, m0_ref, out_ref, ce_out, m_out, ce_s, m_s, sqk_s, tile_s):
    step = pl.program_id(0)
    t = MLSTM_CHUNK
    qw = heads * MLSTM_DK
    dk = MLSTM_DK

    @pl.when(step == 0)
    def _():
        ce_s[...] = ce0_ref[...]
        m_s[...] = m0_ref[...]

    row = lax.broadcasted_iota(jnp.int32, (t, t), 0)
    col = lax.broadcasted_iota(jnp.int32, (t, t), 1)
    causal = col <= row
    eye = col == row
    upper = jnp.where(row <= col, 1.0, 0.0).astype(BF16)
    lane = lax.broadcasted_iota(jnp.int32, (1, LANES), 1)
    lane_h = lax.broadcasted_iota(jnp.int32, (heads, t), 1)
    blk_r = lax.broadcasted_iota(jnp.int32, (2 * t, 2 * t), 0) // t
    blk_c = lax.broadcasted_iota(jnp.int32, (2 * t, 2 * t), 1) // t
    ones_bd = jnp.where(blk_r == blk_c, 1.0, 0.0).astype(BF16)
    ones = jnp.ones((t, HEAD_DIM), BF16)

    gt = gt_ref[...] + bgc_ref[...]
    hi, lo = _split_bf16(_log_sigmoid(gt[heads:]))
    b2 = _dot(jnp.concatenate([hi, lo], axis=0), upper)
    b = b2[:heads] + b2[heads:]
    c = gt[:heads] - b
    m_prev = m_s[...]
    cmax = c
    for shift in (1, 2, 4, 8, 16, 32, 64):
        cmax = jnp.maximum(cmax, jnp.where(lane_h >= shift, pltpu.roll(cmax, shift, axis=1), NEG))
    a = jnp.maximum(cmax, m_prev)
    m_t = b + a
    a_last = a[:, t - 1:t]
    b_last = b[:, t - 1:t]
    m_new = b_last + a_last
    w_k = jnp.exp(c - a_last)
    decay = jnp.exp(b_last + m_prev - m_new)
    a_hi, a_lo = _split_bf16(a)
    mt_hi, mt_lo = _split_bf16(m_t)

    def diag(x_row):
        return jnp.where(eye, x_row.astype(F32), 0.0).astype(BF16)

    qms, dens = [], []
    for h in range(heads):
        pair, half = divmod(h, 2)
        hs = slice(h, h + 1)
        tile = (_dot(jnp.concatenate([diag(a_hi[hs]), diag(mt_hi[hs])], axis=1), ones_bd)
                + _dot(jnp.concatenate([diag(a_lo[hs]), diag(mt_lo[hs])], axis=1), ones_bd))
        tile_s[h] = tile
        w_intra = jnp.exp(jnp.where(causal, c[hs] - tile[:, :t], NEG))
        mine = (lane >= half * dk) & (lane < (half + 1) * dk)
        qm = jnp.where(mine, qk_ref[:, pair * LANES:(pair + 1) * LANES], 0.0) * QK_SCALE
        km = jnp.where(mine, qk_ref[:, qw + pair * LANES:qw + (pair + 1) * LANES], 0.0)
        sqk = _dot_nt(qm, km) * w_intra
        sqk_s[h] = sqk.astype(BF16)
        dens.append(jnp.sum(sqk, axis=-1, keepdims=True))
        qms.append(qm)

    for h in range(heads):
        pair = h // 2
        sl = slice(h * HEAD_DIM, (h + 1) * HEAD_DIM)
        tile = tile_s[h]
        w_prev = jnp.exp(m_prev[h:h + 1, :] - tile[:, :t])
        q_ce = _dot(qms[h], ce_s[pair * LANES:(pair + 1) * LANES, :].astype(BF16))
        num = _dot(sqk_s[h], v_ref[:, sl]) + w_prev * q_ce[:, :HEAD_DIM]
        den = dens[h] + w_prev * q_ce[:, HEAD_DIM:]
        hh = num / jnp.maximum(jnp.abs(den), jnp.exp(-tile[:, t:]))
        hn = hh * lax.rsqrt(jnp.mean(hh * hh, axis=-1, keepdims=True) + EPS)
        gate = jax.nn.sigmoid(mo_ref[:, sl].astype(F32))
        out_ref[:, sl] = (hn * gm_ref[:, sl] * gate).astype(out_ref.dtype)

    for h in range(heads):
        hs = slice(h * dk, (h + 1) * dk)
        kw = (kt_ref[hs, :].astype(F32) * w_k[h:h + 1, :]).astype(BF16)
        v_ext = jnp.concatenate([v_ref[:, h * HEAD_DIM:(h + 1) * HEAD_DIM], ones], axis=1)
        dec = jnp.concatenate([decay[h:h + 1, :], decay[h:h + 1, :]], axis=1)
        ce_s[hs, :] = dec * ce_s[hs, :] + _dot(kw, v_ext)
    m_s[...] = jnp.broadcast_to(m_new, (heads, LANES))

    @pl.when(step == pl.num_programs(0) - 1)
    def _():
        ce_out[...] = ce_s[...]
        m_out[...] = m_s[...]


def _mlstm_prompt(z6, k_t, gates_t, b_gate, g_mlstm, heads):
    _, s, w = z6.shape
    t = MLSTM_CHUNK
    qw = heads * MLSTM_DK
    ce0 = jnp.zeros((qw, 2 * HEAD_DIM), F32)
    m0 = jnp.zeros((heads, LANES), F32)
    const = lambda c: (0, 0)
    return pl.pallas_call(
        functools.partial(_mlstm_kernel, heads),
        out_shape=(jax.ShapeDtypeStruct((s, w), BF16),
                   jax.ShapeDtypeStruct((qw, 2 * HEAD_DIM), F32),
                   jax.ShapeDtypeStruct((heads, LANES), F32)),
        grid=(s // t,),
        in_specs=[pl.BlockSpec((None, t, w), lambda c: (3, c, 0)),
                  pl.BlockSpec((qw, t), lambda c: (0, c)),
                  pl.BlockSpec((None, t, w), lambda c: (4, c, 0)),
                  pl.BlockSpec((None, t, w), lambda c: (5, c, 0)),
                  pl.BlockSpec((2 * heads, t), lambda c: (0, c)),
                  pl.BlockSpec((2 * heads, 1), const),
                  pl.BlockSpec((1, w), const),
                  pl.BlockSpec((qw, 2 * HEAD_DIM), const),
                  pl.BlockSpec((heads, LANES), const)],
        out_specs=(pl.BlockSpec((t, w), lambda c: (c, 0)),
                   pl.BlockSpec((qw, 2 * HEAD_DIM), const),
                   pl.BlockSpec((heads, LANES), const)),
        scratch_shapes=[pltpu.VMEM((qw, 2 * HEAD_DIM), F32),
                        pltpu.VMEM((heads, LANES), F32),
                        pltpu.VMEM((heads, t, t), BF16),
                        pltpu.VMEM((heads, t, 2 * t), F32)],
        compiler_params=_params(("arbitrary",)),
        name="mlstm_prompt",
    )(z6, k_t, z6, z6, gates_t, b_gate.reshape(2 * heads, 1), g_mlstm.reshape(1, w), ce0, m0)


def _out_proj_kernel(merge, heads, *refs):
    if merge:
        (o1, o2, o3, l1, l2, l3, mo_ref, x_ref, wa_ref, wm_ref, gpost_ref, gpre_ref, wq_ref,
         x1_ref, qx_ref, att_s) = refs
        la, lb, lc = l1[...], l2[...], l3[...]
        mx = jnp.maximum(jnp.maximum(la, lb), lc)
        ea, eb, ec = jnp.exp(la - mx), jnp.exp(lb - mx), jnp.exp(lc - mx)
        tot = ea + eb + ec
        wa_, wb_, wc_ = ea / tot, eb / tot, ec / tot
        for h in range(heads):
            sl = slice(h * HEAD_DIM, (h + 1) * HEAD_DIM)
            mix = wa_[:, h:h + 1] * o1[h] + wb_[:, h:h + 1] * o2[h] + wc_[:, h:h + 1] * o3[h]
            att_s[:, sl] = mix.astype(BF16)
        att = att_s[...]
    else:
        att_ref, mo_ref, x_ref, wa_ref, wm_ref, gpost_ref, gpre_ref, wq_ref, x1_ref, qx_ref = refs
        att = att_ref[...]
    y = _dot(att, wa_ref[...]) + _dot(mo_ref[...], wm_ref[...])
    x1 = x_ref[...] + _rms(y, gpost_ref[...])
    x1_ref[...] = x1
    u = _rms(x1, gpre_ref[...]).astype(BF16)
    qx_ref[...] = _dot(u, wq_ref[...]).astype(qx_ref.dtype)


def _out_proj(att_parts, m_out, x, w_out, g_post, g_pre, w_xq, tm, heads):
    m, d = x.shape
    wm = m_out.shape[1]
    wa = w_out.shape[0] - wm
    assert wa == wm
    nq = w_xq.shape[1]
    merge = len(att_parts) == 6
    row = lambda i: (i, 0)
    const = lambda i: (0, 0)
    if merge:
        part_specs = ([pl.BlockSpec((heads, tm, HEAD_DIM), lambda i: (0, i, 0))] * 3
                      + [pl.BlockSpec((tm, LANES), row)] * 3)
    else:
        part_specs = [pl.BlockSpec((tm, wa), row)]
    return pl.pallas_call(
        functools.partial(_out_proj_kernel, merge, heads),
        out_shape=(jax.ShapeDtypeStruct((m, d), F32), jax.ShapeDtypeStruct((m, nq), BF16)),
        grid=(m // tm,),
        in_specs=part_specs + [pl.BlockSpec((tm, wm), row),
                               pl.BlockSpec((tm, d), row),
                               pl.BlockSpec((wa, d), const, pipeline_mode=pl.Buffered(1)),
                               pl.BlockSpec((wm, d), lambda i: (1, 0), pipeline_mode=pl.Buffered(1)),
                               pl.BlockSpec((1, d), const),
                               pl.BlockSpec((1, d), const),
                               pl.BlockSpec((d, nq), const, pipeline_mode=pl.Buffered(1))],
        out_specs=(pl.BlockSpec((tm, d), row), pl.BlockSpec((tm, nq), row)),
        scratch_shapes=[pltpu.VMEM((tm, wa), BF16)] if merge else [],
        compiler_params=_params(("parallel",)),
        name="out_proj_merge" if merge else "out_proj",
    )(*att_parts, m_out, x, w_out, w_out, g_post, g_pre, w_xq)


def _xattn_kernel(heads, q_ref, mk_ref, mv_ref, o_ref):
    for h in range(heads):
        sl = slice(h * HEAD_DIM, (h + 1) * HEAD_DIM)
        s = _dot_nt(q_ref[:, sl], mk_ref[:, sl].astype(BF16)) * ATT_SCALE
        m = jnp.max(s, axis=-1, keepdims=True)
        p = jnp.exp(s - m)
        l = jnp.sum(p, axis=-1, keepdims=True)
        o = _dot(p.astype(BF16), mv_ref[:, sl].astype(BF16))
        o_ref[:, sl] = (o / l).astype(o_ref.dtype)


def _xattn(qx, mem_k, mem_v, tm, heads):
    m, w = qx.shape
    n_mem = mem_k.shape[0]
    return pl.pallas_call(
        functools.partial(_xattn_kernel, heads),
        out_shape=jax.ShapeDtypeStruct((m, w), BF16),
        grid=(m // tm,),
        in_specs=[pl.BlockSpec((tm, w), lambda i: (i, 0)),
                  pl.BlockSpec((n_mem, w), lambda i: (0, 0)),
                  pl.BlockSpec((n_mem, w), lambda i: (0, 0))],
        out_specs=pl.BlockSpec((tm, w), lambda i: (i, 0)),
        compiler_params=_params(("parallel",)),
        name="xattn_prompt",
    )(qx, mem_k, mem_v)


def _proj_post_kernel(a_ref, x_ref, w_ref, g_ref, y_ref):
    y_ref[...] = x_ref[...] + _rms(_dot(a_ref[...], w_ref[...]), g_ref[...])


def _proj_post(a, x, w, g, tm):
    m, d = x.shape
    k = a.shape[1]
    return pl.pallas_call(
        _proj_post_kernel,
        out_shape=jax.ShapeDtypeStruct((m, d), F32),
        grid=(m // tm,),
        in_specs=[pl.BlockSpec((tm, k), lambda i: (i, 0)),
                  pl.BlockSpec((tm, d), lambda i: (i, 0)),
                  pl.BlockSpec((k, d), lambda i: (0, 0)),
                  pl.BlockSpec((1, d), lambda i: (0, 0))],
        out_specs=pl.BlockSpec((tm, d), lambda i: (i, 0)),
        compiler_params=_params(("parallel",)),
        name="proj_post",
    )(a, x, w, g)


def _mlp_kernel(x_ref, gpre_ref, wu_ref, wd_ref, gpost_ref, y_ref, xn_s, acc_s):
    j = pl.program_id(1)

    @pl.when(j == 0)
    def _():
        xn_s[...] = _rms(x_ref[...], gpre_ref[...]).astype(BF16)
        acc_s[...] = jnp.zeros(acc_s.shape, F32)

    hidden = jnp.square(jnp.maximum(_dot(xn_s[...], wu_ref[...]), 0.0))
    acc_s[...] += _dot(hidden.astype(BF16), wd_ref[...])

    @pl.when(j == pl.num_programs(1) - 1)
    def _():
        y_ref[...] = x_ref[...] + _rms(acc_s[...], gpost_ref[...])


def _mlp(x, g_pre, w_up, w_down, g_post, tm, tf):
    m, d = x.shape
    ff = w_up.shape[1]
    return pl.pallas_call(
        _mlp_kernel,
        out_shape=jax.ShapeDtypeStruct((m, d), F32),
        grid=(m // tm, ff // tf),
        in_specs=[pl.BlockSpec((tm, d), lambda i, j: (i, 0)),
                  pl.BlockSpec((1, d), lambda i, j: (0, 0)),
                  pl.BlockSpec((d, tf), lambda i, j: (0, j)),
                  pl.BlockSpec((tf, d), lambda i, j: (j, 0)),
                  pl.BlockSpec((1, d), lambda i, j: (0, 0))],
        out_specs=pl.BlockSpec((tm, d), lambda i, j: (i, 0)),
        scratch_shapes=[pltpu.VMEM((tm, d), BF16), pltpu.VMEM((tm, d), F32)],
        compiler_params=_params(("parallel", "arbitrary")),
        name="mlp",
    )(x, g_pre, w_up, w_down, g_post)


def _single_query_attention(q, groups, new):
    scores = []
    for k3, _, b3 in groups:
        s = jnp.sum(k3 * q[None], axis=-1, keepdims=True) * ATT_SCALE
        scores.append(s if b3 is None else s + b3)
    mx = jnp.max(scores[0], axis=0)
    for s in scores[1:]:
        mx = jnp.maximum(mx, jnp.max(s, axis=0))
    if new is not None:
        k_new, v_new, b_new, count = new
        s_new = jnp.sum(k_new * q, axis=-1, keepdims=True) * ATT_SCALE + b_new
        mx = jnp.maximum(mx, s_new)
    den = jnp.zeros(mx.shape, F32)
    acc = jnp.zeros(q.shape, F32)
    for s, (_, v3, _) in zip(scores, groups):
        p = jnp.exp(s - mx[None])
        den = den + jnp.sum(p, axis=0)
        acc = acc + jnp.sum(p * v3, axis=0)
    if new is not None:
        p_new = jnp.exp(s_new - mx) * count
        den = den + p_new
        acc = acc + p_new * v_new
    return acc / den


def _sample_win_attn_kernel(tb, rel_ref, bkt_ref, q_ref, kn_ref, vn_ref,
                            k1, k4, k16, v1, v4, v16, o_ref, bias_s):
    @pl.when(pl.program_id(0) == 0)
    def _():
        for p in range(len(DILATIONS)):
            bkt = bkt_ref[p]
            acc = jnp.zeros(bkt.shape, F32)
            for k in range(REL_BUCKETS):
                acc = jnp.where(bkt == k, rel_ref[k][None], acc)
            bias_s[p] = acc

    for b in range(tb):
        groups = [(k[b], v[b], bias_s[p]) for p, (k, v) in enumerate(((k1, v1), (k4, v4), (k16, v16)))]
        new = (kn_ref[b], vn_ref[b], rel_ref[0], float(len(DILATIONS)))
        o_ref[b] = _single_query_attention(q_ref[b], groups, new)


def _sample_win_attn(q, k_new, v_new, cache_k, cache_v, rel_bias, tb):
    nb, heads, hd = q.shape
    buf = cache_k.shape[1]
    r = SUB_WINDOW
    assert buf == r * DILATIONS[-1], "window buffer must hold exactly the widest dilated pattern"
    j = r - np.arange(r)
    bkt = np.stack([_rel_bucket(j * d) for d in DILATIONS]).astype(np.int32)
    bkt = np.broadcast_to(bkt[:, :, None, None], (len(DILATIONS), r, heads, 1))
    views, specs = [], []
    for cache in (cache_k, cache_v):
        for d in DILATIONS:
            views.append(cache.reshape(nb, buf // d, d, heads, hd))
            specs.append(pl.BlockSpec((tb, r, None, heads, hd), functools.partial(
                lambda i, blk: (i, blk, 0, 0, 0), blk=buf // d // r - 1)))
    tok = pl.BlockSpec((tb, heads, hd), lambda i: (i, 0, 0))
    return pl.pallas_call(
        functools.partial(_sample_win_attn_kernel, tb),
        out_shape=jax.ShapeDtypeStruct((nb, heads, hd), F32),
        grid=(nb // tb,),
        in_specs=[pl.BlockSpec((REL_BUCKETS, heads, 1), lambda i: (0, 0, 0)),
                  pl.BlockSpec((len(DILATIONS), r, heads, 1), lambda i: (0, 0, 0, 0)),
                  tok, tok, tok] + specs,
        out_specs=tok,
        scratch_shapes=[pltpu.VMEM((len(DILATIONS), r, heads, 1), F32)],
        compiler_params=_params(("arbitrary",)),
        name="sample_win_attn",
    )(rel_bias.reshape(REL_BUCKETS, heads, 1), jnp.asarray(bkt), q, k_new, v_new, *views)


def _sample_xattn_kernel(tb, q_ref, k_ref, v_ref, o_ref):
    for b in range(tb):
        o_ref[b] = _single_query_attention(q_ref[b], [(k_ref[b], v_ref[b], None)], None)


def _sample_xattn(qx, mem_k, mem_v, tb):
    nb, heads, hd = qx.shape
    n_mem = mem_k.shape[1]
    tok = pl.BlockSpec((tb, heads, hd), lambda i: (i, 0, 0))
    mem = pl.BlockSpec((tb, n_mem, heads, hd), lambda i: (i, 0, 0, 0))
    return pl.pallas_call(
        functools.partial(_sample_xattn_kernel, tb),
        out_shape=jax.ShapeDtypeStruct((nb, heads, hd), F32),
        grid=(nb // tb,),
        in_specs=[tok, mem, mem],
        out_specs=tok,
        compiler_params=_params(("parallel",)),
        name="sample_xattn",
    )(qx, mem_k, mem_v)


def _sample_mlstm_kernel(heads, tb, q_ref, k_ref, v_ref, mo_ref, li_ref, lf_ref, bi_ref, bf_ref,
                         gm_ref, c_ref, n_ref, m_ref, out_ref, c_out, n_out, m_out):
    qw = heads * MLSTM_DK
    hrow = lax.broadcasted_iota(jnp.int32, (LANES, qw), 0)
    hcol = lax.broadcasted_iota(jnp.int32, (LANES, qw), 1) // MLSTM_DK
    diag = hrow == hcol
    rows8 = slice(0, heads)
    for b in range(tb):
        q_bd = jnp.where(diag, q_ref[b:b + 1, :], 0.0) * QK_SCALE
        k_bd = jnp.where(diag, k_ref[b:b + 1, :], 0.0)
        c_b = c_ref[b]
        n_row = n_ref[b:b + 1, :]
        li = li_ref[b] + bi_ref[...]
        lf = _log_sigmoid(lf_ref[b] + bf_ref[...])
        m_prev = m_ref[b]
        qk = jnp.sum(q_bd * k_ref[b:b + 1, :], axis=-1, keepdims=True)[rows8]
        qn = jnp.sum(q_bd * n_row, axis=-1, keepdims=True)[rows8]
        q_c = _dot(q_bd.astype(BF16), c_b.astype(BF16))[rows8]
        inter = lf + m_prev
        m_t = jnp.maximum(inter, li)
        w_intra = jnp.exp(li - m_t)
        w_prev = jnp.exp(inter - m_t)
        sqk = qk * w_intra
        v_b = v_ref[b]
        num = sqk * v_b + w_prev * q_c
        den = sqk + w_prev * qn
        hh = num / jnp.maximum(jnp.abs(den), jnp.exp(-m_t))
        hn = hh * lax.rsqrt(jnp.mean(hh * hh, axis=-1, keepdims=True) + EPS)
        out_ref[b] = hn * gm_ref[...] * jax.nn.sigmoid(mo_ref[b])
        wv = jnp.concatenate([w_intra * v_b, jnp.zeros((LANES - heads, HEAD_DIM), F32)], axis=0)
        d_c = _dot_tn(k_bd.astype(BF16), wv.astype(BF16))
        for h in range(heads):
            hs = slice(h * MLSTM_DK, (h + 1) * MLSTM_DK)
            c_out[b, hs, :] = w_prev[h:h + 1, :] * c_b[hs, :] + d_c[hs, :]
        k_w = jnp.sum(k_bd[rows8] * w_intra, axis=0, keepdims=True)
        dec_row = jnp.sum(jnp.where(diag[rows8], w_prev, 0.0), axis=0, keepdims=True)
        n_out[b:b + 1, :] = dec_row * n_row + k_w
        m_out[b] = m_t


def _sample_mlstm(z6, gates, b_gate, g_mlstm, state_c, state_n, state_m, heads, tb):
    _, nb, w = z6.shape
    qw = heads * MLSTM_DK
    q = z6[3, :, :qw].astype(F32)
    k = z6[3, :, qw:].astype(F32)
    v = z6[4].astype(F32).reshape(nb, heads, HEAD_DIM)
    mo = z6[5].astype(F32).reshape(nb, heads, HEAD_DIM)
    li = gates[:, :heads].reshape(nb, heads, 1)
    lf = gates[:, heads:2 * heads].reshape(nb, heads, 1)
    row = lambda i: (i, 0)
    blk3 = lambda i: (i, 0, 0)
    const = lambda i: (0, 0)
    return pl.pallas_call(
        functools.partial(_sample_mlstm_kernel, heads, tb),
        out_shape=(jax.ShapeDtypeStruct((nb, heads, HEAD_DIM), F32),
                   jax.ShapeDtypeStruct((nb, qw, HEAD_DIM), F32),
                   jax.ShapeDtypeStruct((nb, qw), F32),
                   jax.ShapeDtypeStruct((nb, heads, 1), F32)),
        grid=(nb // tb,),
        in_specs=[pl.BlockSpec((tb, qw), row),
                  pl.BlockSpec((tb, qw), row),
                  pl.BlockSpec((tb, heads, HEAD_DIM), blk3),
                  pl.BlockSpec((tb, heads, HEAD_DIM), blk3),
                  pl.BlockSpec((tb, heads, 1), blk3),
                  pl.BlockSpec((tb, heads, 1), blk3),
                  pl.BlockSpec((heads, 1), const),
                  pl.BlockSpec((heads, 1), const),
                  pl.BlockSpec((heads, HEAD_DIM), const),
                  pl.BlockSpec((tb, qw, HEAD_DIM), blk3),
                  pl.BlockSpec((tb, qw), row),
                  pl.BlockSpec((tb, heads, 1), blk3)],
        out_specs=(pl.BlockSpec((tb, heads, HEAD_DIM), blk3),
                   pl.BlockSpec((tb, qw, HEAD_DIM), blk3),
                   pl.BlockSpec((tb, qw), row),
                   pl.BlockSpec((tb, heads, 1), blk3)),
        compiler_params=_params(("parallel",)),
        name="sample_mlstm",
    )(q, k, v, mo, li, lf, b_gate[:heads].reshape(heads, 1), b_gate[heads:].reshape(heads, 1),
      g_mlstm.reshape(heads, HEAD_DIM), state_c.reshape(nb, qw, HEAD_DIM), state_n.reshape(nb, qw),
      state_m.reshape(nb, heads, 1))


def _tile(m, cap):
    return min(m, cap)


def kernel(x_prompt, x_sample, mem_prompt, cache_win_k, cache_win_v, state_mlstm_C, state_mlstm_n,
           state_mlstm_m, cache_mem_k, cache_mem_v, rel_bias, g_pre_mix, g_post_mix, w_in, b_gate,
           g_mlstm, w_out, g_pre_xatt, g_post_xatt, g_mem, w_xq, w_xkv, w_xo, g_pre_mlp, g_post_mlp,
           w_up, w_down):
    depth = w_in.shape[0]
    assert depth == 1 and x_prompt.shape[0] == 1 and x_sample.shape[1] == 1
    _, seq, d_model = x_prompt.shape
    n_dec = x_sample.shape[0]
    att_heads = cache_win_k.shape[3]
    ml_heads = state_mlstm_C.shape[2]
    x_heads = cache_mem_k.shape[3]
    att_w = att_heads * HEAD_DIM
    ml_w = ml_heads * HEAD_DIM
    x_w = x_heads * HEAD_DIM
    n_mem = mem_prompt.shape[1]
    keep = cache_win_k.shape[2]
    n_main = 3 * att_w + 2 * ml_heads * MLSTM_DK + 2 * ml_w
    assert att_w == ml_w == 1024 and n_main % 1024 == 0 and seq >= keep

    def row(g):
        return g[0].reshape(1, -1)

    w_main = (w_in[0, :, :n_main].reshape(d_model, n_main // 1024, 1024).transpose(1, 0, 2)
              .astype(BF16))
    wg = w_in[0, :, n_main:].astype(BF16)
    w_out_b = w_out[0].astype(BF16)
    wq = w_xq[0].astype(BF16)
    wkv = w_xkv[0].astype(BF16)
    wo = w_xo[0].astype(BF16)
    wu = w_up[0].astype(BF16)
    wd = w_down[0].astype(BF16)
    rel_flat = rel_bias.reshape(-1)

    def tail(x1, qx_att, mem_attend, tm):
        x2 = _proj_post(mem_attend(qx_att), x1, wo, row(g_post_xatt), tm)
        return _mlp(x2, row(g_pre_mlp), wu, wd, row(g_post_mlp), tm, 1024)

    xp = x_prompt[0]
    mem_kv = _norm_matmul(mem_prompt[0], row(g_mem), wkv)
    mk_p, mv_p = mem_kv[:, :x_w], mem_kv[:, x_w:]
    z6, kv32, _, gates_t, k_t, *zd = _in_proj(xp, row(g_pre_mix), w_main, wg, _tile(seq, 512), True, keep)
    zd = [z6.reshape(z6.shape[0], 1, seq, att_w)] + zd
    parts = [_dil_attn(z, rel_flat, d, att_heads) for z, d in zip(zd, DILATIONS)]
    m_out, ce_p, m_p = _mlstm_prompt(z6, k_t, gates_t, b_gate[0], g_mlstm[0], ml_heads)
    c_p, n_p = ce_p[:, :HEAD_DIM], ce_p[:, HEAD_DIM]
    x1, qx = _out_proj(tuple(p[0] for p in parts) + tuple(p[1] for p in parts), m_out, xp, w_out_b,
                       row(g_post_mix), row(g_pre_xatt), wq, _tile(seq, 512), att_heads)
    tm = _tile(seq, 512)
    y_prompt = tail(x1, qx, lambda q: _xattn(q, mk_p, mv_p, tm, x_heads), tm)

    xs = x_sample[:, 0]
    z6s, kv32s, gates_s, _ = _in_proj(xs, row(g_pre_mix), w_main, wg, n_dec, False, n_dec)
    per_head = (n_dec, att_heads, HEAD_DIM)
    att_s = _sample_win_attn(z6s[0].astype(F32).reshape(per_head), kv32s[0].reshape(per_head),
                             kv32s[1].reshape(per_head), cache_win_k[0], cache_win_v[0], rel_bias, 4)
    mo_s, c_s, n_s, m_s = _sample_mlstm(z6s, gates_s, b_gate[0], g_mlstm[0], state_mlstm_C[0],
                                        state_mlstm_n[0], state_mlstm_m[0], ml_heads, 8)
    x1s, qxs = _out_proj((att_s.reshape(n_dec, att_w).astype(BF16),), mo_s.reshape(n_dec, ml_w).astype(BF16),
                         xs, w_out_b, row(g_post_mix), row(g_pre_xatt), wq, n_dec, att_heads)

    def sample_mem_attend(q):
        q3 = q.astype(F32).reshape(n_dec, x_heads, HEAD_DIM)
        return _sample_xattn(q3, cache_mem_k[0], cache_mem_v[0], 4).reshape(n_dec, x_w).astype(BF16)

    y_sample = tail(x1s, qxs, sample_mem_attend, n_dec)

    dk = MLSTM_DK
    return (y_prompt[None],
            y_sample[:, None],
            kv32[0].reshape(1, 1, keep, att_heads, HEAD_DIM),
            kv32[1].reshape(1, 1, keep, att_heads, HEAD_DIM),
            c_p.reshape(1, 1, ml_heads, dk, HEAD_DIM),
            n_p.reshape(1, 1, ml_heads, dk),
            m_p[:, 0].reshape(1, 1, ml_heads),
            mk_p.reshape(1, 1, n_mem, x_heads, HEAD_DIM),
            mv_p.reshape(1, 1, n_mem, x_heads, HEAD_DIM),
            kv32s[0].reshape(1, n_dec, 1, att_heads, HEAD_DIM),
            kv32s[1].reshape(1, n_dec, 1, att_heads, HEAD_DIM),
            c_s.reshape(1, n_dec, ml_heads, dk, HEAD_DIM),
            n_s.reshape(1, n_dec, ml_heads, dk),
            m_s.reshape(1, n_dec, ml_heads))
```

```python
import functools
import math

import numpy as np
import jax
import jax.numpy as jnp
from jax import lax
from jax.experimental import pallas as pl
from jax.experimental.pallas import tpu as pltpu

F32 = jnp.float32
BF16 = jnp.bfloat16

EPS = 1e-6
HEAD_DIM = 128
MLSTM_DK = 64
LANES = 128
DILATIONS = (1, 4, 16)
SUB_WINDOW = 128
ATT_BLOCK = 128
MLSTM_CHUNK = 128
REL_BUCKETS = 32
REL_MAX_DIST = 2048
NEG = -1e30
ATT_SCALE = HEAD_DIM ** -0.5
QK_SCALE = MLSTM_DK ** -0.5
VMEM_LIMIT = 56 * 2 ** 20


def _params(semantics):
    return pltpu.CompilerParams(dimension_semantics=semantics, vmem_limit_bytes=VMEM_LIMIT)


def _rms(xf, g):
    ms = jnp.mean(xf * xf, axis=-1, keepdims=True)
    return xf * lax.rsqrt(ms + EPS) * g


def _dot(a, b):
    return jnp.dot(a, b, preferred_element_type=F32)


def _dot_nt(a, b):
    return lax.dot_general(a, b, (((1,), (1,)), ((), ())), preferred_element_type=F32)


def _dot_tn(a, b):
    return lax.dot_general(a, b, (((0,), (0,)), ((), ())), preferred_element_type=F32)


def _log_sigmoid(x):
    return jnp.minimum(x, 0.0) - jnp.log(1.0 + jnp.exp(-jnp.abs(x)))


def _rel_bucket(dist):
    dist = np.asarray(dist)
    exact = REL_BUCKETS // 2
    far = exact + (np.log(np.maximum(dist, exact) / exact) / math.log(REL_MAX_DIST / exact)
                   * (REL_BUCKETS - exact)).astype(np.int32)
    return np.where(dist < exact, dist, np.minimum(far, REL_BUCKETS - 1)).astype(np.int32)


def _norm_matmul_kernel(x_ref, g_ref, w_ref, o_ref):
    xn = _rms(x_ref[...], g_ref[...]).astype(BF16)
    o_ref[...] = _dot(xn, w_ref[...])


def _norm_matmul(x, g, w):
    m, d = x.shape
    n = w.shape[1]
    return pl.pallas_call(
        _norm_matmul_kernel,
        out_shape=jax.ShapeDtypeStruct((m, n), F32),
        grid=(1,),
        in_specs=[pl.BlockSpec((m, d), lambda i: (0, 0)),
                  pl.BlockSpec((1, d), lambda i: (0, 0)),
                  pl.BlockSpec((d, n), lambda i: (0, 0))],
        out_specs=pl.BlockSpec((m, n), lambda i: (0, 0)),
        compiler_params=_params(("arbitrary",)),
        name="norm_matmul",
    )(x, g, w)


def _column_order(j):
    return (j % 2) * 3 + j // 2


def _in_proj_kernel(prompt, x_ref, g_ref, w_ref, wg_ref, wgt_ref, *rest):
    if prompt:
        z_ref, kv_ref, gate_ref, gate_t_ref, kt_ref, zd_mid, zd_wide, xn_ref, slab, slab_mid = rest
    else:
        z_ref, kv_ref, gate_ref, gate_t_ref, xn_ref = rest
    j = pl.program_id(1)

    @pl.when(j == 0)
    def _():
        xn = _rms(x_ref[...], g_ref[...]).astype(BF16)
        xn_ref[...] = xn
        gate_ref[...] = _dot(xn, wg_ref[...])
        gate_t_ref[...] = _dot_nt(wgt_ref[...], xn)

    tm, tn = z_ref.shape
    n_steps = 6

    def project(c):
        acc = _dot(xn_ref[...], w_ref[:, c * tn:(c + 1) * tn])
        z_ref[...] = acc.astype(BF16)
        return acc

    if not prompt:
        for step in range(n_steps):
            @pl.when(j == step)
            def _(step=step):
                acc = project(step)
                if step in (1, 2):
                    kv_ref[...] = acc
        return

    mid = DILATIONS[1]
    ratio = DILATIONS[2] // mid

    def even_step(step):
        acc = project(_column_order(step))
        for h in range(tn // LANES):
            slab[h] = acc[:, h * LANES:(h + 1) * LANES]
        if step >= 2:
            kv_ref[...] = acc

    def odd_step(step):
        acc = project(_column_order(step))
        for h in range(tn // LANES):
            hl = slice(h * LANES, (h + 1) * LANES)
            for r in range(mid):
                rows = slab[h, pl.ds(r, tm // mid, stride=mid), :]
                zd_mid[r, :, hl] = rows.astype(BF16)
                slab_mid[h, r] = rows
            for r in range(mid):
                for q in range(ratio):
                    rows = slab_mid[h, r, pl.ds(q, tm // (mid * ratio), stride=ratio), :]
                    zd_wide[r + mid * q, :, hl] = rows.astype(BF16)
        if step == 1:
            kt_ref[...] = acc[:, tn // 2:].T.astype(BF16)

    for step in range(n_steps):
        pl.when(j == step)(functools.partial(odd_step if step % 2 else even_step, step))


def _in_proj(x, g, w_all, wg, tm, prompt, kv_rows):
    m, d = x.shape
    tn = 1024
    nblk = 6
    n_gate = wg.shape[1]
    wg_pad = jnp.zeros((d, LANES), BF16).at[:, :n_gate].set(wg)
    dils = DILATIONS[1:] if prompt else ()
    assert DILATIONS[2] % DILATIONS[1] == 0 and nblk == 6 and (m - kv_rows) % tm == 0
    const = lambda i, j: (0, 0)
    skip = (m - kv_rows) // tm
    if prompt:
        col = _column_order
        kv_c = lambda j: jnp.where(j < 4, 0, 1)
    else:
        col = lambda j: j
        kv_c = lambda j: jnp.clip(j - 1, 0, 1)
    kv_blk = lambda i, j: (jnp.where(i >= skip, kv_c(j), 0), jnp.maximum(i - skip, 0), 0)
    extra_out, extra_out_specs, extra_scratch = [], [], []
    if prompt:
        extra_out = [jax.ShapeDtypeStruct((tn // 2, m), BF16)] + [
            jax.ShapeDtypeStruct((3, dl, m // dl, tn), BF16) for dl in dils]
        extra_out_specs = [pl.BlockSpec((tn // 2, tm), lambda i, j: (0, i))] + [
            pl.BlockSpec((None, dl, tm // dl, tn), lambda i, j: (j // 2, 0, i, 0)) for dl in dils]
        extra_scratch = [pltpu.VMEM((tn // LANES, tm, LANES), F32),
                         pltpu.VMEM((tn // LANES, DILATIONS[1], tm // DILATIONS[1], LANES), F32)]
    return pl.pallas_call(
        functools.partial(_in_proj_kernel, prompt),
        out_shape=tuple([jax.ShapeDtypeStruct((nblk, m, tn), BF16),
                         jax.ShapeDtypeStruct((2, kv_rows, tn), F32),
                         jax.ShapeDtypeStruct((m, LANES), F32),
                         jax.ShapeDtypeStruct((n_gate, m), F32)] + extra_out),
        grid=(m // tm, nblk),
        in_specs=[pl.BlockSpec((tm, d), lambda i, j: (i, 0)),
                  pl.BlockSpec((1, d), const),
                  pl.BlockSpec(w_all.shape, const, pipeline_mode=pl.Buffered(1)),
                  pl.BlockSpec((d, LANES), const),
                  pl.BlockSpec((n_gate, d), const)],
        out_specs=tuple([pl.BlockSpec((None, tm, tn), lambda i, j: (col(j), i, 0)),
                         pl.BlockSpec((None, tm, tn), kv_blk),
                         pl.BlockSpec((tm, LANES), lambda i, j: (i, 0)),
                         pl.BlockSpec((n_gate, tm), lambda i, j: (0, i))] + extra_out_specs),
        scratch_shapes=[pltpu.VMEM((tm, d), BF16)] + extra_scratch,
        compiler_params=_params(("parallel", "arbitrary")),
        name="in_proj_dilated" if prompt else "in_proj",
    )(x, g, w_all, wg_pad, wg.T)


def _dil_attn_kernel(heads, d, nq, rb_ref, bkt_ref, q_ref, kc_ref, kp_ref, vc_ref, vp_ref,
                     o_ref, lse_ref, bias, s_scr, p_scr):
    n = pl.program_id(0)
    r = pl.program_id(1)

    @pl.when((n == 0) & (r == 0))
    def _():
        bkt = bkt_ref[...]
        for h in range(heads):
            acc = jnp.full(bkt.shape, NEG, F32)
            for k in range(REL_BUCKETS):
                acc = jnp.where(bkt == k, rb_ref[k * heads + h], acc)
            bias[h] = acc

    blk = ATT_BLOCK
    in_prev = lax.broadcasted_iota(jnp.int32, (blk, 2 * blk), 1) < blk
    for u in range(nq):
        for h in range(heads):
            sl = slice(h * HEAD_DIM, (h + 1) * HEAD_DIM)
            qh = q_ref[u * blk:(u + 1) * blk, sl]
            if u == 0:
                keys = jnp.concatenate([kp_ref[:, sl], kc_ref[:blk, sl]], axis=0)
            else:
                keys = kc_ref[(u - 1) * blk:(u + 1) * blk, sl]
            s = _dot_nt(qh, keys) * ATT_SCALE + bias[h]
            if u == 0:
                s = jnp.where((n == 0) & in_prev, NEG, s)
            s_scr[u * heads + h] = s
    lane = lax.broadcasted_iota(jnp.int32, (blk, LANES), 1)
    m_tiles = []
    for u in range(nq):
        m_tile = jnp.zeros((blk, LANES), F32)
        for h in range(heads):
            s = s_scr[u * heads + h]
            m = jnp.max(s, axis=-1, keepdims=True)
            p_scr[u * heads + h] = jnp.exp(s - m).astype(BF16)
            m_tile = jnp.where(lane == h, m, m_tile)
        m_tiles.append(m_tile)
    ones = jnp.ones((2 * blk, HEAD_DIM), BF16)
    for u in range(nq):
        rows = pl.ds(r + u * blk * d, blk, stride=d)
        lse_tile = m_tiles[u]
        for h in range(heads):
            sl = slice(h * HEAD_DIM, (h + 1) * HEAD_DIM)
            if u == 0:
                vals = jnp.concatenate([vp_ref[:, sl], vc_ref[:blk, sl]], axis=0)
            else:
                vals = vc_ref[(u - 1) * blk:(u + 1) * blk, sl]
            o = _dot(p_scr[u * heads + h], jnp.concatenate([vals, ones], axis=1))
            l = o[:, HEAD_DIM:]
            o_ref[h, rows, :] = o[:, :HEAD_DIM] / l
            lse_tile = jnp.where(lane == h, lse_tile + jnp.log(l), lse_tile)
        lse_ref[rows, :] = lse_tile


DIL_ATTN_OUT_BLOCK_BYTES = 16 * 2 ** 20


def _dil_attn(zd, rel_flat, dilation, heads):
    d = dilation
    sub_len, w = zd.shape[2], zd.shape[3]
    s = sub_len * d
    blk = ATT_BLOCK
    nq = 2 if 2 * blk * d * w * 4 <= DIL_ATTN_OUT_BLOCK_BYTES else 1
    nb = sub_len // (nq * blk)
    qi = np.arange(blk)[:, None]
    ki = np.arange(2 * blk)[None, :]
    dist = blk + qi - ki
    bkt = np.where((dist >= 0) & (dist <= SUB_WINDOW), _rel_bucket(np.maximum(dist, 0) * d), -1).astype(np.int32)

    def cur(c):
        return pl.BlockSpec((None, None, nq * blk, w), lambda n, r: (c, r, n, 0))

    def prev(c):
        return pl.BlockSpec((None, None, blk, w), lambda n, r: (c, r, jnp.maximum(nq * n - 1, 0), 0))

    return pl.pallas_call(
        functools.partial(_dil_attn_kernel, heads, d, nq),
        out_shape=(jax.ShapeDtypeStruct((heads, s, HEAD_DIM), F32),
                   jax.ShapeDtypeStruct((s, LANES), F32)),
        grid=(nb, d),
        in_specs=[pl.BlockSpec(memory_space=pltpu.SMEM),
                  pl.BlockSpec((blk, 2 * blk), lambda n, r: (0, 0)),
                  cur(0), cur(1), prev(1), cur(2), prev(2)],
        out_specs=(pl.BlockSpec((heads, nq * blk * d, HEAD_DIM), lambda n, r: (0, n, 0)),
                   pl.BlockSpec((nq * blk * d, LANES), lambda n, r: (n, 0))),
        scratch_shapes=[pltpu.VMEM((heads, blk, 2 * blk), F32),
                        pltpu.VMEM((nq * heads, blk, 2 * blk), F32),
                        pltpu.VMEM((nq * heads, blk, 2 * blk), BF16)],
        compiler_params=_params(("arbitrary", "arbitrary")),
        name=f"dil_attn_d{d}",
    )(rel_flat, jnp.asarray(bkt), zd, zd, zd, zd, zd)


def _split_bf16(x):
    hi = x.astype(BF16)
    return hi, (x - hi.astype(F32)).astype(BF16)


def _mlstm_kernel(heads, qk_ref, kt_ref, v_ref, mo_ref, gt_ref, bgc_ref, gm_ref,
                  ce0_ref, m0_ref, out_ref, ce_out, m_out, ce_s, m_s, sqk_s, tile_s):
    step = pl.program_id(0)
    t = MLSTM_CHUNK
    qw = heads * MLSTM_DK
    dk = MLSTM_DK

    @pl.when(step == 0)
    def _():
        ce_s[...] = ce0_ref[...]
        m_s[...] = m0_ref[...]

    row = lax.broadcasted_iota(jnp.int32, (t, t), 0)
    col = lax.broadcasted_iota(jnp.int32, (t, t), 1)
    causal = col <= row
    eye = col == row
    upper = jnp.where(row <= col, 1.0, 0.0).astype(BF16)
    lane = lax.broadcasted_iota(jnp.int32, (1, LANES), 1)
    lane_h = lax.broadcasted_iota(jnp.int32, (heads, t), 1)
    blk_r = lax.broadcasted_iota(jnp.int32, (2 * t, 2 * t), 0) // t
    blk_c = lax.broadcasted_iota(jnp.int32, (2 * t, 2 * t), 1) // t
    ones_bd = jnp.where(blk_r == blk_c, 1.0, 0.0).astype(BF16)
    ones = jnp.ones((t, HEAD_DIM), BF16)

    gt = gt_ref[...] + bgc_ref[...]
    hi, lo = _split_bf16(_log_sigmoid(gt[heads:]))
    b2 = _dot(jnp.concatenate([hi, lo], axis=0), upper)
    b = b2[:heads] + b2[heads:]
    c = gt[:heads] - b
    m_prev = m_s[...]
    cmax = c
    for shift in (1, 2, 4, 8, 16, 32, 64):
        cmax = jnp.maximum(cmax, jnp.where(lane_h >= shift, pltpu.roll(cmax, shift, axis=1), NEG))
    a = jnp.maximum(cmax, m_prev)
    m_t = b + a
    a_last = a[:, t - 1:t]
    b_last = b[:, t - 1:t]
    m_new = b_last + a_last
    w_k = jnp.exp(c - a_last)
    decay = jnp.exp(b_last + m_prev - m_new)
    a_hi, a_lo = _split_bf16(a)
    mt_hi, mt_lo = _split_bf16(m_t)

    def diag(x_row):
        return jnp.where(eye, x_row.astype(F32), 0.0).astype(BF16)

    qms, dens = [], []
    for h in range(heads):
        pair, half = divmod(h, 2)
        hs = slice(h, h + 1)
        tile = (_dot(jnp.concatenate([diag(a_hi[hs]), diag(mt_hi[hs])], axis=1), ones_bd)
                + _dot(jnp.concatenate([diag(a_lo[hs]), diag(mt_lo[hs])], axis=1), ones_bd))
        tile_s[h] = tile
        w_intra = jnp.exp(jnp.where(causal, c[hs] - tile[:, :t], NEG))
        mine = (lane >= half * dk) & (lane < (half + 1) * dk)
        qm = jnp.where(mine, qk_ref[:, pair * LANES:(pair + 1) * LANES], 0.0) * QK_SCALE
        km = jnp.where(mine, qk_ref[:, qw + pair * LANES:qw + (pair + 1) * LANES], 0.0)
        sqk = _dot_nt(qm, km) * w_intra
        sqk_s[h] = sqk.astype(BF16)
        dens.append(jnp.sum(sqk, axis=-1, keepdims=True))
        qms.append(qm)

    for h in range(heads):
        pair = h // 2
        sl = slice(h * HEAD_DIM, (h + 1) * HEAD_DIM)
        tile = tile_s[h]
        w_prev = jnp.exp(m_prev[h:h + 1, :] - tile[:, :t])
        q_ce = _dot(qms[h], ce_s[pair * LANES:(pair + 1) * LANES, :].astype(BF16))
        num = _dot(sqk_s[h], v_ref[:, sl]) + w_prev * q_ce[:, :HEAD_DIM]
        den = dens[h] + w_prev * q_ce[:, HEAD_DIM:]
        hh = num / jnp.maximum(jnp.abs(den), jnp.exp(-tile[:, t:]))
        hn = hh * lax.rsqrt(jnp.mean(hh * hh, axis=-1, keepdims=True) + EPS)
        gate = jax.nn.sigmoid(mo_ref[:, sl].astype(F32))
        out_ref[:, sl] = (hn * gm_ref[:, sl] * gate).astype(out_ref.dtype)

    for h in range(heads):
        hs = slice(h * dk, (h + 1) * dk)
        kw = (kt_ref[hs, :].astype(F32) * w_k[h:h + 1, :]).astype(BF16)
        v_ext = jnp.concatenate([v_ref[:, h * HEAD_DIM:(h + 1) * HEAD_DIM], ones], axis=1)
        dec = jnp.concatenate([decay[h:h + 1, :], decay[h:h + 1, :]], axis=1)
        ce_s[hs, :] = dec * ce_s[hs, :] + _dot(kw, v_ext)
    m_s[...] = jnp.broadcast_to(m_new, (heads, LANES))

    @pl.when(step == pl.num_programs(0) - 1)
    def _():
        ce_out[...] = ce_s[...]
        m_out[...] = m_s[...]


def _mlstm_prompt(z6, k_t, gates_t, b_gate, g_mlstm, heads):
    _, s, w = z6.shape
    t = MLSTM_CHUNK
    qw = heads * MLSTM_DK
    ce0 = jnp.zeros((qw, 2 * HEAD_DIM), F32)
    m0 = jnp.zeros((heads, LANES), F32)
    const = lambda c: (0, 0)
    return pl.pallas_call(
        functools.partial(_mlstm_kernel, heads),
        out_shape=(jax.ShapeDtypeStruct((s, w), BF16),
                   jax.ShapeDtypeStruct((qw, 2 * HEAD_DIM), F32),
                   jax.ShapeDtypeStruct((heads, LANES), F32)),
        grid=(s // t,),
        in_specs=[pl.BlockSpec((None, t, w), lambda c: (3, c, 0)),
                  pl.BlockSpec((qw, t), lambda c: (0, c)),
                  pl.BlockSpec((None, t, w), lambda c: (4, c, 0)),
                  pl.BlockSpec((None, t, w), lambda c: (5, c, 0)),
                  pl.BlockSpec((2 * heads, t), lambda c: (0, c)),
                  pl.BlockSpec((2 * heads, 1), const),
                  pl.BlockSpec((1, w), const),
                  pl.BlockSpec((qw, 2 * HEAD_DIM), const),
                  pl.BlockSpec((heads, LANES), const)],
        out_specs=(pl.BlockSpec((t, w), lambda c: (c, 0)),
                   pl.BlockSpec((qw, 2 * HEAD_DIM), const),
                   pl.BlockSpec((heads, LANES), const)),
        scratch_shapes=[pltpu.VMEM((qw, 2 * HEAD_DIM), F32),
                        pltpu.VMEM((heads, LANES), F32),
                        pltpu.VMEM((heads, t, t), BF16),
                        pltpu.VMEM((heads, t, 2 * t), F32)],
        compiler_params=_params(("arbitrary",)),
        name="mlstm_prompt",
    )(z6, k_t, z6, z6, gates_t, b_gate.reshape(2 * heads, 1), g_mlstm.reshape(1, w), ce0, m0)


def _out_proj_kernel(merge, heads, *refs):
    if merge:
        (o1, o2, o3, l1, l2, l3, mo_ref, x_ref, wa_ref, wm_ref, gpost_ref, gpre_ref, wq_ref,
         x1_ref, qx_ref, att_s) = refs
        la, lb, lc = l1[...], l2[...], l3[...]
        mx = jnp.maximum(jnp.maximum(la, lb), lc)
        ea, eb, ec = jnp.exp(la - mx), jnp.exp(lb - mx), jnp.exp(lc - mx)
        tot = ea + eb + ec
        wa_, wb_, wc_ = ea / tot, eb / tot, ec / tot
        for h in range(heads):
            sl = slice(h * HEAD_DIM, (h + 1) * HEAD_DIM)
            mix = wa_[:, h:h + 1] * o1[h] + wb_[:, h:h + 1] * o2[h] + wc_[:, h:h + 1] * o3[h]
            att_s[:, sl] = mix.astype(BF16)
        att = att_s[...]
    else:
        att_ref, mo_ref, x_ref, wa_ref, wm_ref, gpost_ref, gpre_ref, wq_ref, x1_ref, qx_ref = refs
        att = att_ref[...]
    y = _dot(att, wa_ref[...]) + _dot(mo_ref[...], wm_ref[...])
    x1 = x_ref[...] + _rms(y, gpost_ref[...])
    x1_ref[...] = x1
    u = _rms(x1, gpre_ref[...]).astype(BF16)
    qx_ref[...] = _dot(u, wq_ref[...]).astype(qx_ref.dtype)


def _out_proj(att_parts, m_out, x, w_out, g_post, g_pre, w_xq, tm, heads):
    m, d = x.shape
    wm = m_out.shape[1]
    wa = w_out.shape[0] - wm
    assert wa == wm
    nq = w_xq.shape[1]
    merge = len(att_parts) == 6
    row = lambda i: (i, 0)
    const = lambda i: (0, 0)
    if merge:
        part_specs = ([pl.BlockSpec((heads, tm, HEAD_DIM), lambda i: (0, i, 0))] * 3
                      + [pl.BlockSpec((tm, LANES), row)] * 3)
    else:
        part_specs = [pl.BlockSpec((tm, wa), row)]
    return pl.pallas_call(
        functools.partial(_out_proj_kernel, merge, heads),
        out_shape=(jax.ShapeDtypeStruct((m, d), F32), jax.ShapeDtypeStruct((m, nq), BF16)),
        grid=(m // tm,),
        in_specs=part_specs + [pl.BlockSpec((tm, wm), row),
                               pl.BlockSpec((tm, d), row),
                               pl.BlockSpec((wa, d), const, pipeline_mode=pl.Buffered(1)),
                               pl.BlockSpec((wm, d), lambda i: (1, 0), pipeline_mode=pl.Buffered(1)),
                               pl.BlockSpec((1, d), const),
                               pl.BlockSpec((1, d), const),
                               pl.BlockSpec((d, nq), const, pipeline_mode=pl.Buffered(1))],
        out_specs=(pl.BlockSpec((tm, d), row), pl.BlockSpec((tm, nq), row)),
        scratch_shapes=[pltpu.VMEM((tm, wa), BF16)] if merge else [],
        compiler_params=_params(("parallel",)),
        name="out_proj_merge" if merge else "out_proj",
    )(*att_parts, m_out, x, w_out, w_out, g_post, g_pre, w_xq)


def _xattn_kernel(heads, q_ref, mk_ref, mv_ref, o_ref):
    for h in range(heads):
        sl = slice(h * HEAD_DIM, (h + 1) * HEAD_DIM)
        s = _dot_nt(q_ref[:, sl], mk_ref[:, sl].astype(BF16)) * ATT_SCALE
        m = jnp.max(s, axis=-1, keepdims=True)
        p = jnp.exp(s - m)
        l = jnp.sum(p, axis=-1, keepdims=True)
        o = _dot(p.astype(BF16), mv_ref[:, sl].astype(BF16))
        o_ref[:, sl] = (o / l).astype(o_ref.dtype)


def _xattn(qx, mem_k, mem_v, tm, heads):
    m, w = qx.shape
    n_mem = mem_k.shape[0]
    return pl.pallas_call(
        functools.partial(_xattn_kernel, heads),
        out_shape=jax.ShapeDtypeStruct((m, w), BF16),
        grid=(m // tm,),
        in_specs=[pl.BlockSpec((tm, w), lambda i: (i, 0)),
                  pl.BlockSpec((n_mem, w), lambda i: (0, 0)),
                  pl.BlockSpec((n_mem, w), lambda i: (0, 0))],
        out_specs=pl.BlockSpec((tm, w), lambda i: (i, 0)),
        compiler_params=_params(("parallel",)),
        name="xattn_prompt",
    )(qx, mem_k, mem_v)


def _proj_post_kernel(a_ref, x_ref, w_ref, g_ref, y_ref):
    y_ref[...] = x_ref[...] + _rms(_dot(a_ref[...], w_ref[...]), g_ref[...])


def _proj_post(a, x, w, g, tm):
    m, d = x.shape
    k = a.shape[1]
    return pl.pallas_call(
        _proj_post_kernel,
        out_shape=jax.ShapeDtypeStruct((m, d), F32),
        grid=(m // tm,),
        in_specs=[pl.BlockSpec((tm, k), lambda i: (i, 0)),
                  pl.BlockSpec((tm, d), lambda i: (i, 0)),
                  pl.BlockSpec((k, d), lambda i: (0, 0)),
                  pl.BlockSpec((1, d), lambda i: (0, 0))],
        out_specs=pl.BlockSpec((tm, d), lambda i: (i, 0)),
        compiler_params=_params(("parallel",)),
        name="proj_post",
    )(a, x, w, g)


def _mlp_kernel(n_side, x_ref, gpre_ref, wu_ref, wd_ref, gpost_ref, *rest):
    if n_side:
        sq_ref, sk_ref, sv_ref, y_ref, so_ref, xn_s, acc_s = rest
    else:
        y_ref, xn_s, acc_s = rest
    j = pl.program_id(1)

    @pl.when(j == 0)
    def _():
        xn_s[...] = _rms(x_ref[...], gpre_ref[...]).astype(BF16)
        acc_s[...] = jnp.zeros(acc_s.shape, F32)

    hidden = jnp.square(jnp.maximum(_dot(xn_s[...], wu_ref[...]), 0.0))
    acc_s[...] += _dot(hidden.astype(BF16), wd_ref[...])
    for b in range(n_side):
        so_ref[b] = _single_query_attention(sq_ref[b], [(sk_ref[b], sv_ref[b], None)], None)

    @pl.when(j == pl.num_programs(1) - 1)
    def _():
        y_ref[...] = x_ref[...] + _rms(acc_s[...], gpost_ref[...])


def _mlp(x, g_pre, w_up, w_down, g_post, tm, tf, side=None):
    m, d = x.shape
    ff = w_up.shape[1]
    steps_j = ff // tf
    n_steps = (m // tm) * steps_j
    in_specs = [pl.BlockSpec((tm, d), lambda i, j: (i, 0)),
                pl.BlockSpec((1, d), lambda i, j: (0, 0)),
                pl.BlockSpec((d, tf), lambda i, j: (0, j)),
                pl.BlockSpec((tf, d), lambda i, j: (j, 0)),
                pl.BlockSpec((1, d), lambda i, j: (0, 0))]
    out_shape = [jax.ShapeDtypeStruct((m, d), F32)]
    out_specs = [pl.BlockSpec((tm, d), lambda i, j: (i, 0))]
    n_side = 0
    if side is not None:
        sq, sk, sv = side
        nb, heads, hd = sq.shape
        n_side = nb // n_steps
        assert n_side * n_steps == nb
        tok = pl.BlockSpec((n_side, heads, hd), lambda i, j: (i * steps_j + j, 0, 0))
        mem = pl.BlockSpec((n_side,) + sk.shape[1:], lambda i, j: (i * steps_j + j, 0, 0, 0))
        in_specs += [tok, mem, mem]
        out_shape.append(jax.ShapeDtypeStruct((nb, heads, hd), F32))
        out_specs.append(tok)
    out = pl.pallas_call(
        functools.partial(_mlp_kernel, n_side),
        out_shape=tuple(out_shape),
        grid=(m // tm, steps_j),
        in_specs=in_specs,
        out_specs=tuple(out_specs),
        scratch_shapes=[pltpu.VMEM((tm, d), BF16), pltpu.VMEM((tm, d), F32)],
        compiler_params=_params(("parallel", "arbitrary")),
        name="mlp_side" if side is not None else "mlp",
    )(x, g_pre, w_up, w_down, g_post, *(side or ()))
    return out if side is not None else out[0]


def _single_query_attention(q, groups, new):
    q = q * ATT_SCALE
    scores = []
    for k3, _, b3 in groups:
        s = jnp.sum(k3 * q[None], axis=-1, keepdims=True)
        scores.append(s if b3 is None else s + b3)
    mx = jnp.max(scores[0], axis=0)
    for s in scores[1:]:
        mx = jnp.maximum(mx, jnp.max(s, axis=0))
    if new is not None:
        k_new, v_new, b_new, count = new
        s_new = jnp.sum(k_new * q, axis=-1, keepdims=True) + b_new
        mx = jnp.maximum(mx, s_new)
    den = jnp.zeros(mx.shape, F32)
    acc = jnp.zeros(q.shape, F32)
    for s, (_, v3, _) in zip(scores, groups):
        p = jnp.exp(s - mx[None])
        den = den + jnp.sum(p, axis=0)
        acc = acc + jnp.sum(p * v3, axis=0)
    if new is not None:
        p_new = jnp.exp(s_new - mx) * count
        den = den + p_new
        acc = acc + p_new * v_new
    return acc / den


def _sample_win_attn_kernel(tb, rel_ref, bkt_ref, q_ref, kn_ref, vn_ref,
                            k1, k4, k16, v1, v4, v16, o_ref, bias_s):
    @pl.when(pl.program_id(0) == 0)
    def _():
        for p in range(len(DILATIONS)):
            bkt = bkt_ref[p]
            acc = jnp.zeros(bkt.shape, F32)
            for k in range(REL_BUCKETS):
                acc = jnp.where(bkt == k, rel_ref[k][None], acc)
            bias_s[p] = acc

    for b in range(tb):
        groups = [(k[b], v[b], bias_s[p]) for p, (k, v) in enumerate(((k1, v1), (k4, v4), (k16, v16)))]
        new = (kn_ref[b], vn_ref[b], rel_ref[0], float(len(DILATIONS)))
        o_ref[b] = _single_query_attention(q_ref[b], groups, new)


def _sample_win_attn(q, k_new, v_new, cache_k, cache_v, rel_bias, tb):
    nb, heads, hd = q.shape
    buf = cache_k.shape[1]
    r = SUB_WINDOW
    assert buf == r * DILATIONS[-1], "window buffer must hold exactly the widest dilated pattern"
    j = r - np.arange(r)
    bkt = np.stack([_rel_bucket(j * d) for d in DILATIONS]).astype(np.int32)
    bkt = np.broadcast_to(bkt[:, :, None, None], (len(DILATIONS), r, heads, 1))
    views, specs = [], []
    for cache in (cache_k, cache_v):
        for d in DILATIONS:
            views.append(cache.reshape(nb, buf // d, d, heads, hd))
            specs.append(pl.BlockSpec((tb, r, None, heads, hd), functools.partial(
                lambda i, blk: (i, blk, 0, 0, 0), blk=buf // d // r - 1)))
    tok = pl.BlockSpec((tb, heads, hd), lambda i: (i, 0, 0))
    return pl.pallas_call(
        functools.partial(_sample_win_attn_kernel, tb),
        out_shape=jax.ShapeDtypeStruct((nb, heads, hd), F32),
        grid=(nb // tb,),
        in_specs=[pl.BlockSpec((REL_BUCKETS, heads, 1), lambda i: (0, 0, 0)),
                  pl.BlockSpec((len(DILATIONS), r, heads, 1), lambda i: (0, 0, 0, 0)),
                  tok, tok, tok] + specs,
        out_specs=tok,
        scratch_shapes=[pltpu.VMEM((len(DILATIONS), r, heads, 1), F32)],
        compiler_params=_params(("arbitrary",)),
        name="sample_win_attn",
    )(rel_bias.reshape(REL_BUCKETS, heads, 1), jnp.asarray(bkt), q, k_new, v_new, *views)


def _sample_mlstm_kernel(heads, tb, q_ref, k_ref, v_ref, mo_ref, li_ref, lf_ref, bi_ref, bf_ref,
                         gm_ref, c_ref, n_ref, m_ref, out_ref, c_out, n_out, m_out):
    qw = heads * MLSTM_DK
    hrow = lax.broadcasted_iota(jnp.int32, (LANES, qw), 0)
    hcol = lax.broadcasted_iota(jnp.int32, (LANES, qw), 1) // MLSTM_DK
    diag = hrow == hcol
    rows8 = slice(0, heads)
    for b in range(tb):
        q_bd = jnp.where(diag, q_ref[b:b + 1, :], 0.0) * QK_SCALE
        k_bd = jnp.where(diag, k_ref[b:b + 1, :], 0.0)
        c_b = c_ref[b]
        n_row = n_ref[b:b + 1, :]
        li = li_ref[b] + bi_ref[...]
        lf = _log_sigmoid(lf_ref[b] + bf_ref[...])
        m_prev = m_ref[b]
        qk = jnp.sum(q_bd * k_ref[b:b + 1, :], axis=-1, keepdims=True)[rows8]
        qn = jnp.sum(q_bd * n_row, axis=-1, keepdims=True)[rows8]
        q_c = _dot(q_bd.astype(BF16), c_b.astype(BF16))[rows8]
        inter = lf + m_prev
        m_t = jnp.maximum(inter, li)
        w_intra = jnp.exp(li - m_t)
        w_prev = jnp.exp(inter - m_t)
        sqk = qk * w_intra
        v_b = v_ref[b]
        num = sqk * v_b + w_prev * q_c
        den = sqk + w_prev * qn
        hh = num / jnp.maximum(jnp.abs(den), jnp.exp(-m_t))
        hn = hh * lax.rsqrt(jnp.mean(hh * hh, axis=-1, keepdims=True) + EPS)
        out_ref[b] = hn * gm_ref[...] * jax.nn.sigmoid(mo_ref[b])
        wv = jnp.concatenate([w_intra * v_b, jnp.zeros((LANES - heads, HEAD_DIM), F32)], axis=0)
        d_c = _dot_tn(k_bd.astype(BF16), wv.astype(BF16))
        for h in range(heads):
            hs = slice(h * MLSTM_DK, (h + 1) * MLSTM_DK)
            c_out[b, hs, :] = w_prev[h:h + 1, :] * c_b[hs, :] + d_c[hs, :]
        k_w = jnp.sum(k_bd[rows8] * w_intra, axis=0, keepdims=True)
        dec_row = jnp.sum(jnp.where(diag[rows8], w_prev, 0.0), axis=0, keepdims=True)
        n_out[b:b + 1, :] = dec_row * n_row + k_w
        m_out[b] = m_t


def _sample_mlstm(z6, gates, b_gate, g_mlstm, state_c, state_n, state_m, heads, tb):
    _, nb, w = z6.shape
    qw = heads * MLSTM_DK
    q = z6[3, :, :qw].astype(F32)
    k = z6[3, :, qw:].astype(F32)
    v = z6[4].astype(F32).reshape(nb, heads, HEAD_DIM)
    mo = z6[5].astype(F32).reshape(nb, heads, HEAD_DIM)
    li = gates[:, :heads].reshape(nb, heads, 1)
    lf = gates[:, heads:2 * heads].reshape(nb, heads, 1)
    row = lambda i: (i, 0)
    blk3 = lambda i: (i, 0, 0)
    const = lambda i: (0, 0)
    return pl.pallas_call(
        functools.partial(_sample_mlstm_kernel, heads, tb),
        out_shape=(jax.ShapeDtypeStruct((nb, heads, HEAD_DIM), F32),
                   jax.ShapeDtypeStruct((nb, qw, HEAD_DIM), F32),
                   jax.ShapeDtypeStruct((nb, qw), F32),
                   jax.ShapeDtypeStruct((nb, heads, 1), F32)),
        grid=(nb // tb,),
        in_specs=[pl.BlockSpec((tb, qw), row),
                  pl.BlockSpec((tb, qw), row),
                  pl.BlockSpec((tb, heads, HEAD_DIM), blk3),
                  pl.BlockSpec((tb, heads, HEAD_DIM), blk3),
                  pl.BlockSpec((tb, heads, 1), blk3),
                  pl.BlockSpec((tb, heads, 1), blk3),
                  pl.BlockSpec((heads, 1), const),
                  pl.BlockSpec((heads, 1), const),
                  pl.BlockSpec((heads, HEAD_DIM), const),
                  pl.BlockSpec((tb, qw, HEAD_DIM), blk3),
                  pl.BlockSpec((tb, qw), row),
                  pl.BlockSpec((tb, heads, 1), blk3)],
        out_specs=(pl.BlockSpec((tb, heads, HEAD_DIM), blk3),
                   pl.BlockSpec((tb, qw, HEAD_DIM), blk3),
                   pl.BlockSpec((tb, qw), row),
                   pl.BlockSpec((tb, heads, 1), blk3)),
        compiler_params=_params(("parallel",)),
        name="sample_mlstm",
    )(q, k, v, mo, li, lf, b_gate[:heads].reshape(heads, 1), b_gate[heads:].reshape(heads, 1),
      g_mlstm.reshape(heads, HEAD_DIM), state_c.reshape(nb, qw, HEAD_DIM), state_n.reshape(nb, qw),
      state_m.reshape(nb, heads, 1))


def _tile(m, cap):
    return min(m, cap)


def kernel(x_prompt, x_sample, mem_prompt, cache_win_k, cache_win_v, state_mlstm_C, state_mlstm_n,
           state_mlstm_m, cache_mem_k, cache_mem_v, rel_bias, g_pre_mix, g_post_mix, w_in, b_gate,
           g_mlstm, w_out, g_pre_xatt, g_post_xatt, g_mem, w_xq, w_xkv, w_xo, g_pre_mlp, g_post_mlp,
           w_up, w_down):
    depth = w_in.shape[0]
    assert depth == 1 and x_prompt.shape[0] == 1 and x_sample.shape[1] == 1
    _, seq, d_model = x_prompt.shape
    n_dec = x_sample.shape[0]
    att_heads = cache_win_k.shape[3]
    ml_heads = state_mlstm_C.shape[2]
    x_heads = cache_mem_k.shape[3]
    att_w = att_heads * HEAD_DIM
    ml_w = ml_heads * HEAD_DIM
    x_w = x_heads * HEAD_DIM
    n_mem = mem_prompt.shape[1]
    keep = cache_win_k.shape[2]
    n_main = 3 * att_w + 2 * ml_heads * MLSTM_DK + 2 * ml_w
    assert att_w == ml_w == 1024 and n_main % 1024 == 0 and seq >= keep

    def row(g):
        return g[0].reshape(1, -1)

    assert n_main == 6 * 1024
    w_main = w_in[0].astype(BF16)
    wg = w_in[0, :, n_main:].astype(BF16)
    w_out_b = w_out[0].astype(BF16)
    wq = w_xq[0].astype(BF16)
    wkv = w_xkv[0].astype(BF16)
    wo = w_xo[0].astype(BF16)
    wu = w_up[0].astype(BF16)
    wd = w_down[0].astype(BF16)
    rel_flat = rel_bias.reshape(-1)

    def mlp(x2, tm, side=None):
        return _mlp(x2, row(g_pre_mlp), wu, wd, row(g_post_mlp), tm, 1024, side)

    xp = x_prompt[0]
    mem_kv = _norm_matmul(mem_prompt[0], row(g_mem), wkv)
    mk_p, mv_p = mem_kv[:, :x_w], mem_kv[:, x_w:]
    z6, kv32, _, gates_t, k_t, *zd = _in_proj(xp, row(g_pre_mix), w_main, wg, _tile(seq, 512), True, keep)
    zd = [z6.reshape(z6.shape[0], 1, seq, att_w)] + zd
    parts = [_dil_attn(z, rel_flat, d, att_heads) for z, d in zip(zd, DILATIONS)]
    m_out, ce_p, m_p = _mlstm_prompt(z6, k_t, gates_t, b_gate[0], g_mlstm[0], ml_heads)
    c_p, n_p = ce_p[:, :HEAD_DIM], ce_p[:, HEAD_DIM]
    x1, qx = _out_proj(tuple(p[0] for p in parts) + tuple(p[1] for p in parts), m_out, xp, w_out_b,
                       row(g_post_mix), row(g_pre_xatt), wq, _tile(seq, 512), att_heads)
    tm = _tile(seq, 512)
    x2 = _proj_post(_xattn(qx, mk_p, mv_p, tm, x_heads), x1, wo, row(g_post_xatt), tm)

    xs = x_sample[:, 0]
    z6s, kv32s, gates_s, _ = _in_proj(xs, row(g_pre_mix), w_main, wg, n_dec, False, n_dec)
    per_head = (n_dec, att_heads, HEAD_DIM)
    att_s = _sample_win_attn(z6s[0].astype(F32).reshape(per_head), kv32s[0].reshape(per_head),
                             kv32s[1].reshape(per_head), cache_win_k[0], cache_win_v[0], rel_bias, 4)
    mo_s, c_s, n_s, m_s = _sample_mlstm(z6s, gates_s, b_gate[0], g_mlstm[0], state_mlstm_C[0],
                                        state_mlstm_n[0], state_mlstm_m[0], ml_heads, 8)
    x1s, qxs = _out_proj((att_s.reshape(n_dec, att_w).astype(BF16),), mo_s.reshape(n_dec, ml_w).astype(BF16),
                         xs, w_out_b, row(g_post_mix), row(g_pre_xatt), wq, n_dec, att_heads)

    q3 = qxs.astype(F32).reshape(n_dec, x_heads, HEAD_DIM)
    y_prompt, o_mem = mlp(x2, tm, (q3, cache_mem_k[0], cache_mem_v[0]))
    x2s = _proj_post(o_mem.reshape(n_dec, x_w).astype(BF16), x1s, wo, row(g_post_xatt), n_dec)
    y_sample = mlp(x2s, n_dec)

    dk = MLSTM_DK
    return (y_prompt[None],
            y_sample[:, None],
            kv32[0].reshape(1, 1, keep, att_heads, HEAD_DIM),
            kv32[1].reshape(1, 1, keep, att_heads, HEAD_DIM),
            c_p.reshape(1, 1, ml_heads, dk, HEAD_DIM),
            n_p.reshape(1, 1, ml_heads, dk),
            m_p[:, 0].reshape(1, 1, ml_heads),
            mk_p.reshape(1, 1, n_mem, x_heads, HEAD_DIM),
            mv_p.reshape(1, 1, n_mem, x_heads, HEAD_DIM),
            kv32s[0].reshape(1, n_dec, 1, att_heads, HEAD_DIM),
            kv32s[1].reshape(1, n_dec, 1, att_heads, HEAD_DIM),
            c_s.reshape(1, n_dec, ml_heads, dk, HEAD_DIM),
            n_s.reshape(1, n_dec, ml_heads, dk),
            m_s.reshape(1, n_dec, ml_heads))
```

```python
import functools
import math

import numpy as np
import jax
import jax.numpy as jnp
from jax import lax
from jax.experimental import pallas as pl
from jax.experimental.pallas import tpu as pltpu

F32 = jnp.float32
BF16 = jnp.bfloat16

EPS = 1e-6
HEAD_DIM = 128
MLSTM_DK = 64
LANES = 128
DILATIONS = (1, 4, 16)
SUB_WINDOW = 128
ATT_BLOCK = 128
MLSTM_CHUNK = 128
REL_BUCKETS = 32
REL_MAX_DIST = 2048
NEG = -1e30
ATT_SCALE = HEAD_DIM ** -0.5
QK_SCALE = MLSTM_DK ** -0.5
VMEM_LIMIT = 56 * 2 ** 20


def _params(semantics):
    return pltpu.CompilerParams(dimension_semantics=semantics, vmem_limit_bytes=VMEM_LIMIT)


def _rms(xf, g):
    ms = jnp.mean(xf * xf, axis=-1, keepdims=True)
    return xf * lax.rsqrt(ms + EPS) * g


def _dot(a, b):
    return jnp.dot(a, b, preferred_element_type=F32)


def _dot_nt(a, b):
    return lax.dot_general(a, b, (((1,), (1,)), ((), ())), preferred_element_type=F32)


def _dot_tn(a, b):
    return lax.dot_general(a, b, (((0,), (0,)), ((), ())), preferred_element_type=F32)


def _log_sigmoid(x):
    return jnp.minimum(x, 0.0) - jnp.log(1.0 + jnp.exp(-jnp.abs(x)))


def _rel_bucket(dist):
    dist = np.asarray(dist)
    exact = REL_BUCKETS // 2
    far = exact + (np.log(np.maximum(dist, exact) / exact) / math.log(REL_MAX_DIST / exact)
                   * (REL_BUCKETS - exact)).astype(np.int32)
    return np.where(dist < exact, dist, np.minimum(far, REL_BUCKETS - 1)).astype(np.int32)


def _norm_matmul_kernel(x_ref, g_ref, w_ref, o_ref):
    xn = _rms(x_ref[...], g_ref[...]).astype(BF16)
    o_ref[...] = _dot(xn, w_ref[...])


def _norm_matmul(x, g, w):
    m, d = x.shape
    n = w.shape[1]
    return pl.pallas_call(
        _norm_matmul_kernel,
        out_shape=jax.ShapeDtypeStruct((m, n), F32),
        grid=(1,),
        in_specs=[pl.BlockSpec((m, d), lambda i: (0, 0)),
                  pl.BlockSpec((1, d), lambda i: (0, 0)),
                  pl.BlockSpec((d, n), lambda i: (0, 0))],
        out_specs=pl.BlockSpec((m, n), lambda i: (0, 0)),
        compiler_params=_params(("arbitrary",)),
        name="norm_matmul",
    )(x, g, w)


def _column_order(j):
    return (j % 2) * 3 + j // 2


def _in_proj_kernel(prompt, x_ref, g_ref, w_ref, wg_ref, *rest):
    if prompt:
        z_ref, kv_ref, gate_ref, gate_t_ref, kt_ref, zd_mid, zd_wide, xn_ref, slab, slab_mid = rest
    else:
        z_ref, kv_ref, gate_ref, gate_t_ref, xn_ref = rest
    j = pl.program_id(1)

    @pl.when(j == 0)
    def _():
        xn = _rms(x_ref[...], g_ref[...]).astype(BF16)
        xn_ref[...] = xn
        gates = _dot(xn, wg_ref[...])
        gate_ref[...] = gates
        gate_t_ref[...] = gates.T[:gate_t_ref.shape[0]]

    tm, tn = z_ref.shape
    n_steps = 6

    def project(c):
        acc = _dot(xn_ref[...], w_ref[:, c * tn:(c + 1) * tn])
        z_ref[...] = acc.astype(BF16)
        return acc

    if not prompt:
        for step in range(n_steps):
            @pl.when(j == step)
            def _(step=step):
                acc = project(step)
                if step in (1, 2):
                    kv_ref[...] = acc
        return

    mid = DILATIONS[1]
    ratio = DILATIONS[2] // mid

    def even_step(step):
        acc = project(_column_order(step))
        for h in range(tn // LANES):
            slab[h] = acc[:, h * LANES:(h + 1) * LANES]
        if step >= 2:
            kv_ref[...] = acc

    def odd_step(step):
        acc = project(_column_order(step))
        for h in range(tn // LANES):
            hl = slice(h * LANES, (h + 1) * LANES)
            for r in range(mid):
                rows = slab[h, pl.ds(r, tm // mid, stride=mid), :]
                zd_mid[r, :, hl] = rows.astype(BF16)
                slab_mid[h, r] = rows
            for r in range(mid):
                for q in range(ratio):
                    rows = slab_mid[h, r, pl.ds(q, tm // (mid * ratio), stride=ratio), :]
                    zd_wide[r + mid * q, :, hl] = rows.astype(BF16)
        if step == 1:
            kt_ref[...] = acc[:, tn // 2:].T.astype(BF16)

    for step in range(n_steps):
        pl.when(j == step)(functools.partial(odd_step if step % 2 else even_step, step))


def _in_proj(x, g, w_all, wg, tm, prompt, kv_rows):
    m, d = x.shape
    tn = 1024
    nblk = 6
    n_gate = wg.shape[1]
    wg_pad = jnp.zeros((d, LANES), BF16).at[:, :n_gate].set(wg)
    dils = DILATIONS[1:] if prompt else ()
    assert DILATIONS[2] % DILATIONS[1] == 0 and nblk == 6 and (m - kv_rows) % tm == 0
    const = lambda i, j: (0, 0)
    skip = (m - kv_rows) // tm
    if prompt:
        col = _column_order
        kv_c = lambda j: jnp.where(j < 4, 0, 1)
    else:
        col = lambda j: j
        kv_c = lambda j: jnp.clip(j - 1, 0, 1)
    kv_blk = lambda i, j: (jnp.where(i >= skip, kv_c(j), 0), jnp.maximum(i - skip, 0), 0)
    extra_out, extra_out_specs, extra_scratch = [], [], []
    if prompt:
        extra_out = [jax.ShapeDtypeStruct((tn // 2, m), BF16)] + [
            jax.ShapeDtypeStruct((3, dl, m // dl, tn), BF16) for dl in dils]
        extra_out_specs = [pl.BlockSpec((tn // 2, tm), lambda i, j: (0, i))] + [
            pl.BlockSpec((None, dl, tm // dl, tn), lambda i, j: (j // 2, 0, i, 0)) for dl in dils]
        extra_scratch = [pltpu.VMEM((tn // LANES, tm, LANES), F32),
                         pltpu.VMEM((tn // LANES, DILATIONS[1], tm // DILATIONS[1], LANES), F32)]
    return pl.pallas_call(
        functools.partial(_in_proj_kernel, prompt),
        out_shape=tuple([jax.ShapeDtypeStruct((nblk, m, tn), BF16),
                         jax.ShapeDtypeStruct((2, kv_rows, tn), F32),
                         jax.ShapeDtypeStruct((m, LANES), F32),
                         jax.ShapeDtypeStruct((n_gate, m), F32)] + extra_out),
        grid=(m // tm, nblk),
        in_specs=[pl.BlockSpec((tm, d), lambda i, j: (i, 0)),
                  pl.BlockSpec((1, d), const),
                  pl.BlockSpec(w_all.shape, const, pipeline_mode=pl.Buffered(1)),
                  pl.BlockSpec((d, LANES), const)],
        out_specs=tuple([pl.BlockSpec((None, tm, tn), lambda i, j: (col(j), i, 0)),
                         pl.BlockSpec((None, tm, tn), kv_blk),
                         pl.BlockSpec((tm, LANES), lambda i, j: (i, 0)),
                         pl.BlockSpec((n_gate, tm), lambda i, j: (0, i))] + extra_out_specs),
        scratch_shapes=[pltpu.VMEM((tm, d), BF16)] + extra_scratch,
        compiler_params=_params(("parallel", "arbitrary")),
        name="in_proj_dilated" if prompt else "in_proj",
    )(x, g, w_all, wg_pad)


def _dil_attn_kernel(heads, d, nq, rb_ref, bkt_ref, q_ref, kc_ref, kp_ref, vc_ref, vp_ref,
                     o_ref, lse_ref, bias, s_scr, p_scr):
    n = pl.program_id(0)
    r = pl.program_id(1)

    @pl.when((n == 0) & (r == 0))
    def _():
        bkt = bkt_ref[...]
        for h in range(heads):
            acc = jnp.full(bkt.shape, NEG, F32)
            for k in range(REL_BUCKETS):
                acc = jnp.where(bkt == k, rb_ref[k * heads + h], acc)
            bias[h] = acc

    blk = ATT_BLOCK
    in_prev = lax.broadcasted_iota(jnp.int32, (blk, 2 * blk), 1) < blk
    for u in range(nq):
        for h in range(heads):
            sl = slice(h * HEAD_DIM, (h + 1) * HEAD_DIM)
            qh = q_ref[u * blk:(u + 1) * blk, sl]
            if u == 0:
                keys = jnp.concatenate([kp_ref[:, sl], kc_ref[:blk, sl]], axis=0)
            else:
                keys = kc_ref[(u - 1) * blk:(u + 1) * blk, sl]
            s = _dot_nt(qh, keys) * ATT_SCALE + bias[h]
            if u == 0:
                s = jnp.where((n == 0) & in_prev, NEG, s)
            s_scr[u * heads + h] = s
    lane = lax.broadcasted_iota(jnp.int32, (blk, LANES), 1)
    m_tiles = []
    for u in range(nq):
        m_tile = jnp.zeros((blk, LANES), F32)
        for h in range(heads):
            s = s_scr[u * heads + h]
            m = jnp.max(s, axis=-1, keepdims=True)
            p_scr[u * heads + h] = jnp.exp(s - m).astype(BF16)
            m_tile = jnp.where(lane == h, m, m_tile)
        m_tiles.append(m_tile)
    ones = jnp.ones((2 * blk, HEAD_DIM), BF16)
    for u in range(nq):
        rows = pl.ds(r + u * blk * d, blk, stride=d)
        lse_tile = m_tiles[u]
        for h in range(heads):
            sl = slice(h * HEAD_DIM, (h + 1) * HEAD_DIM)
            if u == 0:
                vals = jnp.concatenate([vp_ref[:, sl], vc_ref[:blk, sl]], axis=0)
            else:
                vals = vc_ref[(u - 1) * blk:(u + 1) * blk, sl]
            o = _dot(p_scr[u * heads + h], jnp.concatenate([vals, ones], axis=1))
            l = o[:, HEAD_DIM:]
            o_ref[h, rows, :] = o[:, :HEAD_DIM] / l
            lse_tile = jnp.where(lane == h, lse_tile + jnp.log(l), lse_tile)
        lse_ref[rows, :] = lse_tile


DIL_ATTN_OUT_BLOCK_BYTES = 16 * 2 ** 20


def _dil_attn(zd, rel_flat, dilation, heads):
    d = dilation
    sub_len, w = zd.shape[2], zd.shape[3]
    s = sub_len * d
    blk = ATT_BLOCK
    nq = 2 if 2 * blk * d * w * 4 <= DIL_ATTN_OUT_BLOCK_BYTES else 1
    nb = sub_len // (nq * blk)
    qi = np.arange(blk)[:, None]
    ki = np.arange(2 * blk)[None, :]
    dist = blk + qi - ki
    bkt = np.where((dist >= 0) & (dist <= SUB_WINDOW), _rel_bucket(np.maximum(dist, 0) * d), -1).astype(np.int32)

    def cur(c):
        return pl.BlockSpec((None, None, nq * blk, w), lambda n, r: (c, r, n, 0))

    def prev(c):
        return pl.BlockSpec((None, None, blk, w), lambda n, r: (c, r, jnp.maximum(nq * n - 1, 0), 0))

    return pl.pallas_call(
        functools.partial(_dil_attn_kernel, heads, d, nq),
        out_shape=(jax.ShapeDtypeStruct((heads, s, HEAD_DIM), F32),
                   jax.ShapeDtypeStruct((s, LANES), F32)),
        grid=(nb, d),
        in_specs=[pl.BlockSpec(memory_space=pltpu.SMEM),
                  pl.BlockSpec((blk, 2 * blk), lambda n, r: (0, 0)),
                  cur(0), cur(1), prev(1), cur(2), prev(2)],
        out_specs=(pl.BlockSpec((heads, nq * blk * d, HEAD_DIM), lambda n, r: (0, n, 0)),
                   pl.BlockSpec((nq * blk * d, LANES), lambda n, r: (n, 0))),
        scratch_shapes=[pltpu.VMEM((heads, blk, 2 * blk), F32),
                        pltpu.VMEM((nq * heads, blk, 2 * blk), F32),
                        pltpu.VMEM((nq * heads, blk, 2 * blk), BF16)],
        compiler_params=_params(("arbitrary", "arbitrary")),
        name=f"dil_attn_d{d}",
    )(rel_flat, jnp.asarray(bkt), zd, zd, zd, zd, zd)


def _split_bf16(x):
    hi = x.astype(BF16)
    return hi, (x - hi.astype(F32)).astype(BF16)


def _mlstm_kernel(heads, n_side, qk_ref, kt_ref, v_ref, mo_ref, gt_ref, bgc_ref, gm_ref,
                  ce0_ref, m0_ref, *rest):
    step = pl.program_id(0)
    t = MLSTM_CHUNK
    qw = heads * MLSTM_DK
    dk = MLSTM_DK
    if n_side:
        side_in, rest = rest[:N_WIN_ATTN_IN], rest[N_WIN_ATTN_IN:]
        out_ref, ce_out, m_out, side_out, ce_s, m_s, sqk_s, tile_s, side_bias = rest
    else:
        out_ref, ce_out, m_out, ce_s, m_s, sqk_s, tile_s = rest

    @pl.when(step == 0)
    def _():
        ce_s[...] = ce0_ref[...]
        m_s[...] = m0_ref[...]
        if n_side:
            _sample_win_attn_init(side_in[0], side_in[1], side_bias)

    if n_side:
        _sample_win_attn_compute(n_side, side_in[0], *side_in[2:], side_out, side_bias)

    row = lax.broadcasted_iota(jnp.int32, (t, t), 0)
    col = lax.broadcasted_iota(jnp.int32, (t, t), 1)
    causal = col <= row
    eye = col == row
    upper = jnp.where(row <= col, 1.0, 0.0).astype(BF16)
    lane = lax.broadcasted_iota(jnp.int32, (1, LANES), 1)
    lane_h = lax.broadcasted_iota(jnp.int32, (heads, t), 1)
    blk_r = lax.broadcasted_iota(jnp.int32, (2 * t, 2 * t), 0) // t
    blk_c = lax.broadcasted_iota(jnp.int32, (2 * t, 2 * t), 1) // t
    ones_bd = jnp.where(blk_r == blk_c, 1.0, 0.0).astype(BF16)
    ones = jnp.ones((t, HEAD_DIM), BF16)

    gt = gt_ref[...] + bgc_ref[...]
    hi, lo = _split_bf16(_log_sigmoid(gt[heads:]))
    b2 = _dot(jnp.concatenate([hi, lo], axis=0), upper)
    b = b2[:heads] + b2[heads:]
    c = gt[:heads] - b
    m_prev = m_s[...]
    cmax = c
    for shift in (1, 2, 4, 8, 16, 32, 64):
        cmax = jnp.maximum(cmax, jnp.where(lane_h >= shift, pltpu.roll(cmax, shift, axis=1), NEG))
    a = jnp.maximum(cmax, m_prev)
    m_t = b + a
    a_last = a[:, t - 1:t]
    b_last = b[:, t - 1:t]
    m_new = b_last + a_last
    w_k = jnp.exp(c - a_last)
    decay = jnp.exp(b_last + m_prev - m_new)
    a_hi, a_lo = _split_bf16(a)
    mt_hi, mt_lo = _split_bf16(m_t)

    def diag(x_row):
        return jnp.where(eye, x_row.astype(F32), 0.0).astype(BF16)

    qms, dens = [], []
    for h in range(heads):
        pair, half = divmod(h, 2)
        hs = slice(h, h + 1)
        tile = (_dot(jnp.concatenate([diag(a_hi[hs]), diag(mt_hi[hs])], axis=1), ones_bd)
                + _dot(jnp.concatenate([diag(a_lo[hs]), diag(mt_lo[hs])], axis=1), ones_bd))
        tile_s[h] = tile
        w_intra = jnp.exp(jnp.where(causal, c[hs] - tile[:, :t], NEG))
        mine = (lane >= half * dk) & (lane < (half + 1) * dk)
        qm = jnp.where(mine, qk_ref[:, pair * LANES:(pair + 1) * LANES], 0.0) * QK_SCALE
        km = jnp.where(mine, qk_ref[:, qw + pair * LANES:qw + (pair + 1) * LANES], 0.0)
        sqk = _dot_nt(qm, km) * w_intra
        sqk_s[h] = sqk.astype(BF16)
        dens.append(jnp.sum(sqk, axis=-1, keepdims=True))
        qms.append(qm)

    for h in range(heads):
        pair = h // 2
        sl = slice(h * HEAD_DIM, (h + 1) * HEAD_DIM)
        tile = tile_s[h]
        w_prev = jnp.exp(m_prev[h:h + 1, :] - tile[:, :t])
        q_ce = _dot(qms[h], ce_s[pair * LANES:(pair + 1) * LANES, :].astype(BF16))
        num = _dot(sqk_s[h], v_ref[:, sl]) + w_prev * q_ce[:, :HEAD_DIM]
        den = dens[h] + w_prev * q_ce[:, HEAD_DIM:]
        hh = num / jnp.maximum(jnp.abs(den), jnp.exp(-tile[:, t:]))
        hn = hh * lax.rsqrt(jnp.mean(hh * hh, axis=-1, keepdims=True) + EPS)
        gate = jax.nn.sigmoid(mo_ref[:, sl].astype(F32))
        out_ref[:, sl] = (hn * gm_ref[:, sl] * gate).astype(out_ref.dtype)

    for h in range(heads):
        hs = slice(h * dk, (h + 1) * dk)
        kw = (kt_ref[hs, :].astype(F32) * w_k[h:h + 1, :]).astype(BF16)
        v_ext = jnp.concatenate([v_ref[:, h * HEAD_DIM:(h + 1) * HEAD_DIM], ones], axis=1)
        dec = jnp.concatenate([decay[h:h + 1, :], decay[h:h + 1, :]], axis=1)
        ce_s[hs, :] = dec * ce_s[hs, :] + _dot(kw, v_ext)
    m_s[...] = jnp.broadcast_to(m_new, (heads, LANES))

    @pl.when(step == pl.num_programs(0) - 1)
    def _():
        ce_out[...] = ce_s[...]
        m_out[...] = m_s[...]


def _mlstm_prompt(z6, k_t, gates_t, b_gate, g_mlstm, heads, side=None):
    _, s, w = z6.shape
    t = MLSTM_CHUNK
    qw = heads * MLSTM_DK
    n_steps = s // t
    ce0 = jnp.zeros((qw, 2 * HEAD_DIM), F32)
    m0 = jnp.zeros((heads, LANES), F32)
    const = lambda c: (0, 0)
    n_side, side_ops, side_specs, side_out, side_out_spec, side_scratch = 0, [], [], [], [], []
    if side is not None:
        n_side = side[0].shape[0] // n_steps
        assert n_side * n_steps == side[0].shape[0]
        side_ops, side_specs, out_shape, out_spec, side_scratch = _sample_win_attn_operands(*side, n_side)
        side_out, side_out_spec = [out_shape], [out_spec]
    return pl.pallas_call(
        functools.partial(_mlstm_kernel, heads, n_side),
        out_shape=tuple([jax.ShapeDtypeStruct((s, w), BF16),
                         jax.ShapeDtypeStruct((qw, 2 * HEAD_DIM), F32),
                         jax.ShapeDtypeStruct((heads, LANES), F32)] + side_out),
        grid=(n_steps,),
        in_specs=[pl.BlockSpec((None, t, w), lambda c: (3, c, 0)),
                  pl.BlockSpec((qw, t), lambda c: (0, c)),
                  pl.BlockSpec((None, t, w), lambda c: (4, c, 0)),
                  pl.BlockSpec((None, t, w), lambda c: (5, c, 0)),
                  pl.BlockSpec((2 * heads, t), lambda c: (0, c)),
                  pl.BlockSpec((2 * heads, 1), const),
                  pl.BlockSpec((1, w), const),
                  pl.BlockSpec((qw, 2 * HEAD_DIM), const),
                  pl.BlockSpec((heads, LANES), const)] + side_specs,
        out_specs=tuple([pl.BlockSpec((t, w), lambda c: (c, 0)),
                         pl.BlockSpec((qw, 2 * HEAD_DIM), const),
                         pl.BlockSpec((heads, LANES), const)] + side_out_spec),
        scratch_shapes=[pltpu.VMEM((qw, 2 * HEAD_DIM), F32),
                        pltpu.VMEM((heads, LANES), F32),
                        pltpu.VMEM((heads, t, t), BF16),
                        pltpu.VMEM((heads, t, 2 * t), F32)] + side_scratch,
        compiler_params=_params(("arbitrary",)),
        name="mlstm_prompt",
    )(z6, k_t, z6, z6, gates_t, b_gate.reshape(2 * heads, 1), g_mlstm.reshape(1, w), ce0, m0, *side_ops)


def _out_proj_kernel(merge, heads, *refs):
    if merge:
        (o1, o2, o3, l1, l2, l3, mo_ref, x_ref, wa_ref, wm_ref, gpost_ref, gpre_ref, wq_ref,
         x1_ref, qx_ref, att_s) = refs
        la, lb, lc = l1[...], l2[...], l3[...]
        mx = jnp.maximum(jnp.maximum(la, lb), lc)
        ea, eb, ec = jnp.exp(la - mx), jnp.exp(lb - mx), jnp.exp(lc - mx)
        tot = ea + eb + ec
        wa_, wb_, wc_ = ea / tot, eb / tot, ec / tot
        for h in range(heads):
            sl = slice(h * HEAD_DIM, (h + 1) * HEAD_DIM)
            mix = wa_[:, h:h + 1] * o1[h] + wb_[:, h:h + 1] * o2[h] + wc_[:, h:h + 1] * o3[h]
            att_s[:, sl] = mix.astype(BF16)
        att = att_s[...]
    else:
        att_ref, mo_ref, x_ref, wa_ref, wm_ref, gpost_ref, gpre_ref, wq_ref, x1_ref, qx_ref = refs
        att = att_ref[...]
    y = _dot(att, wa_ref[...]) + _dot(mo_ref[...], wm_ref[...])
    x1 = x_ref[...] + _rms(y, gpost_ref[...])
    x1_ref[...] = x1
    u = _rms(x1, gpre_ref[...]).astype(BF16)
    qx_ref[...] = _dot(u, wq_ref[...]).astype(qx_ref.dtype)


def _out_proj(att_parts, m_out, x, w_out, g_post, g_pre, w_xq, tm, heads):
    m, d = x.shape
    wm = m_out.shape[1]
    wa = w_out.shape[0] - wm
    assert wa == wm
    nq = w_xq.shape[1]
    merge = len(att_parts) == 6
    row = lambda i: (i, 0)
    const = lambda i: (0, 0)
    if merge:
        part_specs = ([pl.BlockSpec((heads, tm, HEAD_DIM), lambda i: (0, i, 0))] * 3
                      + [pl.BlockSpec((tm, LANES), row)] * 3)
    else:
        part_specs = [pl.BlockSpec((tm, wa), row)]
    return pl.pallas_call(
        functools.partial(_out_proj_kernel, merge, heads),
        out_shape=(jax.ShapeDtypeStruct((m, d), F32), jax.ShapeDtypeStruct((m, nq), BF16)),
        grid=(m // tm,),
        in_specs=part_specs + [pl.BlockSpec((tm, wm), row),
                               pl.BlockSpec((tm, d), row),
                               pl.BlockSpec((wa, d), const, pipeline_mode=pl.Buffered(1)),
                               pl.BlockSpec((wm, d), lambda i: (1, 0), pipeline_mode=pl.Buffered(1)),
                               pl.BlockSpec((1, d), const),
                               pl.BlockSpec((1, d), const),
                               pl.BlockSpec((d, nq), const, pipeline_mode=pl.Buffered(1))],
        out_specs=(pl.BlockSpec((tm, d), row), pl.BlockSpec((tm, nq), row)),
        scratch_shapes=[pltpu.VMEM((tm, wa), BF16)] if merge else [],
        compiler_params=_params(("parallel",)),
        name="out_proj_merge" if merge else "out_proj",
    )(*att_parts, m_out, x, w_out, w_out, g_post, g_pre, w_xq)


def _xattn_kernel(heads, q_ref, mk_ref, mv_ref, o_ref):
    for h in range(heads):
        sl = slice(h * HEAD_DIM, (h + 1) * HEAD_DIM)
        s = _dot_nt(q_ref[:, sl], mk_ref[:, sl].astype(BF16)) * ATT_SCALE
        m = jnp.max(s, axis=-1, keepdims=True)
        p = jnp.exp(s - m)
        l = jnp.sum(p, axis=-1, keepdims=True)
        o = _dot(p.astype(BF16), mv_ref[:, sl].astype(BF16))
        o_ref[:, sl] = (o / l).astype(o_ref.dtype)


def _xattn(qx, mem_k, mem_v, tm, heads):
    m, w = qx.shape
    n_mem = mem_k.shape[0]
    return pl.pallas_call(
        functools.partial(_xattn_kernel, heads),
        out_shape=jax.ShapeDtypeStruct((m, w), BF16),
        grid=(m // tm,),
        in_specs=[pl.BlockSpec((tm, w), lambda i: (i, 0)),
                  pl.BlockSpec((n_mem, w), lambda i: (0, 0)),
                  pl.BlockSpec((n_mem, w), lambda i: (0, 0))],
        out_specs=pl.BlockSpec((tm, w), lambda i: (i, 0)),
        compiler_params=_params(("parallel",)),
        name="xattn_prompt",
    )(qx, mem_k, mem_v)


def _proj_post_kernel(a_ref, x_ref, w_ref, g_ref, y_ref):
    y_ref[...] = x_ref[...] + _rms(_dot(a_ref[...], w_ref[...]), g_ref[...])


def _proj_post(a, x, w, g, tm):
    m, d = x.shape
    k = a.shape[1]
    return pl.pallas_call(
        _proj_post_kernel,
        out_shape=jax.ShapeDtypeStruct((m, d), F32),
        grid=(m // tm,),
        in_specs=[pl.BlockSpec((tm, k), lambda i: (i, 0)),
                  pl.BlockSpec((tm, d), lambda i: (i, 0)),
                  pl.BlockSpec((k, d), lambda i: (0, 0)),
                  pl.BlockSpec((1, d), lambda i: (0, 0))],
        out_specs=pl.BlockSpec((tm, d), lambda i: (i, 0)),
        compiler_params=_params(("parallel",)),
        name="proj_post",
    )(a, x, w, g)


def _mlp_kernel(n_side, x_ref, gpre_ref, wu_ref, wd_ref, gpost_ref, *rest):
    if n_side:
        sq_ref, sk_ref, sv_ref, y_ref, so_ref, xn_s, acc_s = rest
    else:
        y_ref, xn_s, acc_s = rest
    j = pl.program_id(1)

    @pl.when(j == 0)
    def _():
        xn_s[...] = _rms(x_ref[...], gpre_ref[...]).astype(BF16)
        acc_s[...] = jnp.zeros(acc_s.shape, F32)

    hidden = jnp.square(jnp.maximum(_dot(xn_s[...], wu_ref[...]), 0.0))
    acc_s[...] += _dot(hidden.astype(BF16), wd_ref[...])
    for b in range(n_side):
        so_ref[b] = _single_query_attention(sq_ref[b], [(sk_ref[b], sv_ref[b], None)], None)

    @pl.when(j == pl.num_programs(1) - 1)
    def _():
        y_ref[...] = x_ref[...] + _rms(acc_s[...], gpost_ref[...])


def _mlp(x, g_pre, w_up, w_down, g_post, tm, tf, side=None):
    m, d = x.shape
    ff = w_up.shape[1]
    steps_j = ff // tf
    n_steps = (m // tm) * steps_j
    in_specs = [pl.BlockSpec((tm, d), lambda i, j: (i, 0)),
                pl.BlockSpec((1, d), lambda i, j: (0, 0)),
                pl.BlockSpec((d, tf), lambda i, j: (0, j)),
                pl.BlockSpec((tf, d), lambda i, j: (j, 0)),
                pl.BlockSpec((1, d), lambda i, j: (0, 0))]
    out_shape = [jax.ShapeDtypeStruct((m, d), F32)]
    out_specs = [pl.BlockSpec((tm, d), lambda i, j: (i, 0))]
    n_side = 0
    if side is not None:
        sq, sk, sv = side
        nb, heads, hd = sq.shape
        n_side = nb // n_steps
        assert n_side * n_steps == nb
        tok = pl.BlockSpec((n_side, heads, hd), lambda i, j: (i * steps_j + j, 0, 0))
        mem = pl.BlockSpec((n_side,) + sk.shape[1:], lambda i, j: (i * steps_j + j, 0, 0, 0))
        in_specs += [tok, mem, mem]
        out_shape.append(jax.ShapeDtypeStruct((nb, heads, hd), F32))
        out_specs.append(tok)
    out = pl.pallas_call(
        functools.partial(_mlp_kernel, n_side),
        out_shape=tuple(out_shape),
        grid=(m // tm, steps_j),
        in_specs=in_specs,
        out_specs=tuple(out_specs),
        scratch_shapes=[pltpu.VMEM((tm, d), BF16), pltpu.VMEM((tm, d), F32)],
        compiler_params=_params(("parallel", "arbitrary")),
        name="mlp_side" if side is not None else "mlp",
    )(x, g_pre, w_up, w_down, g_post, *(side or ()))
    return out if side is not None else out[0]


def _single_query_attention(q, groups, new):
    q = q * ATT_SCALE
    scores = []
    for k3, _, b3 in groups:
        s = jnp.sum(k3 * q[None], axis=-1, keepdims=True)
        scores.append(s if b3 is None else s + b3)
    mx = jnp.max(scores[0], axis=0)
    for s in scores[1:]:
        mx = jnp.maximum(mx, jnp.max(s, axis=0))
    if new is not None:
        k_new, v_new, b_new, count = new
        s_new = jnp.sum(k_new * q, axis=-1, keepdims=True) + b_new
        mx = jnp.maximum(mx, s_new)
    den = jnp.zeros(mx.shape, F32)
    acc = jnp.zeros(q.shape, F32)
    for s, (_, v3, _) in zip(scores, groups):
        p = jnp.exp(s - mx[None])
        den = den + jnp.sum(p, axis=0)
        acc = acc + jnp.sum(p * v3, axis=0)
    if new is not None:
        p_new = jnp.exp(s_new - mx) * count
        den = den + p_new
        acc = acc + p_new * v_new
    return acc / den


N_WIN_ATTN_IN = 5 + 2 * len(DILATIONS)


def _sample_win_attn_init(rel_ref, bkt_ref, bias_s):
    for p in range(len(DILATIONS)):
        bkt = bkt_ref[p]
        acc = jnp.zeros(bkt.shape, F32)
        for k in range(REL_BUCKETS):
            acc = jnp.where(bkt == k, rel_ref[k][None], acc)
        bias_s[p] = acc


def _sample_win_attn_compute(tb, rel_ref, q_ref, kn_ref, vn_ref, k1, k4, k16, v1, v4, v16, o_ref, bias_s):
    for b in range(tb):
        groups = [(k[b], v[b], bias_s[p]) for p, (k, v) in enumerate(((k1, v1), (k4, v4), (k16, v16)))]
        new = (kn_ref[b], vn_ref[b], rel_ref[0], float(len(DILATIONS)))
        o_ref[b] = _single_query_attention(q_ref[b], groups, new)


def _sample_win_attn_operands(q, k_new, v_new, cache_k, cache_v, rel_bias, tb):
    nb, heads, hd = q.shape
    buf = cache_k.shape[1]
    r = SUB_WINDOW
    assert buf == r * DILATIONS[-1], "window buffer must hold exactly the widest dilated pattern"
    j = r - np.arange(r)
    bkt = np.stack([_rel_bucket(j * d) for d in DILATIONS]).astype(np.int32)
    bkt = np.broadcast_to(bkt[:, :, None, None], (len(DILATIONS), r, heads, 1))
    views, specs = [], []
    for cache in (cache_k, cache_v):
        for d in DILATIONS:
            views.append(cache.reshape(nb, buf // d, d, heads, hd))
            specs.append(pl.BlockSpec((tb, r, None, heads, hd), functools.partial(
                lambda i, blk: (i, blk, 0, 0, 0), blk=buf // d // r - 1)))
    tok = pl.BlockSpec((tb, heads, hd), lambda i: (i, 0, 0))
    operands = [rel_bias.reshape(REL_BUCKETS, heads, 1), jnp.asarray(bkt), q, k_new, v_new] + views
    in_specs = [pl.BlockSpec((REL_BUCKETS, heads, 1), lambda i: (0, 0, 0)),
                pl.BlockSpec((len(DILATIONS), r, heads, 1), lambda i: (0, 0, 0, 0)),
                tok, tok, tok] + specs
    scratch = [pltpu.VMEM((len(DILATIONS), r, heads, 1), F32)]
    return operands, in_specs, jax.ShapeDtypeStruct((nb, heads, hd), F32), tok, scratch


def _sample_mlstm_kernel(heads, tb, q_ref, k_ref, v_ref, mo_ref, li_ref, lf_ref, bi_ref, bf_ref,
                         gm_ref, c_ref, n_ref, m_ref, out_ref, c_out, n_out, m_out):
    qw = heads * MLSTM_DK
    hrow = lax.broadcasted_iota(jnp.int32, (LANES, qw), 0)
    hcol = lax.broadcasted_iota(jnp.int32, (LANES, qw), 1) // MLSTM_DK
    diag = hrow == hcol
    rows8 = slice(0, heads)
    for b in range(tb):
        q_bd = jnp.where(diag, q_ref[b:b + 1, :], 0.0) * QK_SCALE
        k_bd = jnp.where(diag, k_ref[b:b + 1, :], 0.0)
        c_b = c_ref[b]
        n_row = n_ref[b:b + 1, :]
        li = li_ref[b] + bi_ref[...]
        lf = _log_sigmoid(lf_ref[b] + bf_ref[...])
        m_prev = m_ref[b]
        qk = jnp.sum(q_bd * k_ref[b:b + 1, :], axis=-1, keepdims=True)[rows8]
        qn = jnp.sum(q_bd * n_row, axis=-1, keepdims=True)[rows8]
        q_c = _dot(q_bd.astype(BF16), c_b.astype(BF16))[rows8]
        inter = lf + m_prev
        m_t = jnp.maximum(inter, li)
        w_intra = jnp.exp(li - m_t)
        w_prev = jnp.exp(inter - m_t)
        sqk = qk * w_intra
        v_b = v_ref[b]
        num = sqk * v_b + w_prev * q_c
        den = sqk + w_prev * qn
        hh = num / jnp.maximum(jnp.abs(den), jnp.exp(-m_t))
        hn = hh * lax.rsqrt(jnp.mean(hh * hh, axis=-1, keepdims=True) + EPS)
        out_ref[b] = hn * gm_ref[...] * jax.nn.sigmoid(mo_ref[b])
        wv = jnp.concatenate([w_intra * v_b, jnp.zeros((LANES - heads, HEAD_DIM), F32)], axis=0)
        d_c = _dot_tn(k_bd.astype(BF16), wv.astype(BF16))
        for h in range(heads):
            hs = slice(h * MLSTM_DK, (h + 1) * MLSTM_DK)
            c_out[b, hs, :] = w_prev[h:h + 1, :] * c_b[hs, :] + d_c[hs, :]
        k_w = jnp.sum(k_bd[rows8] * w_intra, axis=0, keepdims=True)
        dec_row = jnp.sum(jnp.where(diag[rows8], w_prev, 0.0), axis=0, keepdims=True)
        n_out[b:b + 1, :] = dec_row * n_row + k_w
        m_out[b] = m_t


def _sample_mlstm(z6, gates, b_gate, g_mlstm, state_c, state_n, state_m, heads, tb):
    _, nb, w = z6.shape
    qw = heads * MLSTM_DK
    q = z6[3, :, :qw].astype(F32)
    k = z6[3, :, qw:].astype(F32)
    v = z6[4].astype(F32).reshape(nb, heads, HEAD_DIM)
    mo = z6[5].astype(F32).reshape(nb, heads, HEAD_DIM)
    li = gates[:, :heads].reshape(nb, heads, 1)
    lf = gates[:, heads:2 * heads].reshape(nb, heads, 1)
    row = lambda i: (i, 0)
    blk3 = lambda i: (i, 0, 0)
    const = lambda i: (0, 0)
    return pl.pallas_call(
        functools.partial(_sample_mlstm_kernel, heads, tb),
        out_shape=(jax.ShapeDtypeStruct((nb, heads, HEAD_DIM), F32),
                   jax.ShapeDtypeStruct((nb, qw, HEAD_DIM), F32),
                   jax.ShapeDtypeStruct((nb, qw), F32),
                   jax.ShapeDtypeStruct((nb, heads, 1), F32)),
        grid=(nb // tb,),
        in_specs=[pl.BlockSpec((tb, qw), row),
                  pl.BlockSpec((tb, qw), row),
                  pl.BlockSpec((tb, heads, HEAD_DIM), blk3),
                  pl.BlockSpec((tb, heads, HEAD_DIM), blk3),
                  pl.BlockSpec((tb, heads, 1), blk3),
                  pl.BlockSpec((tb, heads, 1), blk3),
                  pl.BlockSpec((heads, 1), const),
                  pl.BlockSpec((heads, 1), const),
                  pl.BlockSpec((heads, HEAD_DIM), const),
                  pl.BlockSpec((tb, qw, HEAD_DIM), blk3),
                  pl.BlockSpec((tb, qw), row),
                  pl.BlockSpec((tb, heads, 1), blk3)],
        out_specs=(pl.BlockSpec((tb, heads, HEAD_DIM), blk3),
                   pl.BlockSpec((tb, qw, HEAD_DIM), blk3),
                   pl.BlockSpec((tb, qw), row),
                   pl.BlockSpec((tb, heads, 1), blk3)),
        compiler_params=_params(("parallel",)),
        name="sample_mlstm",
    )(q, k, v, mo, li, lf, b_gate[:heads].reshape(heads, 1), b_gate[heads:].reshape(heads, 1),
      g_mlstm.reshape(heads, HEAD_DIM), state_c.reshape(nb, qw, HEAD_DIM), state_n.reshape(nb, qw),
      state_m.reshape(nb, heads, 1))


def _tile(m, cap):
    return min(m, cap)


def kernel(x_prompt, x_sample, mem_prompt, cache_win_k, cache_win_v, state_mlstm_C, state_mlstm_n,
           state_mlstm_m, cache_mem_k, cache_mem_v, rel_bias, g_pre_mix, g_post_mix, w_in, b_gate,
           g_mlstm, w_out, g_pre_xatt, g_post_xatt, g_mem, w_xq, w_xkv, w_xo, g_pre_mlp, g_post_mlp,
           w_up, w_down):
    depth = w_in.shape[0]
    assert depth == 1 and x_prompt.shape[0] == 1 and x_sample.shape[1] == 1
    _, seq, d_model = x_prompt.shape
    n_dec = x_sample.shape[0]
    att_heads = cache_win_k.shape[3]
    ml_heads = state_mlstm_C.shape[2]
    x_heads = cache_mem_k.shape[3]
    att_w = att_heads * HEAD_DIM
    ml_w = ml_heads * HEAD_DIM
    x_w = x_heads * HEAD_DIM
    n_mem = mem_prompt.shape[1]
    keep = cache_win_k.shape[2]
    n_main = 3 * att_w + 2 * ml_heads * MLSTM_DK + 2 * ml_w
    assert att_w == ml_w == 1024 and n_main % 1024 == 0 and seq >= keep

    def row(g):
        return g[0].reshape(1, -1)

    assert n_main == 6 * 1024
    w_main = w_in[0].astype(BF16)
    wg = w_in[0, :, n_main:].astype(BF16)
    w_out_b = w_out[0].astype(BF16)
    wq = w_xq[0].astype(BF16)
    wkv = w_xkv[0].astype(BF16)
    wo = w_xo[0].astype(BF16)
    wu = w_up[0].astype(BF16)
    wd = w_down[0].astype(BF16)
    rel_flat = rel_bias.reshape(-1)

    def mlp(x2, tm, side=None):
        return _mlp(x2, row(g_pre_mlp), wu, wd, row(g_post_mlp), tm, 1024, side)

    xs = x_sample[:, 0]
    z6s, kv32s, gates_s, _ = _in_proj(xs, row(g_pre_mix), w_main, wg, n_dec, False, n_dec)
    per_head = (n_dec, att_heads, HEAD_DIM)
    win_attn_args = (z6s[0].astype(F32).reshape(per_head), kv32s[0].reshape(per_head),
                     kv32s[1].reshape(per_head), cache_win_k[0], cache_win_v[0], rel_bias)

    xp = x_prompt[0]
    mem_kv = _norm_matmul(mem_prompt[0], row(g_mem), wkv)
    mk_p, mv_p = mem_kv[:, :x_w], mem_kv[:, x_w:]
    z6, kv32, _, gates_t, k_t, *zd = _in_proj(xp, row(g_pre_mix), w_main, wg, _tile(seq, 512), True, keep)
    zd = [z6.reshape(z6.shape[0], 1, seq, att_w)] + zd
    parts = [_dil_attn(z, rel_flat, d, att_heads) for z, d in zip(zd, DILATIONS)]
    m_out, ce_p, m_p, att_s = _mlstm_prompt(z6, k_t, gates_t, b_gate[0], g_mlstm[0], ml_heads, win_attn_args)
    c_p, n_p = ce_p[:, :HEAD_DIM], ce_p[:, HEAD_DIM]
    x1, qx = _out_proj(tuple(p[0] for p in parts) + tuple(p[1] for p in parts), m_out, xp, w_out_b,
                       row(g_post_mix), row(g_pre_xatt), wq, _tile(seq, 512), att_heads)
    tm = _tile(seq, 512)
    x2 = _proj_post(_xattn(qx, mk_p, mv_p, tm, x_heads), x1, wo, row(g_post_xatt), tm)

    mo_s, c_s, n_s, m_s = _sample_mlstm(z6s, gates_s, b_gate[0], g_mlstm[0], state_mlstm_C[0],
                                        state_mlstm_n[0], state_mlstm_m[0], ml_heads, 8)
    x1s, qxs = _out_proj((att_s.reshape(n_dec, att_w).astype(BF16),), mo_s.reshape(n_dec, ml_w).astype(BF16),
                         xs, w_out_b, row(g_post_mix), row(g_pre_xatt), wq, n_dec, att_heads)

    q3 = qxs.astype(F32).reshape(n_dec, x_heads, HEAD_DIM)
    y_prompt, o_mem = mlp(x2, tm, (q3, cache_mem_k[0], cache_mem_v[0]))
    x2s = _proj_post(o_mem.reshape(n_dec, x_w).astype(BF16), x1s, wo, row(g_post_xatt), n_dec)
    y_sample = mlp(x2s, n_dec)

    dk = MLSTM_DK
    return (y_prompt[None],
            y_sample[:, None],
            kv32[0].reshape(1, 1, keep, att_heads, HEAD_DIM),
            kv32[1].reshape(1, 1, keep, att_heads, HEAD_DIM),
            c_p.reshape(1, 1, ml_heads, dk, HEAD_DIM),
            n_p.reshape(1, 1, ml_heads, dk),
            m_p[:, 0].reshape(1, 1, ml_heads),
            mk_p.reshape(1, 1, n_mem, x_heads, HEAD_DIM),
            mv_p.reshape(1, 1, n_mem, x_heads, HEAD_DIM),
            kv32s[0].reshape(1, n_dec, 1, att_heads, HEAD_DIM),
            kv32s[1].reshape(1, n_dec, 1, att_heads, HEAD_DIM),
            c_s.reshape(1, n_dec, ml_heads, dk, HEAD_DIM),
            n_s.reshape(1, n_dec, ml_heads, dk),
            m_s.reshape(1, n_dec, ml_heads))
```

```python
import functools
import math

import numpy as np
import jax
import jax.numpy as jnp
from jax import lax
from jax.experimental import pallas as pl
from jax.experimental.pallas import tpu as pltpu

F32 = jnp.float32
BF16 = jnp.bfloat16

EPS = 1e-6
HEAD_DIM = 128
MLSTM_DK = 64
LANES = 128
DILATIONS = (1, 4, 16)
SUB_WINDOW = 128
ATT_BLOCK = 128
MLSTM_CHUNK = 128
REL_BUCKETS = 32
REL_MAX_DIST = 2048
NEG = -1e30
ATT_SCALE = HEAD_DIM ** -0.5
QK_SCALE = MLSTM_DK ** -0.5
VMEM_LIMIT = 56 * 2 ** 20


def _params(semantics):
    return pltpu.CompilerParams(dimension_semantics=semantics, vmem_limit_bytes=VMEM_LIMIT)


def _rms(xf, g):
    ms = jnp.mean(xf * xf, axis=-1, keepdims=True)
    return xf * lax.rsqrt(ms + EPS) * g


def _dot(a, b):
    return jnp.dot(a, b, preferred_element_type=F32)


def _dot_nt(a, b):
    return lax.dot_general(a, b, (((1,), (1,)), ((), ())), preferred_element_type=F32)


def _dot_tn(a, b):
    return lax.dot_general(a, b, (((0,), (0,)), ((), ())), preferred_element_type=F32)


def _log_sigmoid(x):
    return jnp.minimum(x, 0.0) - jnp.log(1.0 + jnp.exp(-jnp.abs(x)))


def _rel_bucket(dist):
    dist = np.asarray(dist)
    exact = REL_BUCKETS // 2
    far = exact + (np.log(np.maximum(dist, exact) / exact) / math.log(REL_MAX_DIST / exact)
                   * (REL_BUCKETS - exact)).astype(np.int32)
    return np.where(dist < exact, dist, np.minimum(far, REL_BUCKETS - 1)).astype(np.int32)


def _norm_matmul_kernel(x_ref, g_ref, w_ref, o_ref):
    xn = _rms(x_ref[...], g_ref[...]).astype(BF16)
    o_ref[...] = _dot(xn, w_ref[...])


def _norm_matmul(x, g, w):
    m, d = x.shape
    n = w.shape[1]
    return pl.pallas_call(
        _norm_matmul_kernel,
        out_shape=jax.ShapeDtypeStruct((m, n), F32),
        grid=(1,),
        in_specs=[pl.BlockSpec((m, d), lambda i: (0, 0)),
                  pl.BlockSpec((1, d), lambda i: (0, 0)),
                  pl.BlockSpec((d, n), lambda i: (0, 0))],
        out_specs=pl.BlockSpec((m, n), lambda i: (0, 0)),
        compiler_params=_params(("arbitrary",)),
        name="norm_matmul",
    )(x, g, w)


def _column_order(j):
    return (j % 2) * 3 + j // 2


def _in_proj_kernel(prompt, x_ref, g_ref, w_ref, wg_ref, *rest):
    if prompt:
        z_ref, kv_ref, gate_ref, gate_t_ref, kt_ref, zd_mid, zd_wide, xn_ref, slab, slab_mid = rest
    else:
        z_ref, kv_ref, gate_ref, gate_t_ref, xn_ref = rest
    j = pl.program_id(1)

    @pl.when(j == 0)
    def _():
        xn = _rms(x_ref[...], g_ref[...]).astype(BF16)
        xn_ref[...] = xn
        gates = _dot(xn, wg_ref[...])
        gate_ref[...] = gates
        gate_t_ref[...] = gates.T[:gate_t_ref.shape[0]]

    tm, tn = z_ref.shape
    n_steps = 6

    def project(c):
        acc = _dot(xn_ref[...], w_ref[:, c * tn:(c + 1) * tn])
        z_ref[...] = acc.astype(BF16)
        return acc

    if not prompt:
        for step in range(n_steps):
            @pl.when(j == step)
            def _(step=step):
                acc = project(step)
                if step in (1, 2):
                    kv_ref[...] = acc
        return

    mid = DILATIONS[1]
    ratio = DILATIONS[2] // mid

    def even_step(step):
        acc = project(_column_order(step))
        for h in range(tn // LANES):
            slab[h] = acc[:, h * LANES:(h + 1) * LANES]
        if step >= 2:
            kv_ref[...] = acc

    def odd_step(step):
        acc = project(_column_order(step))
        for h in range(tn // LANES):
            hl = slice(h * LANES, (h + 1) * LANES)
            for r in range(mid):
                rows = slab[h, pl.ds(r, tm // mid, stride=mid), :]
                zd_mid[r, :, hl] = rows.astype(BF16)
                slab_mid[h, r] = rows
            for r in range(mid):
                for q in range(ratio):
                    rows = slab_mid[h, r, pl.ds(q, tm // (mid * ratio), stride=ratio), :]
                    zd_wide[r + mid * q, :, hl] = rows.astype(BF16)
        if step == 1:
            kt_ref[...] = acc[:, tn // 2:].T.astype(BF16)

    for step in range(n_steps):
        pl.when(j == step)(functools.partial(odd_step if step % 2 else even_step, step))


def _in_proj(x, g, w_all, wg, tm, prompt, kv_rows):
    m, d = x.shape
    tn = 1024
    nblk = 6
    n_gate = wg.shape[1]
    wg_pad = jnp.zeros((d, LANES), BF16).at[:, :n_gate].set(wg)
    dils = DILATIONS[1:] if prompt else ()
    assert DILATIONS[2] % DILATIONS[1] == 0 and nblk == 6 and (m - kv_rows) % tm == 0
    const = lambda i, j: (0, 0)
    skip = (m - kv_rows) // tm
    if prompt:
        col = _column_order
        kv_c = lambda j: jnp.where(j < 4, 0, 1)
    else:
        col = lambda j: j
        kv_c = lambda j: jnp.clip(j - 1, 0, 1)
    kv_blk = lambda i, j: (jnp.where(i >= skip, kv_c(j), 0), jnp.maximum(i - skip, 0), 0)
    extra_out, extra_out_specs, extra_scratch = [], [], []
    if prompt:
        extra_out = [jax.ShapeDtypeStruct((tn // 2, m), BF16)] + [
            jax.ShapeDtypeStruct((3, dl, m // dl, tn), BF16) for dl in dils]
        extra_out_specs = [pl.BlockSpec((tn // 2, tm), lambda i, j: (0, i))] + [
            pl.BlockSpec((None, dl, tm // dl, tn), lambda i, j: (j // 2, 0, i, 0)) for dl in dils]
        extra_scratch = [pltpu.VMEM((tn // LANES, tm, LANES), F32),
                         pltpu.VMEM((tn // LANES, DILATIONS[1], tm // DILATIONS[1], LANES), F32)]
    return pl.pallas_call(
        functools.partial(_in_proj_kernel, prompt),
        out_shape=tuple([jax.ShapeDtypeStruct((nblk, m, tn), BF16),
                         jax.ShapeDtypeStruct((2, kv_rows, tn), F32),
                         jax.ShapeDtypeStruct((m, LANES), F32),
                         jax.ShapeDtypeStruct((n_gate, m), F32)] + extra_out),
        grid=(m // tm, nblk),
        in_specs=[pl.BlockSpec((tm, d), lambda i, j: (i, 0)),
                  pl.BlockSpec((1, d), const),
                  pl.BlockSpec(w_all.shape, const, pipeline_mode=pl.Buffered(1)),
                  pl.BlockSpec((d, LANES), const)],
        out_specs=tuple([pl.BlockSpec((None, tm, tn), lambda i, j: (col(j), i, 0)),
                         pl.BlockSpec((None, tm, tn), kv_blk),
                         pl.BlockSpec((tm, LANES), lambda i, j: (i, 0)),
                         pl.BlockSpec((n_gate, tm), lambda i, j: (0, i))] + extra_out_specs),
        scratch_shapes=[pltpu.VMEM((tm, d), BF16)] + extra_scratch,
        compiler_params=_params(("parallel", "arbitrary")),
        name="in_proj_dilated" if prompt else "in_proj",
    )(x, g, w_all, wg_pad)


def _dil_attn_kernel(heads, d, nq, rb_ref, bkt_ref, q_ref, kc_ref, kp_ref, vc_ref, vp_ref,
                     o_ref, lse_ref, bias, s_scr, p_scr):
    n = pl.program_id(0)
    r = pl.program_id(1)

    @pl.when((n == 0) & (r == 0))
    def _():
        bkt = bkt_ref[...]
        for h in range(heads):
            acc = jnp.full(bkt.shape, NEG, F32)
            for k in range(REL_BUCKETS):
                acc = jnp.where(bkt == k, rb_ref[k * heads + h], acc)
            bias[h] = acc

    blk = ATT_BLOCK
    in_prev = lax.broadcasted_iota(jnp.int32, (blk, 2 * blk), 1) < blk
    for u in range(nq):
        for h in range(heads):
            sl = slice(h * HEAD_DIM, (h + 1) * HEAD_DIM)
            qh = q_ref[u * blk:(u + 1) * blk, sl]
            if u == 0:
                keys = jnp.concatenate([kp_ref[:, sl], kc_ref[:blk, sl]], axis=0)
            else:
                keys = kc_ref[(u - 1) * blk:(u + 1) * blk, sl]
            s = _dot_nt(qh, keys) * ATT_SCALE + bias[h]
            if u == 0:
                s = jnp.where((n == 0) & in_prev, NEG, s)
            s_scr[u * heads + h] = s
    lane = lax.broadcasted_iota(jnp.int32, (blk, LANES), 1)
    m_tiles = []
    for u in range(nq):
        m_tile = jnp.zeros((blk, LANES), F32)
        for h in range(heads):
            s = s_scr[u * heads + h]
            m = jnp.max(s, axis=-1, keepdims=True)
            p_scr[u * heads + h] = jnp.exp(s - m).astype(BF16)
            m_tile = jnp.where(lane == h, m, m_tile)
        m_tiles.append(m_tile)
    ones = jnp.ones((2 * blk, HEAD_DIM), BF16)
    for u in range(nq):
        rows = pl.ds(r + u * blk * d, blk, stride=d)
        lse_tile = m_tiles[u]
        for h in range(heads):
            sl = slice(h * HEAD_DIM, (h + 1) * HEAD_DIM)
            if u == 0:
                vals = jnp.concatenate([vp_ref[:, sl], vc_ref[:blk, sl]], axis=0)
            else:
                vals = vc_ref[(u - 1) * blk:(u + 1) * blk, sl]
            o = _dot(p_scr[u * heads + h], jnp.concatenate([vals, ones], axis=1))
            l = o[:, HEAD_DIM:]
            o_ref[h, rows, :] = o[:, :HEAD_DIM] / l
            lse_tile = jnp.where(lane == h, lse_tile + jnp.log(l), lse_tile)
        lse_ref[rows, :] = lse_tile


DIL_ATTN_OUT_BLOCK_BYTES = 16 * 2 ** 20


def _dil_attn(zd, rel_flat, dilation, heads):
    d = dilation
    sub_len, w = zd.shape[2], zd.shape[3]
    s = sub_len * d
    blk = ATT_BLOCK
    nq = 2 if 2 * blk * d * w * 4 <= DIL_ATTN_OUT_BLOCK_BYTES else 1
    nb = sub_len // (nq * blk)
    qi = np.arange(blk)[:, None]
    ki = np.arange(2 * blk)[None, :]
    dist = blk + qi - ki
    bkt = np.where((dist >= 0) & (dist <= SUB_WINDOW), _rel_bucket(np.maximum(dist, 0) * d), -1).astype(np.int32)

    def cur(c):
        return pl.BlockSpec((None, None, nq * blk, w), lambda n, r: (c, r, n, 0))

    def prev(c):
        return pl.BlockSpec((None, None, blk, w), lambda n, r: (c, r, jnp.maximum(nq * n - 1, 0), 0))

    return pl.pallas_call(
        functools.partial(_dil_attn_kernel, heads, d, nq),
        out_shape=(jax.ShapeDtypeStruct((heads, s, HEAD_DIM), F32),
                   jax.ShapeDtypeStruct((s, LANES), F32)),
        grid=(nb, d),
        in_specs=[pl.BlockSpec(memory_space=pltpu.SMEM),
                  pl.BlockSpec((blk, 2 * blk), lambda n, r: (0, 0)),
                  cur(0), cur(1), prev(1), cur(2), prev(2)],
        out_specs=(pl.BlockSpec((heads, nq * blk * d, HEAD_DIM), lambda n, r: (0, n, 0)),
                   pl.BlockSpec((nq * blk * d, LANES), lambda n, r: (n, 0))),
        scratch_shapes=[pltpu.VMEM((heads, blk, 2 * blk), F32),
                        pltpu.VMEM((nq * heads, blk, 2 * blk), F32),
                        pltpu.VMEM((nq * heads, blk, 2 * blk), BF16)],
        compiler_params=_params(("arbitrary", "arbitrary")),
        name=f"dil_attn_d{d}",
    )(rel_flat, jnp.asarray(bkt), zd, zd, zd, zd, zd)


def _split_bf16(x):
    hi = x.astype(BF16)
    return hi, (x - hi.astype(F32)).astype(BF16)


def _mlstm_kernel(heads, n_side, qk_ref, kt_ref, v_ref, mo_ref, gt_ref, bgc_ref, gm_ref,
                  ce0_ref, m0_ref, *rest):
    step = pl.program_id(0)
    t = MLSTM_CHUNK
    qw = heads * MLSTM_DK
    dk = MLSTM_DK
    if n_side:
        side_in, rest = rest[:N_WIN_ATTN_IN], rest[N_WIN_ATTN_IN:]
        out_ref, ce_out, m_out, side_out, ce_s, m_s, sqk_s, tile_s, side_bias = rest
    else:
        out_ref, ce_out, m_out, ce_s, m_s, sqk_s, tile_s = rest

    @pl.when(step == 0)
    def _():
        ce_s[...] = ce0_ref[...]
        m_s[...] = m0_ref[...]
        if n_side:
            _sample_win_attn_init(side_in[0], side_in[1], side_bias)

    if n_side:
        _sample_win_attn_compute(n_side, side_in[0], *side_in[2:], side_out, side_bias)

    row = lax.broadcasted_iota(jnp.int32, (t, t), 0)
    col = lax.broadcasted_iota(jnp.int32, (t, t), 1)
    causal = col <= row
    eye = col == row
    upper = jnp.where(row <= col, 1.0, 0.0).astype(BF16)
    lane = lax.broadcasted_iota(jnp.int32, (1, LANES), 1)
    lane_h = lax.broadcasted_iota(jnp.int32, (heads, t), 1)
    blk_r = lax.broadcasted_iota(jnp.int32, (2 * t, 2 * t), 0) // t
    blk_c = lax.broadcasted_iota(jnp.int32, (2 * t, 2 * t), 1) // t
    ones_bd = jnp.where(blk_r == blk_c, 1.0, 0.0).astype(BF16)
    ones = jnp.ones((t, HEAD_DIM), BF16)

    gt = gt_ref[...] + bgc_ref[...]
    hi, lo = _split_bf16(_log_sigmoid(gt[heads:]))
    b2 = _dot(jnp.concatenate([hi, lo], axis=0), upper)
    b = b2[:heads] + b2[heads:]
    c = gt[:heads] - b
    m_prev = m_s[...]
    cmax = c
    for shift in (1, 2, 4, 8, 16, 32, 64):
        cmax = jnp.maximum(cmax, jnp.where(lane_h >= shift, pltpu.roll(cmax, shift, axis=1), NEG))
    a = jnp.maximum(cmax, m_prev)
    m_t = b + a
    a_last = a[:, t - 1:t]
    b_last = b[:, t - 1:t]
    m_new = b_last + a_last
    w_k = jnp.exp(c - a_last)
    decay = jnp.exp(b_last + m_prev - m_new)
    a_hi, a_lo = _split_bf16(a)
    mt_hi, mt_lo = _split_bf16(m_t)

    def diag(x_row):
        return jnp.where(eye, x_row.astype(F32), 0.0).astype(BF16)

    qms, dens = [], []
    for h in range(heads):
        pair, half = divmod(h, 2)
        hs = slice(h, h + 1)
        tile = (_dot(jnp.concatenate([diag(a_hi[hs]), diag(mt_hi[hs])], axis=1), ones_bd)
                + _dot(jnp.concatenate([diag(a_lo[hs]), diag(mt_lo[hs])], axis=1), ones_bd))
        tile_s[h] = tile
        w_intra = jnp.exp(jnp.where(causal, c[hs] - tile[:, :t], NEG))
        mine = (lane >= half * dk) & (lane < (half + 1) * dk)
        qm = jnp.where(mine, qk_ref[:, pair * LANES:(pair + 1) * LANES], 0.0) * QK_SCALE
        km = jnp.where(mine, qk_ref[:, qw + pair * LANES:qw + (pair + 1) * LANES], 0.0)
        sqk = _dot_nt(qm, km) * w_intra
        sqk_s[h] = sqk.astype(BF16)
        dens.append(jnp.sum(sqk, axis=-1, keepdims=True))
        qms.append(qm)

    for h in range(heads):
        pair = h // 2
        sl = slice(h * HEAD_DIM, (h + 1) * HEAD_DIM)
        tile = tile_s[h]
        w_prev = jnp.exp(m_prev[h:h + 1, :] - tile[:, :t])
        q_ce = _dot(qms[h], ce_s[pair * LANES:(pair + 1) * LANES, :].astype(BF16))
        num = _dot(sqk_s[h], v_ref[:, sl]) + w_prev * q_ce[:, :HEAD_DIM]
        den = dens[h] + w_prev * q_ce[:, HEAD_DIM:]
        hh = num / jnp.maximum(jnp.abs(den), jnp.exp(-tile[:, t:]))
        hn = hh * lax.rsqrt(jnp.mean(hh * hh, axis=-1, keepdims=True) + EPS)
        gate = jax.nn.sigmoid(mo_ref[:, sl].astype(F32))
        out_ref[:, sl] = (hn * gm_ref[:, sl] * gate).astype(out_ref.dtype)

    for h in range(heads):
        hs = slice(h * dk, (h + 1) * dk)
        kw = (kt_ref[hs, :].astype(F32) * w_k[h:h + 1, :]).astype(BF16)
        v_ext = jnp.concatenate([v_ref[:, h * HEAD_DIM:(h + 1) * HEAD_DIM], ones], axis=1)
        dec = jnp.concatenate([decay[h:h + 1, :], decay[h:h + 1, :]], axis=1)
        ce_s[hs, :] = dec * ce_s[hs, :] + _dot(kw, v_ext)
    m_s[...] = jnp.broadcast_to(m_new, (heads, LANES))

    @pl.when(step == pl.num_programs(0) - 1)
    def _():
        ce_out[...] = ce_s[...]
        m_out[...] = m_s[...]


def _mlstm_prompt(z6, k_t, gates_t, b_gate, g_mlstm, heads, side=None):
    _, s, w = z6.shape
    t = MLSTM_CHUNK
    qw = heads * MLSTM_DK
    n_steps = s // t
    ce0 = jnp.zeros((qw, 2 * HEAD_DIM), F32)
    m0 = jnp.zeros((heads, LANES), F32)
    const = lambda c: (0, 0)
    n_side, side_ops, side_specs, side_out, side_out_spec, side_scratch = 0, [], [], [], [], []
    if side is not None:
        n_side = side[0].shape[0] // n_steps
        assert n_side * n_steps == side[0].shape[0]
        side_ops, side_specs, out_shape, out_spec, side_scratch = _sample_win_attn_operands(*side, n_side)
        side_out, side_out_spec = [out_shape], [out_spec]
    return pl.pallas_call(
        functools.partial(_mlstm_kernel, heads, n_side),
        out_shape=tuple([jax.ShapeDtypeStruct((s, w), BF16),
                         jax.ShapeDtypeStruct((qw, 2 * HEAD_DIM), F32),
                         jax.ShapeDtypeStruct((heads, LANES), F32)] + side_out),
        grid=(n_steps,),
        in_specs=[pl.BlockSpec((None, t, w), lambda c: (3, c, 0)),
                  pl.BlockSpec((qw, t), lambda c: (0, c)),
                  pl.BlockSpec((None, t, w), lambda c: (4, c, 0)),
                  pl.BlockSpec((None, t, w), lambda c: (5, c, 0)),
                  pl.BlockSpec((2 * heads, t), lambda c: (0, c)),
                  pl.BlockSpec((2 * heads, 1), const),
                  pl.BlockSpec((1, w), const),
                  pl.BlockSpec((qw, 2 * HEAD_DIM), const),
                  pl.BlockSpec((heads, LANES), const)] + side_specs,
        out_specs=tuple([pl.BlockSpec((t, w), lambda c: (c, 0)),
                         pl.BlockSpec((qw, 2 * HEAD_DIM), const),
                         pl.BlockSpec((heads, LANES), const)] + side_out_spec),
        scratch_shapes=[pltpu.VMEM((qw, 2 * HEAD_DIM), F32),
                        pltpu.VMEM((heads, LANES), F32),
                        pltpu.VMEM((heads, t, t), BF16),
                        pltpu.VMEM((heads, t, 2 * t), F32)] + side_scratch,
        compiler_params=_params(("arbitrary",)),
        name="mlstm_prompt",
    )(z6, k_t, z6, z6, gates_t, b_gate.reshape(2 * heads, 1), g_mlstm.reshape(1, w), ce0, m0, *side_ops)


def _out_proj_kernel(merge, heads, *refs):
    if merge:
        (o1, o2, o3, l1, l2, l3, mo_ref, x_ref, wa_ref, wm_ref, gpost_ref, gpre_ref, wq_ref,
         x1_ref, qx_ref, att_s) = refs
        la, lb, lc = l1[...], l2[...], l3[...]
        mx = jnp.maximum(jnp.maximum(la, lb), lc)
        ea, eb, ec = jnp.exp(la - mx), jnp.exp(lb - mx), jnp.exp(lc - mx)
        tot = ea + eb + ec
        wa_, wb_, wc_ = ea / tot, eb / tot, ec / tot
        for h in range(heads):
            sl = slice(h * HEAD_DIM, (h + 1) * HEAD_DIM)
            mix = wa_[:, h:h + 1] * o1[h] + wb_[:, h:h + 1] * o2[h] + wc_[:, h:h + 1] * o3[h]
            att_s[:, sl] = mix.astype(BF16)
        att = att_s[...]
    else:
        att_ref, mo_ref, x_ref, wa_ref, wm_ref, gpost_ref, gpre_ref, wq_ref, x1_ref, qx_ref = refs
        att = att_ref[...]
    y = _dot(att, wa_ref[...]) + _dot(mo_ref[...], wm_ref[...])
    x1 = x_ref[...] + _rms(y, gpost_ref[...])
    x1_ref[...] = x1
    u = _rms(x1, gpre_ref[...]).astype(BF16)
    qx_ref[...] = _dot(u, wq_ref[...]).astype(qx_ref.dtype)


def _out_proj(att_parts, m_out, x, w_out, g_post, g_pre, w_xq, tm, heads):
    m, d = x.shape
    wm = m_out.shape[1]
    wa = w_out.shape[0] - wm
    assert wa == wm
    nq = w_xq.shape[1]
    merge = len(att_parts) == 6
    row = lambda i: (i, 0)
    const = lambda i: (0, 0)
    if merge:
        part_specs = ([pl.BlockSpec((heads, tm, HEAD_DIM), lambda i: (0, i, 0))] * 3
                      + [pl.BlockSpec((tm, LANES), row)] * 3)
    else:
        part_specs = [pl.BlockSpec((tm, wa), row)]
    return pl.pallas_call(
        functools.partial(_out_proj_kernel, merge, heads),
        out_shape=(jax.ShapeDtypeStruct((m, d), F32), jax.ShapeDtypeStruct((m, nq), BF16)),
        grid=(m // tm,),
        in_specs=part_specs + [pl.BlockSpec((tm, wm), row),
                               pl.BlockSpec((tm, d), row),
                               pl.BlockSpec((wa, d), const, pipeline_mode=pl.Buffered(1)),
                               pl.BlockSpec((wm, d), lambda i: (1, 0), pipeline_mode=pl.Buffered(1)),
                               pl.BlockSpec((1, d), const),
                               pl.BlockSpec((1, d), const),
                               pl.BlockSpec((d, nq), const, pipeline_mode=pl.Buffered(1))],
        out_specs=(pl.BlockSpec((tm, d), row), pl.BlockSpec((tm, nq), row)),
        scratch_shapes=[pltpu.VMEM((tm, wa), BF16)] if merge else [],
        compiler_params=_params(("parallel",)),
        name="out_proj_merge" if merge else "out_proj",
    )(*att_parts, m_out, x, w_out, w_out, g_post, g_pre, w_xq)


def _xattn_kernel(heads, q_ref, mk_ref, mv_ref, o_ref):
    for h in range(heads):
        sl = slice(h * HEAD_DIM, (h + 1) * HEAD_DIM)
        s = _dot_nt(q_ref[:, sl], mk_ref[:, sl].astype(BF16)) * ATT_SCALE
        m = jnp.max(s, axis=-1, keepdims=True)
        p = jnp.exp(s - m)
        l = jnp.sum(p, axis=-1, keepdims=True)
        o = _dot(p.astype(BF16), mv_ref[:, sl].astype(BF16))
        o_ref[:, sl] = (o / l).astype(o_ref.dtype)


def _xattn(qx, mem_k, mem_v, tm, heads):
    m, w = qx.shape
    n_mem = mem_k.shape[0]
    return pl.pallas_call(
        functools.partial(_xattn_kernel, heads),
        out_shape=jax.ShapeDtypeStruct((m, w), BF16),
        grid=(m // tm,),
        in_specs=[pl.BlockSpec((tm, w), lambda i: (i, 0)),
                  pl.BlockSpec((n_mem, w), lambda i: (0, 0)),
                  pl.BlockSpec((n_mem, w), lambda i: (0, 0))],
        out_specs=pl.BlockSpec((tm, w), lambda i: (i, 0)),
        compiler_params=_params(("parallel",)),
        name="xattn_prompt",
    )(qx, mem_k, mem_v)


def _proj_post_kernel(a_ref, x_ref, w_ref, g_ref, g_next_ref, y_ref, yn_ref):
    y = x_ref[...] + _rms(_dot(a_ref[...], w_ref[...]), g_ref[...])
    y_ref[...] = y
    yn_ref[...] = _rms(y, g_next_ref[...]).astype(BF16)


def _proj_post(a, x, w, g, g_next, tm):
    m, d = x.shape
    k = a.shape[1]
    row = lambda i: (i, 0)
    const = lambda i: (0, 0)
    return pl.pallas_call(
        _proj_post_kernel,
        out_shape=(jax.ShapeDtypeStruct((m, d), F32), jax.ShapeDtypeStruct((m, d), BF16)),
        grid=(m // tm,),
        in_specs=[pl.BlockSpec((tm, k), row),
                  pl.BlockSpec((tm, d), row),
                  pl.BlockSpec((k, d), const),
                  pl.BlockSpec((1, d), const),
                  pl.BlockSpec((1, d), const)],
        out_specs=(pl.BlockSpec((tm, d), row), pl.BlockSpec((tm, d), row)),
        compiler_params=_params(("parallel",)),
        name="proj_post",
    )(a, x, w, g, g_next)


def _mlp_kernel(n_side, x_ref, xn_ref, wu_ref, wd_ref, gpost_ref, *rest):
    if n_side:
        sq_ref, sk_ref, sv_ref, y_ref, so_ref, acc_s = rest
    else:
        y_ref, acc_s = rest
    j = pl.program_id(1)

    def step(first):
        hidden = jnp.square(jnp.maximum(_dot(xn_ref[...], wu_ref[...]), 0.0))
        update = _dot(hidden.astype(BF16), wd_ref[...])
        acc_s[...] = update if first else acc_s[...] + update
        for b in range(n_side):
            so_ref[b] = _single_query_attention(sq_ref[b], [(sk_ref[b], sv_ref[b], None)], None)

    pl.when(j == 0)(functools.partial(step, True))
    pl.when(j != 0)(functools.partial(step, False))

    @pl.when(j == pl.num_programs(1) - 1)
    def _():
        y_ref[...] = x_ref[...] + _rms(acc_s[...], gpost_ref[...])


def _mlp(x, xn, w_up, w_down, g_post, tm, tf, side=None):
    m, d = x.shape
    ff = w_up.shape[1]
    steps_j = ff // tf
    n_steps = (m // tm) * steps_j
    in_specs = [pl.BlockSpec((tm, d), lambda i, j: (i, 0)),
                pl.BlockSpec((tm, d), lambda i, j: (i, 0)),
                pl.BlockSpec((d, tf), lambda i, j: (0, j)),
                pl.BlockSpec((tf, d), lambda i, j: (j, 0)),
                pl.BlockSpec((1, d), lambda i, j: (0, 0))]
    out_shape = [jax.ShapeDtypeStruct((m, d), F32)]
    out_specs = [pl.BlockSpec((tm, d), lambda i, j: (i, 0))]
    n_side = 0
    if side is not None:
        sq, sk, sv = side
        nb, heads, hd = sq.shape
        n_side = nb // n_steps
        assert n_side * n_steps == nb
        tok = pl.BlockSpec((n_side, heads, hd), lambda i, j: (i * steps_j + j, 0, 0))
        mem = pl.BlockSpec((n_side,) + sk.shape[1:], lambda i, j: (i * steps_j + j, 0, 0, 0))
        in_specs += [tok, mem, mem]
        out_shape.append(jax.ShapeDtypeStruct((nb, heads, hd), F32))
        out_specs.append(tok)
    out = pl.pallas_call(
        functools.partial(_mlp_kernel, n_side),
        out_shape=tuple(out_shape),
        grid=(m // tm, steps_j),
        in_specs=in_specs,
        out_specs=tuple(out_specs),
        scratch_shapes=[pltpu.VMEM((tm, d), F32)],
        compiler_params=_params(("parallel", "arbitrary")),
        name="mlp_side" if side is not None else "mlp",
    )(x, xn, w_up, w_down, g_post, *(side or ()))
    return out if side is not None else out[0]


def _single_query_attention(q, groups, new):
    q = q * ATT_SCALE
    scores = []
    for k3, _, b3 in groups:
        s = jnp.sum(k3 * q[None], axis=-1, keepdims=True)
        scores.append(s if b3 is None else s + b3)
    mx = jnp.max(scores[0], axis=0)
    for s in scores[1:]:
        mx = jnp.maximum(mx, jnp.max(s, axis=0))
    if new is not None:
        k_new, v_new, b_new, count = new
        s_new = jnp.sum(k_new * q, axis=-1, keepdims=True) + b_new
        mx = jnp.maximum(mx, s_new)
    den = jnp.zeros(mx.shape, F32)
    acc = jnp.zeros(q.shape, F32)
    for s, (_, v3, _) in zip(scores, groups):
        p = jnp.exp(s - mx[None])
        den = den + jnp.sum(p, axis=0)
        acc = acc + jnp.sum(p * v3, axis=0)
    if new is not None:
        p_new = jnp.exp(s_new - mx) * count
        den = den + p_new
        acc = acc + p_new * v_new
    return acc / den


N_WIN_ATTN_IN = 5 + 2 * len(DILATIONS)


def _sample_win_attn_init(rel_ref, bkt_ref, bias_s):
    for p in range(len(DILATIONS)):
        bkt = bkt_ref[p]
        acc = jnp.zeros(bkt.shape, F32)
        for k in range(REL_BUCKETS):
            acc = jnp.where(bkt == k, rel_ref[k][None], acc)
        bias_s[p] = acc


def _sample_win_attn_compute(tb, rel_ref, q_ref, kn_ref, vn_ref, k1, k4, k16, v1, v4, v16, o_ref, bias_s):
    for b in range(tb):
        groups = [(k[b], v[b], bias_s[p]) for p, (k, v) in enumerate(((k1, v1), (k4, v4), (k16, v16)))]
        new = (kn_ref[b], vn_ref[b], rel_ref[0], float(len(DILATIONS)))
        o_ref[b] = _single_query_attention(q_ref[b], groups, new)


def _sample_win_attn_operands(q, k_new, v_new, cache_k, cache_v, rel_bias, tb):
    nb, heads, hd = q.shape
    buf = cache_k.shape[1]
    r = SUB_WINDOW
    assert buf == r * DILATIONS[-1], "window buffer must hold exactly the widest dilated pattern"
    j = r - np.arange(r)
    bkt = np.stack([_rel_bucket(j * d) for d in DILATIONS]).astype(np.int32)
    bkt = np.broadcast_to(bkt[:, :, None, None], (len(DILATIONS), r, heads, 1))
    views, specs = [], []
    for cache in (cache_k, cache_v):
        for d in DILATIONS:
            views.append(cache.reshape(nb, buf // d, d, heads, hd))
            specs.append(pl.BlockSpec((tb, r, None, heads, hd), functools.partial(
                lambda i, blk: (i, blk, 0, 0, 0), blk=buf // d // r - 1)))
    tok = pl.BlockSpec((tb, heads, hd), lambda i: (i, 0, 0))
    operands = [rel_bias.reshape(REL_BUCKETS, heads, 1), jnp.asarray(bkt), q, k_new, v_new] + views
    in_specs = [pl.BlockSpec((REL_BUCKETS, heads, 1), lambda i: (0, 0, 0)),
                pl.BlockSpec((len(DILATIONS), r, heads, 1), lambda i: (0, 0, 0, 0)),
                tok, tok, tok] + specs
    scratch = [pltpu.VMEM((len(DILATIONS), r, heads, 1), F32)]
    return operands, in_specs, jax.ShapeDtypeStruct((nb, heads, hd), F32), tok, scratch


def _sample_mlstm_kernel(heads, tb, q_ref, k_ref, v_ref, mo_ref, li_ref, lf_ref, bi_ref, bf_ref,
                         gm_ref, c_ref, n_ref, m_ref, out_ref, c_out, n_out, m_out):
    qw = heads * MLSTM_DK
    hrow = lax.broadcasted_iota(jnp.int32, (LANES, qw), 0)
    hcol = lax.broadcasted_iota(jnp.int32, (LANES, qw), 1) // MLSTM_DK
    diag = hrow == hcol
    rows8 = slice(0, heads)
    for b in range(tb):
        q_bd = jnp.where(diag, q_ref[b:b + 1, :], 0.0) * QK_SCALE
        k_bd = jnp.where(diag, k_ref[b:b + 1, :], 0.0)
        c_b = c_ref[b]
        n_row = n_ref[b:b + 1, :]
        li = li_ref[b] + bi_ref[...]
        lf = _log_sigmoid(lf_ref[b] + bf_ref[...])
        m_prev = m_ref[b]
        qk = jnp.sum(q_bd * k_ref[b:b + 1, :], axis=-1, keepdims=True)[rows8]
        qn = jnp.sum(q_bd * n_row, axis=-1, keepdims=True)[rows8]
        q_c = _dot(q_bd.astype(BF16), c_b.astype(BF16))[rows8]
        inter = lf + m_prev
        m_t = jnp.maximum(inter, li)
        w_intra = jnp.exp(li - m_t)
        w_prev = jnp.exp(inter - m_t)
        sqk = qk * w_intra
        v_b = v_ref[b]
        num = sqk * v_b + w_prev * q_c
        den = sqk + w_prev * qn
        hh = num / jnp.maximum(jnp.abs(den), jnp.exp(-m_t))
        hn = hh * lax.rsqrt(jnp.mean(hh * hh, axis=-1, keepdims=True) + EPS)
        out_ref[b] = hn * gm_ref[...] * jax.nn.sigmoid(mo_ref[b])
        wv = jnp.concatenate([w_intra * v_b, jnp.zeros((LANES - heads, HEAD_DIM), F32)], axis=0)
        d_c = _dot_tn(k_bd.astype(BF16), wv.astype(BF16))
        for h in range(heads):
            hs = slice(h * MLSTM_DK, (h + 1) * MLSTM_DK)
            c_out[b, hs, :] = w_prev[h:h + 1, :] * c_b[hs, :] + d_c[hs, :]
        k_w = jnp.sum(k_bd[rows8] * w_intra, axis=0, keepdims=True)
        dec_row = jnp.sum(jnp.where(diag[rows8], w_prev, 0.0), axis=0, keepdims=True)
        n_out[b:b + 1, :] = dec_row * n_row + k_w
        m_out[b] = m_t


def _sample_mlstm(z6, gates, b_gate, g_mlstm, state_c, state_n, state_m, heads, tb):
    _, nb, w = z6.shape
    qw = heads * MLSTM_DK
    q = z6[3, :, :qw].astype(F32)
    k = z6[3, :, qw:].astype(F32)
    v = z6[4].astype(F32).reshape(nb, heads, HEAD_DIM)
    mo = z6[5].astype(F32).reshape(nb, heads, HEAD_DIM)
    li = gates[:, :heads].reshape(nb, heads, 1)
    lf = gates[:, heads:2 * heads].reshape(nb, heads, 1)
    row = lambda i: (i, 0)
    blk3 = lambda i: (i, 0, 0)
    const = lambda i: (0, 0)
    return pl.pallas_call(
        functools.partial(_sample_mlstm_kernel, heads, tb),
        out_shape=(jax.ShapeDtypeStruct((nb, heads, HEAD_DIM), F32),
                   jax.ShapeDtypeStruct((nb, qw, HEAD_DIM), F32),
                   jax.ShapeDtypeStruct((nb, qw), F32),
                   jax.ShapeDtypeStruct((nb, heads, 1), F32)),
        grid=(nb // tb,),
        in_specs=[pl.BlockSpec((tb, qw), row),
                  pl.BlockSpec((tb, qw), row),
                  pl.BlockSpec((tb, heads, HEAD_DIM), blk3),
                  pl.BlockSpec((tb, heads, HEAD_DIM), blk3),
                  pl.BlockSpec((tb, heads, 1), blk3),
                  pl.BlockSpec((tb, heads, 1), blk3),
                  pl.BlockSpec((heads, 1), const),
                  pl.BlockSpec((heads, 1), const),
                  pl.BlockSpec((heads, HEAD_DIM), const),
                  pl.BlockSpec((tb, qw, HEAD_DIM), blk3),
                  pl.BlockSpec((tb, qw), row),
                  pl.BlockSpec((tb, heads, 1), blk3)],
        out_specs=(pl.BlockSpec((tb, heads, HEAD_DIM), blk3),
                   pl.BlockSpec((tb, qw, HEAD_DIM), blk3),
                   pl.BlockSpec((tb, qw), row),
                   pl.BlockSpec((tb, heads, 1), blk3)),
        compiler_params=_params(("parallel",)),
        name="sample_mlstm",
    )(q, k, v, mo, li, lf, b_gate[:heads].reshape(heads, 1), b_gate[heads:].reshape(heads, 1),
      g_mlstm.reshape(heads, HEAD_DIM), state_c.reshape(nb, qw, HEAD_DIM), state_n.reshape(nb, qw),
      state_m.reshape(nb, heads, 1))


ROW_TILE = 512
MLP_FF_TILE = 1024
SAMPLE_STATE_TILE = 8


def _tile(m, cap):
    return min(m, cap)


def kernel(x_prompt, x_sample, mem_prompt, cache_win_k, cache_win_v, state_mlstm_C, state_mlstm_n,
           state_mlstm_m, cache_mem_k, cache_mem_v, rel_bias, g_pre_mix, g_post_mix, w_in, b_gate,
           g_mlstm, w_out, g_pre_xatt, g_post_xatt, g_mem, w_xq, w_xkv, w_xo, g_pre_mlp, g_post_mlp,
           w_up, w_down):
    depth = w_in.shape[0]
    assert depth == 1 and x_prompt.shape[0] == 1 and x_sample.shape[1] == 1
    _, seq, d_model = x_prompt.shape
    n_dec = x_sample.shape[0]
    att_heads = cache_win_k.shape[3]
    ml_heads = state_mlstm_C.shape[2]
    x_heads = cache_mem_k.shape[3]
    att_w = att_heads * HEAD_DIM
    ml_w = ml_heads * HEAD_DIM
    x_w = x_heads * HEAD_DIM
    n_mem = mem_prompt.shape[1]
    keep = cache_win_k.shape[2]
    n_main = 3 * att_w + 2 * ml_heads * MLSTM_DK + 2 * ml_w
    assert att_w == ml_w == 1024 and n_main % 1024 == 0 and seq >= keep

    def row(g):
        return g[0].reshape(1, -1)

    assert n_main == 6 * 1024
    w_main = w_in[0].astype(BF16)
    wg = w_in[0, :, n_main:].astype(BF16)
    w_out_b = w_out[0].astype(BF16)
    wq = w_xq[0].astype(BF16)
    wkv = w_xkv[0].astype(BF16)
    wo = w_xo[0].astype(BF16)
    wu = w_up[0].astype(BF16)
    wd = w_down[0].astype(BF16)
    rel_flat = rel_bias.reshape(-1)

    tm = _tile(seq, ROW_TILE)

    def mlp(x2, x2n, rows, side=None):
        return _mlp(x2, x2n, wu, wd, row(g_post_mlp), rows, MLP_FF_TILE, side)

    def mem_out_proj(o, x1, rows):
        return _proj_post(o, x1, wo, row(g_post_xatt), row(g_pre_mlp), rows)

    xs = x_sample[:, 0]
    z6s, kv32s, gates_s, _ = _in_proj(xs, row(g_pre_mix), w_main, wg, n_dec, False, n_dec)
    per_head = (n_dec, att_heads, HEAD_DIM)
    win_attn_args = (z6s[0].astype(F32).reshape(per_head), kv32s[0].reshape(per_head),
                     kv32s[1].reshape(per_head), cache_win_k[0], cache_win_v[0], rel_bias)

    xp = x_prompt[0]
    mem_kv = _norm_matmul(mem_prompt[0], row(g_mem), wkv)
    mk_p, mv_p = mem_kv[:, :x_w], mem_kv[:, x_w:]
    z6, kv32, _, gates_t, k_t, *zd = _in_proj(xp, row(g_pre_mix), w_main, wg, tm, True, keep)
    zd = [z6.reshape(z6.shape[0], 1, seq, att_w)] + zd
    parts = [_dil_attn(z, rel_flat, d, att_heads) for z, d in zip(zd, DILATIONS)]
    m_out, ce_p, m_p, att_s = _mlstm_prompt(z6, k_t, gates_t, b_gate[0], g_mlstm[0], ml_heads, win_attn_args)
    c_p, n_p = ce_p[:, :HEAD_DIM], ce_p[:, HEAD_DIM]
    x1, qx = _out_proj(tuple(p[0] for p in parts) + tuple(p[1] for p in parts), m_out, xp, w_out_b,
                       row(g_post_mix), row(g_pre_xatt), wq, tm, att_heads)
    x2, x2n = mem_out_proj(_xattn(qx, mk_p, mv_p, tm, x_heads), x1, tm)

    mo_s, c_s, n_s, m_s = _sample_mlstm(z6s, gates_s, b_gate[0], g_mlstm[0], state_mlstm_C[0],
                                        state_mlstm_n[0], state_mlstm_m[0], ml_heads, SAMPLE_STATE_TILE)
    x1s, qxs = _out_proj((att_s.reshape(n_dec, att_w).astype(BF16),), mo_s.reshape(n_dec, ml_w).astype(BF16),
                         xs, w_out_b, row(g_post_mix), row(g_pre_xatt), wq, n_dec, att_heads)

    q3 = qxs.astype(F32).reshape(n_dec, x_heads, HEAD_DIM)
    y_prompt, o_mem = mlp(x2, x2n, tm, (q3, cache_mem_k[0], cache_mem_v[0]))
    x2s, x2sn = mem_out_proj(o_mem.reshape(n_dec, x_w).astype(BF16), x1s, n_dec)
    y_sample = mlp(x2s, x2sn, n_dec)

    dk = MLSTM_DK
    return (y_prompt[None],
            y_sample[:, None],
            kv32[0].reshape(1, 1, keep, att_heads, HEAD_DIM),
            kv32[1].reshape(1, 1, keep, att_heads, HEAD_DIM),
            c_p.reshape(1, 1, ml_heads, dk, HEAD_DIM),
            n_p.reshape(1, 1, ml_heads, dk),
            m_p[:, 0].reshape(1, 1, ml_heads),
            mk_p.reshape(1, 1, n_mem, x_heads, HEAD_DIM),
            mv_p.reshape(1, 1, n_mem, x_heads, HEAD_DIM),
            kv32s[0].reshape(1, n_dec, 1, att_heads, HEAD_DIM),
            kv32s[1].reshape(1, n_dec, 1, att_heads, HEAD_DIM),
            c_s.reshape(1, n_dec, ml_heads, dk, HEAD_DIM),
            n_s.reshape(1, n_dec, ml_heads, dk),
            m_s.reshape(1, n_dec, ml_heads))
```

```python
import functools
import math

import numpy as np
import jax
import jax.numpy as jnp
from jax import lax
from jax.experimental import pallas as pl
from jax.experimental.pallas import tpu as pltpu

F32 = jnp.float32
BF16 = jnp.bfloat16

EPS = 1e-6
HEAD_DIM = 128
MLSTM_DK = 64
LANES = 128
DILATIONS = (1, 4, 16)
SUB_WINDOW = 128
ATT_BLOCK = 128
MLSTM_CHUNK = 128
REL_BUCKETS = 32
REL_MAX_DIST = 2048
NEG = -1e30
ATT_SCALE = HEAD_DIM ** -0.5
QK_SCALE = MLSTM_DK ** -0.5
VMEM_LIMIT = 56 * 2 ** 20


def _params(semantics):
    return pltpu.CompilerParams(dimension_semantics=semantics, vmem_limit_bytes=VMEM_LIMIT)


def _rms(xf, g):
    ms = jnp.mean(xf * xf, axis=-1, keepdims=True)
    return xf * lax.rsqrt(ms + EPS) * g


def _dot(a, b):
    return jnp.dot(a, b, preferred_element_type=F32)


def _dot_nt(a, b):
    return lax.dot_general(a, b, (((1,), (1,)), ((), ())), preferred_element_type=F32)


def _dot_tn(a, b):
    return lax.dot_general(a, b, (((0,), (0,)), ((), ())), preferred_element_type=F32)


def _log_sigmoid(x):
    return jnp.minimum(x, 0.0) - jnp.log(1.0 + jnp.exp(-jnp.abs(x)))


def _rel_bucket(dist):
    dist = np.asarray(dist)
    exact = REL_BUCKETS // 2
    far = exact + (np.log(np.maximum(dist, exact) / exact) / math.log(REL_MAX_DIST / exact)
                   * (REL_BUCKETS - exact)).astype(np.int32)
    return np.where(dist < exact, dist, np.minimum(far, REL_BUCKETS - 1)).astype(np.int32)


def _norm_matmul_kernel(x_ref, g_ref, w_ref, o_ref):
    xn = _rms(x_ref[...], g_ref[...]).astype(BF16)
    o_ref[...] = _dot(xn, w_ref[...])


def _norm_matmul(x, g, w):
    m, d = x.shape
    n = w.shape[1]
    return pl.pallas_call(
        _norm_matmul_kernel,
        out_shape=jax.ShapeDtypeStruct((m, n), F32),
        grid=(1,),
        in_specs=[pl.BlockSpec((m, d), lambda i: (0, 0)),
                  pl.BlockSpec((1, d), lambda i: (0, 0)),
                  pl.BlockSpec((d, n), lambda i: (0, 0))],
        out_specs=pl.BlockSpec((m, n), lambda i: (0, 0)),
        compiler_params=_params(("arbitrary",)),
        name="norm_matmul",
    )(x, g, w)


def _column_order(j):
    return (j % 2) * 3 + j // 2


def _in_proj_kernel(prompt, x_ref, g_ref, w_ref, wg_ref, *rest):
    if prompt:
        z_ref, kv_ref, gate_ref, gate_t_ref, kt_ref, zd_mid, zd_wide, xn_ref, slab, slab_mid = rest
    else:
        z_ref, kv_ref, gate_ref, gate_t_ref, xn_ref = rest
    j = pl.program_id(1)
    tm, tn = z_ref.shape
    n_steps = 6

    def project(step, c):
        if step == 0:
            xn = _rms(x_ref[...], g_ref[...]).astype(BF16)
            xn_ref[...] = xn
            gates = _dot(xn, wg_ref[...])
            gate_ref[...] = gates
            gate_t_ref[...] = gates.T[:gate_t_ref.shape[0]]
        else:
            xn = xn_ref[...]
        acc = _dot(xn, w_ref[:, c * tn:(c + 1) * tn])
        z_ref[...] = acc.astype(BF16)
        return acc

    if not prompt:
        for step in range(n_steps):
            @pl.when(j == step)
            def _(step=step):
                acc = project(step, step)
                if step in (1, 2):
                    kv_ref[...] = acc
        return

    mid = DILATIONS[1]
    ratio = DILATIONS[2] // mid

    def even_step(step):
        acc = project(step, _column_order(step))
        for h in range(tn // LANES):
            slab[h] = acc[:, h * LANES:(h + 1) * LANES]
        if step >= 2:
            kv_ref[...] = acc

    def odd_step(step):
        acc = project(step, _column_order(step))
        for h in range(tn // LANES):
            hl = slice(h * LANES, (h + 1) * LANES)
            for r in range(mid):
                rows = slab[h, pl.ds(r, tm // mid, stride=mid), :]
                zd_mid[r, :, hl] = rows.astype(BF16)
                slab_mid[h, r] = rows
            for r in range(mid):
                for q in range(ratio):
                    rows = slab_mid[h, r, pl.ds(q, tm // (mid * ratio), stride=ratio), :]
                    zd_wide[r + mid * q, :, hl] = rows.astype(BF16)
        if step == 1:
            kt_ref[...] = acc[:, tn // 2:].T.astype(BF16)

    for step in range(n_steps):
        pl.when(j == step)(functools.partial(odd_step if step % 2 else even_step, step))


def _in_proj(x, g, w_all, wg, tm, prompt, kv_rows):
    m, d = x.shape
    tn = 1024
    nblk = 6
    n_gate = wg.shape[1]
    wg_pad = jnp.zeros((d, LANES), BF16).at[:, :n_gate].set(wg)
    dils = DILATIONS[1:] if prompt else ()
    assert DILATIONS[2] % DILATIONS[1] == 0 and nblk == 6 and (m - kv_rows) % tm == 0
    const = lambda i, j: (0, 0)
    skip = (m - kv_rows) // tm
    if prompt:
        col = _column_order
        kv_c = lambda j: jnp.where(j < 4, 0, 1)
    else:
        col = lambda j: j
        kv_c = lambda j: jnp.clip(j - 1, 0, 1)
    kv_blk = lambda i, j: (jnp.where(i >= skip, kv_c(j), 0), jnp.maximum(i - skip, 0), 0)
    extra_out, extra_out_specs, extra_scratch = [], [], []
    if prompt:
        extra_out = [jax.ShapeDtypeStruct((tn // 2, m), BF16)] + [
            jax.ShapeDtypeStruct((3, dl, m // dl, tn), BF16) for dl in dils]
        extra_out_specs = [pl.BlockSpec((tn // 2, tm), lambda i, j: (0, i))] + [
            pl.BlockSpec((None, dl, tm // dl, tn), lambda i, j: (j // 2, 0, i, 0)) for dl in dils]
        extra_scratch = [pltpu.VMEM((tn // LANES, tm, LANES), F32),
                         pltpu.VMEM((tn // LANES, DILATIONS[1], tm // DILATIONS[1], LANES), F32)]
    return pl.pallas_call(
        functools.partial(_in_proj_kernel, prompt),
        out_shape=tuple([jax.ShapeDtypeStruct((nblk, m, tn), BF16),
                         jax.ShapeDtypeStruct((2, kv_rows, tn), F32),
                         jax.ShapeDtypeStruct((m, LANES), F32),
                         jax.ShapeDtypeStruct((n_gate, m), F32)] + extra_out),
        grid=(m // tm, nblk),
        in_specs=[pl.BlockSpec((tm, d), lambda i, j: (i, 0)),
                  pl.BlockSpec((1, d), const),
                  pl.BlockSpec(w_all.shape, const, pipeline_mode=pl.Buffered(1)),
                  pl.BlockSpec((d, LANES), const)],
        out_specs=tuple([pl.BlockSpec((None, tm, tn), lambda i, j: (col(j), i, 0)),
                         pl.BlockSpec((None, tm, tn), kv_blk),
                         pl.BlockSpec((tm, LANES), lambda i, j: (i, 0)),
                         pl.BlockSpec((n_gate, tm), lambda i, j: (0, i))] + extra_out_specs),
        scratch_shapes=[pltpu.VMEM((tm, d), BF16)] + extra_scratch,
        compiler_params=_params(("parallel", "arbitrary")),
        name="in_proj_dilated" if prompt else "in_proj",
    )(x, g, w_all, wg_pad)


def _dil_attn_kernel(heads, d, nq, rb_ref, bkt_ref, q_ref, kc_ref, kp_ref, vc_ref, vp_ref,
                     o_ref, lse_ref, bias, s_scr, p_scr):
    n = pl.program_id(0)
    r = pl.program_id(1)

    @pl.when((n == 0) & (r == 0))
    def _():
        bkt = bkt_ref[...]
        for h in range(heads):
            acc = jnp.full(bkt.shape, NEG, F32)
            for k in range(REL_BUCKETS):
                acc = jnp.where(bkt == k, rb_ref[k * heads + h], acc)
            bias[h] = acc

    blk = ATT_BLOCK
    in_prev = lax.broadcasted_iota(jnp.int32, (blk, 2 * blk), 1) < blk
    for u in range(nq):
        for h in range(heads):
            sl = slice(h * HEAD_DIM, (h + 1) * HEAD_DIM)
            qh = q_ref[u * blk:(u + 1) * blk, sl]
            if u == 0:
                keys = jnp.concatenate([kp_ref[:, sl], kc_ref[:blk, sl]], axis=0)
            else:
                keys = kc_ref[(u - 1) * blk:(u + 1) * blk, sl]
            s = _dot_nt(qh, keys) * ATT_SCALE + bias[h]
            if u == 0:
                s = jnp.where((n == 0) & in_prev, NEG, s)
            s_scr[u * heads + h] = s
    lane = lax.broadcasted_iota(jnp.int32, (blk, LANES), 1)
    m_tiles = []
    for u in range(nq):
        m_tile = jnp.zeros((blk, LANES), F32)
        for h in range(heads):
            s = s_scr[u * heads + h]
            m = jnp.max(s, axis=-1, keepdims=True)
            p_scr[u * heads + h] = jnp.exp(s - m).astype(BF16)
            m_tile = jnp.where(lane == h, m, m_tile)
        m_tiles.append(m_tile)
    ones = jnp.ones((2 * blk, HEAD_DIM), BF16)
    for u in range(nq):
        rows = pl.ds(r + u * blk * d, blk, stride=d)
        lse_tile = m_tiles[u]
        for h in range(heads):
            sl = slice(h * HEAD_DIM, (h + 1) * HEAD_DIM)
            if u == 0:
                vals = jnp.concatenate([vp_ref[:, sl], vc_ref[:blk, sl]], axis=0)
            else:
                vals = vc_ref[(u - 1) * blk:(u + 1) * blk, sl]
            o = _dot(p_scr[u * heads + h], jnp.concatenate([vals, ones], axis=1))
            l = o[:, HEAD_DIM:]
            o_ref[h, rows, :] = o[:, :HEAD_DIM] / l
            lse_tile = jnp.where(lane == h, lse_tile + jnp.log(l), lse_tile)
        lse_ref[rows, :] = lse_tile


DIL_ATTN_OUT_BLOCK_BYTES = 16 * 2 ** 20


def _dil_attn(zd, rel_flat, dilation, heads):
    d = dilation
    sub_len, w = zd.shape[2], zd.shape[3]
    s = sub_len * d
    blk = ATT_BLOCK
    nq = 2 if 2 * blk * d * w * 4 <= DIL_ATTN_OUT_BLOCK_BYTES else 1
    nb = sub_len // (nq * blk)
    qi = np.arange(blk)[:, None]
    ki = np.arange(2 * blk)[None, :]
    dist = blk + qi - ki
    bkt = np.where((dist >= 0) & (dist <= SUB_WINDOW), _rel_bucket(np.maximum(dist, 0) * d), -1).astype(np.int32)

    def cur(c):
        return pl.BlockSpec((None, None, nq * blk, w), lambda n, r: (c, r, n, 0))

    def prev(c):
        return pl.BlockSpec((None, None, blk, w), lambda n, r: (c, r, jnp.maximum(nq * n - 1, 0), 0))

    return pl.pallas_call(
        functools.partial(_dil_attn_kernel, heads, d, nq),
        out_shape=(jax.ShapeDtypeStruct((heads, s, HEAD_DIM), F32),
                   jax.ShapeDtypeStruct((s, LANES), F32)),
        grid=(nb, d),
        in_specs=[pl.BlockSpec(memory_space=pltpu.SMEM),
                  pl.BlockSpec((blk, 2 * blk), lambda n, r: (0, 0)),
                  cur(0), cur(1), prev(1), cur(2), prev(2)],
        out_specs=(pl.BlockSpec((heads, nq * blk * d, HEAD_DIM), lambda n, r: (0, n, 0)),
                   pl.BlockSpec((nq * blk * d, LANES), lambda n, r: (n, 0))),
        scratch_shapes=[pltpu.VMEM((heads, blk, 2 * blk), F32),
                        pltpu.VMEM((nq * heads, blk, 2 * blk), F32),
                        pltpu.VMEM((nq * heads, blk, 2 * blk), BF16)],
        compiler_params=_params(("arbitrary", "arbitrary")),
        name=f"dil_attn_d{d}",
    )(rel_flat, jnp.asarray(bkt), zd, zd, zd, zd, zd)


def _split_bf16(x):
    hi = x.astype(BF16)
    return hi, (x - hi.astype(F32)).astype(BF16)


def _mlstm_kernel(heads, n_side, qk_ref, kt_ref, v_ref, mo_ref, gt_ref, bgc_ref, gm_ref,
                  ce0_ref, m0_ref, *rest):
    step = pl.program_id(0)
    t = MLSTM_CHUNK
    qw = heads * MLSTM_DK
    dk = MLSTM_DK
    if n_side:
        side_in, rest = rest[:N_WIN_ATTN_IN], rest[N_WIN_ATTN_IN:]
        out_ref, ce_out, m_out, side_out, ce_s, m_s, sqk_s, tile_s, side_bias = rest
    else:
        out_ref, ce_out, m_out, ce_s, m_s, sqk_s, tile_s = rest

    @pl.when(step == 0)
    def _():
        ce_s[...] = ce0_ref[...]
        m_s[...] = m0_ref[...]
        if n_side:
            _sample_win_attn_init(side_in[0], side_in[1], side_bias)

    if n_side:
        _sample_win_attn_compute(n_side, side_in[0], *side_in[2:], side_out, side_bias)

    row = lax.broadcasted_iota(jnp.int32, (t, t), 0)
    col = lax.broadcasted_iota(jnp.int32, (t, t), 1)
    causal = col <= row
    eye = col == row
    upper = jnp.where(row <= col, 1.0, 0.0).astype(BF16)
    lane = lax.broadcasted_iota(jnp.int32, (1, LANES), 1)
    lane_h = lax.broadcasted_iota(jnp.int32, (heads, t), 1)
    blk_r = lax.broadcasted_iota(jnp.int32, (2 * t, 2 * t), 0) // t
    blk_c = lax.broadcasted_iota(jnp.int32, (2 * t, 2 * t), 1) // t
    ones_bd = jnp.where(blk_r == blk_c, 1.0, 0.0).astype(BF16)
    ones = jnp.ones((t, HEAD_DIM), BF16)

    gt = gt_ref[...] + bgc_ref[...]
    hi, lo = _split_bf16(_log_sigmoid(gt[heads:]))
    b2 = _dot(jnp.concatenate([hi, lo], axis=0), upper)
    b = b2[:heads] + b2[heads:]
    c = gt[:heads] - b
    m_prev = m_s[...]
    cmax = c
    for shift in (1, 2, 4, 8, 16, 32, 64):
        cmax = jnp.maximum(cmax, jnp.where(lane_h >= shift, pltpu.roll(cmax, shift, axis=1), NEG))
    a = jnp.maximum(cmax, m_prev)
    m_t = b + a
    a_last = a[:, t - 1:t]
    b_last = b[:, t - 1:t]
    m_new = b_last + a_last
    w_k = jnp.exp(c - a_last)
    decay = jnp.exp(b_last + m_prev - m_new)
    a_hi, a_lo = _split_bf16(a)
    mt_hi, mt_lo = _split_bf16(m_t)

    def diag(x_row):
        return jnp.where(eye, x_row.astype(F32), 0.0).astype(BF16)

    qms, dens = [], []
    for h in range(heads):
        pair, half = divmod(h, 2)
        hs = slice(h, h + 1)
        tile = (_dot(jnp.concatenate([diag(a_hi[hs]), diag(mt_hi[hs])], axis=1), ones_bd)
                + _dot(jnp.concatenate([diag(a_lo[hs]), diag(mt_lo[hs])], axis=1), ones_bd))
        tile_s[h] = tile
        w_intra = jnp.exp(jnp.where(causal, c[hs] - tile[:, :t], NEG))
        mine = (lane >= half * dk) & (lane < (half + 1) * dk)
        qm = jnp.where(mine, qk_ref[:, pair * LANES:(pair + 1) * LANES], 0.0) * QK_SCALE
        km = jnp.where(mine, qk_ref[:, qw + pair * LANES:qw + (pair + 1) * LANES], 0.0)
        sqk = _dot_nt(qm, km) * w_intra
        sqk_s[h] = sqk.astype(BF16)
        dens.append(jnp.sum(sqk, axis=-1, keepdims=True))
        qms.append(qm)

    for h in range(heads):
        pair = h // 2
        sl = slice(h * HEAD_DIM, (h + 1) * HEAD_DIM)
        tile = tile_s[h]
        w_prev = jnp.exp(m_prev[h:h + 1, :] - tile[:, :t])
        q_ce = _dot(qms[h], ce_s[pair * LANES:(pair + 1) * LANES, :].astype(BF16))
        num = _dot(sqk_s[h], v_ref[:, sl]) + w_prev * q_ce[:, :HEAD_DIM]
        den = dens[h] + w_prev * q_ce[:, HEAD_DIM:]
        hh = num / jnp.maximum(jnp.abs(den), jnp.exp(-tile[:, t:]))
        hn = hh * lax.rsqrt(jnp.mean(hh * hh, axis=-1, keepdims=True) + EPS)
        gate = jax.nn.sigmoid(mo_ref[:, sl].astype(F32))
        out_ref[:, sl] = (hn * gm_ref[:, sl] * gate).astype(out_ref.dtype)

    for h in range(heads):
        hs = slice(h * dk, (h + 1) * dk)
        kw = (kt_ref[hs, :].astype(F32) * w_k[h:h + 1, :]).astype(BF16)
        v_ext = jnp.concatenate([v_ref[:, h * HEAD_DIM:(h + 1) * HEAD_DIM], ones], axis=1)
        dec = jnp.concatenate([decay[h:h + 1, :], decay[h:h + 1, :]], axis=1)
        ce_s[hs, :] = dec * ce_s[hs, :] + _dot(kw, v_ext)
    m_s[...] = jnp.broadcast_to(m_new, (heads, LANES))

    @pl.when(step == pl.num_programs(0) - 1)
    def _():
        ce_out[...] = ce_s[...]
        m_out[...] = m_s[...]


def _mlstm_prompt(z6, k_t, gates_t, b_gate, g_mlstm, heads, side=None):
    _, s, w = z6.shape
    t = MLSTM_CHUNK
    qw = heads * MLSTM_DK
    n_steps = s // t
    ce0 = jnp.zeros((qw, 2 * HEAD_DIM), F32)
    m0 = jnp.zeros((heads, LANES), F32)
    const = lambda c: (0, 0)
    n_side, side_ops, side_specs, side_out, side_out_spec, side_scratch = 0, [], [], [], [], []
    if side is not None:
        n_side = side[0].shape[0] // n_steps
        assert n_side * n_steps == side[0].shape[0]
        side_ops, side_specs, out_shape, out_spec, side_scratch = _sample_win_attn_operands(*side, n_side)
        side_out, side_out_spec = [out_shape], [out_spec]
    return pl.pallas_call(
        functools.partial(_mlstm_kernel, heads, n_side),
        out_shape=tuple([jax.ShapeDtypeStruct((s, w), BF16),
                         jax.ShapeDtypeStruct((qw, 2 * HEAD_DIM), F32),
                         jax.ShapeDtypeStruct((heads, LANES), F32)] + side_out),
        grid=(n_steps,),
        in_specs=[pl.BlockSpec((None, t, w), lambda c: (3, c, 0)),
                  pl.BlockSpec((qw, t), lambda c: (0, c)),
                  pl.BlockSpec((None, t, w), lambda c: (4, c, 0)),
                  pl.BlockSpec((None, t, w), lambda c: (5, c, 0)),
                  pl.BlockSpec((2 * heads, t), lambda c: (0, c)),
                  pl.BlockSpec((2 * heads, 1), const),
                  pl.BlockSpec((1, w), const),
                  pl.BlockSpec((qw, 2 * HEAD_DIM), const),
                  pl.BlockSpec((heads, LANES), const)] + side_specs,
        out_specs=tuple([pl.BlockSpec((t, w), lambda c: (c, 0)),
                         pl.BlockSpec((qw, 2 * HEAD_DIM), const),
                         pl.BlockSpec((heads, LANES), const)] + side_out_spec),
        scratch_shapes=[pltpu.VMEM((qw, 2 * HEAD_DIM), F32),
                        pltpu.VMEM((heads, LANES), F32),
                        pltpu.VMEM((heads, t, t), BF16),
                        pltpu.VMEM((heads, t, 2 * t), F32)] + side_scratch,
        compiler_params=_params(("arbitrary",)),
        name="mlstm_prompt",
    )(z6, k_t, z6, z6, gates_t, b_gate.reshape(2 * heads, 1), g_mlstm.reshape(1, w), ce0, m0, *side_ops)


def _out_proj_kernel(merge, heads, *refs):
    if merge:
        (o1, o2, o3, l1, l2, l3, mo_ref, x_ref, wa_ref, wm_ref, gpost_ref, gpre_ref, wq_ref,
         x1_ref, qx_ref, att_s) = refs
        la, lb, lc = l1[...], l2[...], l3[...]
        mx = jnp.maximum(jnp.maximum(la, lb), lc)
        ea, eb, ec = jnp.exp(la - mx), jnp.exp(lb - mx), jnp.exp(lc - mx)
        tot = ea + eb + ec
        wa_, wb_, wc_ = ea / tot, eb / tot, ec / tot
        for h in range(heads):
            sl = slice(h * HEAD_DIM, (h + 1) * HEAD_DIM)
            mix = wa_[:, h:h + 1] * o1[h] + wb_[:, h:h + 1] * o2[h] + wc_[:, h:h + 1] * o3[h]
            att_s[:, sl] = mix.astype(BF16)
        att = att_s[...]
    else:
        att_ref, mo_ref, x_ref, wa_ref, wm_ref, gpost_ref, gpre_ref, wq_ref, x1_ref, qx_ref = refs
        att = att_ref[...]
    y = _dot(att, wa_ref[...]) + _dot(mo_ref[...], wm_ref[...])
    x1 = x_ref[...] + _rms(y, gpost_ref[...])
    x1_ref[...] = x1
    u = _rms(x1, gpre_ref[...]).astype(BF16)
    qx_ref[...] = _dot(u, wq_ref[...]).astype(qx_ref.dtype)


def _out_proj(att_parts, m_out, x, w_out, g_post, g_pre, w_xq, tm, heads):
    m, d = x.shape
    wm = m_out.shape[1]
    wa = w_out.shape[0] - wm
    assert wa == wm
    nq = w_xq.shape[1]
    merge = len(att_parts) == 6
    row = lambda i: (i, 0)
    const = lambda i: (0, 0)
    if merge:
        part_specs = ([pl.BlockSpec((heads, tm, HEAD_DIM), lambda i: (0, i, 0))] * 3
                      + [pl.BlockSpec((tm, LANES), row)] * 3)
    else:
        part_specs = [pl.BlockSpec((tm, wa), row)]
    return pl.pallas_call(
        functools.partial(_out_proj_kernel, merge, heads),
        out_shape=(jax.ShapeDtypeStruct((m, d), F32), jax.ShapeDtypeStruct((m, nq), BF16)),
        grid=(m // tm,),
        in_specs=part_specs + [pl.BlockSpec((tm, wm), row),
                               pl.BlockSpec((tm, d), row),
                               pl.BlockSpec((wa, d), const, pipeline_mode=pl.Buffered(1)),
                               pl.BlockSpec((wm, d), lambda i: (1, 0), pipeline_mode=pl.Buffered(1)),
                               pl.BlockSpec((1, d), const),
                               pl.BlockSpec((1, d), const),
                               pl.BlockSpec((d, nq), const, pipeline_mode=pl.Buffered(1))],
        out_specs=(pl.BlockSpec((tm, d), row), pl.BlockSpec((tm, nq), row)),
        scratch_shapes=[pltpu.VMEM((tm, wa), BF16)] if merge else [],
        compiler_params=_params(("parallel",)),
        name="out_proj_merge" if merge else "out_proj",
    )(*att_parts, m_out, x, w_out, w_out, g_post, g_pre, w_xq)


def _xattn_kernel(heads, q_ref, mk_ref, mv_ref, o_ref):
    for h in range(heads):
        sl = slice(h * HEAD_DIM, (h + 1) * HEAD_DIM)
        s = _dot_nt(q_ref[:, sl], mk_ref[:, sl].astype(BF16)) * ATT_SCALE
        m = jnp.max(s, axis=-1, keepdims=True)
        p = jnp.exp(s - m)
        l = jnp.sum(p, axis=-1, keepdims=True)
        o = _dot(p.astype(BF16), mv_ref[:, sl].astype(BF16))
        o_ref[:, sl] = (o / l).astype(o_ref.dtype)


def _xattn(qx, mem_k, mem_v, tm, heads):
    m, w = qx.shape
    n_mem = mem_k.shape[0]
    return pl.pallas_call(
        functools.partial(_xattn_kernel, heads),
        out_shape=jax.ShapeDtypeStruct((m, w), BF16),
        grid=(m // tm,),
        in_specs=[pl.BlockSpec((tm, w), lambda i: (i, 0)),
                  pl.BlockSpec((n_mem, w), lambda i: (0, 0)),
                  pl.BlockSpec((n_mem, w), lambda i: (0, 0))],
        out_specs=pl.BlockSpec((tm, w), lambda i: (i, 0)),
        compiler_params=_params(("parallel",)),
        name="xattn_prompt",
    )(qx, mem_k, mem_v)


def _proj_post_kernel(a_ref, x_ref, w_ref, g_ref, g_next_ref, y_ref, yn_ref):
    y = x_ref[...] + _rms(_dot(a_ref[...], w_ref[...]), g_ref[...])
    y_ref[...] = y
    yn_ref[...] = _rms(y, g_next_ref[...]).astype(BF16)


def _proj_post(a, x, w, g, g_next, tm):
    m, d = x.shape
    k = a.shape[1]
    row = lambda i: (i, 0)
    const = lambda i: (0, 0)
    return pl.pallas_call(
        _proj_post_kernel,
        out_shape=(jax.ShapeDtypeStruct((m, d), F32), jax.ShapeDtypeStruct((m, d), BF16)),
        grid=(m // tm,),
        in_specs=[pl.BlockSpec((tm, k), row),
                  pl.BlockSpec((tm, d), row),
                  pl.BlockSpec((k, d), const),
                  pl.BlockSpec((1, d), const),
                  pl.BlockSpec((1, d), const)],
        out_specs=(pl.BlockSpec((tm, d), row), pl.BlockSpec((tm, d), row)),
        compiler_params=_params(("parallel",)),
        name="proj_post",
    )(a, x, w, g, g_next)


def _mlp_kernel(n_side, x_ref, xn_ref, wu_ref, wd_ref, gpost_ref, *rest):
    if n_side:
        sq_ref, sk_ref, sv_ref, y_ref, so_ref, acc_s = rest
    else:
        y_ref, acc_s = rest
    j = pl.program_id(1)

    def step(first):
        hidden = jnp.square(jnp.maximum(_dot(xn_ref[...], wu_ref[...]), 0.0))
        update = _dot(hidden.astype(BF16), wd_ref[...])
        acc_s[...] = update if first else acc_s[...] + update
        for b in range(n_side):
            so_ref[b] = _single_query_attention(sq_ref[b], [(sk_ref[b], sv_ref[b], None)], None)

    pl.when(j == 0)(functools.partial(step, True))
    pl.when(j != 0)(functools.partial(step, False))

    @pl.when(j == pl.num_programs(1) - 1)
    def _():
        y_ref[...] = x_ref[...] + _rms(acc_s[...], gpost_ref[...])


def _mlp(x, xn, w_up, w_down, g_post, tm, tf, side=None):
    m, d = x.shape
    ff = w_up.shape[1]
    steps_j = ff // tf
    n_steps = (m // tm) * steps_j
    in_specs = [pl.BlockSpec((tm, d), lambda i, j: (i, 0)),
                pl.BlockSpec((tm, d), lambda i, j: (i, 0)),
                pl.BlockSpec((d, tf), lambda i, j: (0, j)),
                pl.BlockSpec((tf, d), lambda i, j: (j, 0)),
                pl.BlockSpec((1, d), lambda i, j: (0, 0))]
    out_shape = [jax.ShapeDtypeStruct((m, d), F32)]
    out_specs = [pl.BlockSpec((tm, d), lambda i, j: (i, 0))]
    n_side = 0
    if side is not None:
        sq, sk, sv = side
        nb, heads, hd = sq.shape
        n_side = nb // n_steps
        assert n_side * n_steps == nb
        tok = pl.BlockSpec((n_side, heads, hd), lambda i, j: (i * steps_j + j, 0, 0))
        mem = pl.BlockSpec((n_side,) + sk.shape[1:], lambda i, j: (i * steps_j + j, 0, 0, 0))
        in_specs += [tok, mem, mem]
        out_shape.append(jax.ShapeDtypeStruct((nb, heads, hd), F32))
        out_specs.append(tok)
    out = pl.pallas_call(
        functools.partial(_mlp_kernel, n_side),
        out_shape=tuple(out_shape),
        grid=(m // tm, steps_j),
        in_specs=in_specs,
        out_specs=tuple(out_specs),
        scratch_shapes=[pltpu.VMEM((tm, d), F32)],
        compiler_params=_params(("parallel", "arbitrary")),
        name="mlp_side" if side is not None else "mlp",
    )(x, xn, w_up, w_down, g_post, *(side or ()))
    return out if side is not None else out[0]


def _single_query_attention(q, groups, new):
    q = q * ATT_SCALE
    scores = []
    for k3, _, b3 in groups:
        s = jnp.sum(k3 * q[None], axis=-1, keepdims=True)
        scores.append(s if b3 is None else s + b3)
    mx = jnp.max(scores[0], axis=0)
    for s in scores[1:]:
        mx = jnp.maximum(mx, jnp.max(s, axis=0))
    if new is not None:
        k_new, v_new, b_new, count = new
        s_new = jnp.sum(k_new * q, axis=-1, keepdims=True) + b_new
        mx = jnp.maximum(mx, s_new)
    den = jnp.zeros(mx.shape, F32)
    acc = jnp.zeros(q.shape, F32)
    for s, (_, v3, _) in zip(scores, groups):
        p = jnp.exp(s - mx[None])
        den = den + jnp.sum(p, axis=0)
        acc = acc + jnp.sum(p * v3, axis=0)
    if new is not None:
        p_new = jnp.exp(s_new - mx) * count
        den = den + p_new
        acc = acc + p_new * v_new
    return acc / den


N_WIN_ATTN_IN = 5 + 2 * len(DILATIONS)


def _sample_win_attn_init(rel_ref, bkt_ref, bias_s):
    for p in range(len(DILATIONS)):
        bkt = bkt_ref[p]
        acc = jnp.zeros(bkt.shape, F32)
        for k in range(REL_BUCKETS):
            acc = jnp.where(bkt == k, rel_ref[k][None], acc)
        bias_s[p] = acc


def _sample_win_attn_compute(tb, rel_ref, q_ref, kn_ref, vn_ref, k1, k4, k16, v1, v4, v16, o_ref, bias_s):
    for b in range(tb):
        groups = [(k[b], v[b], bias_s[p]) for p, (k, v) in enumerate(((k1, v1), (k4, v4), (k16, v16)))]
        new = (kn_ref[b], vn_ref[b], rel_ref[0], float(len(DILATIONS)))
        o_ref[b] = _single_query_attention(q_ref[b], groups, new)


def _sample_win_attn_operands(q, k_new, v_new, cache_k, cache_v, rel_bias, tb):
    nb, heads, hd = q.shape
    buf = cache_k.shape[1]
    r = SUB_WINDOW
    assert buf == r * DILATIONS[-1], "window buffer must hold exactly the widest dilated pattern"
    j = r - np.arange(r)
    bkt = np.stack([_rel_bucket(j * d) for d in DILATIONS]).astype(np.int32)
    bkt = np.broadcast_to(bkt[:, :, None, None], (len(DILATIONS), r, heads, 1))
    views, specs = [], []
    for cache in (cache_k, cache_v):
        for d in DILATIONS:
            views.append(cache.reshape(nb, buf // d, d, heads, hd))
            specs.append(pl.BlockSpec((tb, r, None, heads, hd), functools.partial(
                lambda i, blk: (i, blk, 0, 0, 0), blk=buf // d // r - 1)))
    tok = pl.BlockSpec((tb, heads, hd), lambda i: (i, 0, 0))
    operands = [rel_bias.reshape(REL_BUCKETS, heads, 1), jnp.asarray(bkt), q, k_new, v_new] + views
    in_specs = [pl.BlockSpec((REL_BUCKETS, heads, 1), lambda i: (0, 0, 0)),
                pl.BlockSpec((len(DILATIONS), r, heads, 1), lambda i: (0, 0, 0, 0)),
                tok, tok, tok] + specs
    scratch = [pltpu.VMEM((len(DILATIONS), r, heads, 1), F32)]
    return operands, in_specs, jax.ShapeDtypeStruct((nb, heads, hd), F32), tok, scratch


def _sample_mlstm_kernel(heads, tb, q_ref, k_ref, v_ref, mo_ref, li_ref, lf_ref, bi_ref, bf_ref,
                         gm_ref, c_ref, n_ref, m_ref, out_ref, c_out, n_out, m_out):
    qw = heads * MLSTM_DK
    hrow = lax.broadcasted_iota(jnp.int32, (LANES, qw), 0)
    hcol = lax.broadcasted_iota(jnp.int32, (LANES, qw), 1) // MLSTM_DK
    diag = hrow == hcol
    rows8 = slice(0, heads)
    for b in range(tb):
        q_bd = jnp.where(diag, q_ref[b:b + 1, :], 0.0) * QK_SCALE
        k_bd = jnp.where(diag, k_ref[b:b + 1, :], 0.0)
        c_b = c_ref[b]
        n_row = n_ref[b:b + 1, :]
        li = li_ref[b] + bi_ref[...]
        lf = _log_sigmoid(lf_ref[b] + bf_ref[...])
        m_prev = m_ref[b]
        qk = jnp.sum(q_bd * k_ref[b:b + 1, :], axis=-1, keepdims=True)[rows8]
        qn = jnp.sum(q_bd * n_row, axis=-1, keepdims=True)[rows8]
        q_c = _dot(q_bd.astype(BF16), c_b.astype(BF16))[rows8]
        inter = lf + m_prev
        m_t = jnp.maximum(inter, li)
        w_intra = jnp.exp(li - m_t)
        w_prev = jnp.exp(inter - m_t)
        sqk = qk * w_intra
        v_b = v_ref[b]
        num = sqk * v_b + w_prev * q_c
        den = sqk + w_prev * qn
        hh = num / jnp.maximum(jnp.abs(den), jnp.exp(-m_t))
        hn = hh * lax.rsqrt(jnp.mean(hh * hh, axis=-1, keepdims=True) + EPS)
        out_ref[b] = hn * gm_ref[...] * jax.nn.sigmoid(mo_ref[b])
        wv = jnp.concatenate([w_intra * v_b, jnp.zeros((LANES - heads, HEAD_DIM), F32)], axis=0)
        d_c = _dot_tn(k_bd.astype(BF16), wv.astype(BF16))
        for h in range(heads):
            hs = slice(h * MLSTM_DK, (h + 1) * MLSTM_DK)
            c_out[b, hs, :] = w_prev[h:h + 1, :] * c_b[hs, :] + d_c[hs, :]
        k_w = jnp.sum(k_bd[rows8] * w_intra, axis=0, keepdims=True)
        dec_row = jnp.sum(jnp.where(diag[rows8], w_prev, 0.0), axis=0, keepdims=True)
        n_out[b:b + 1, :] = dec_row * n_row + k_w
        m_out[b] = m_t


def _sample_mlstm(z6, gates, b_gate, g_mlstm, state_c, state_n, state_m, heads, tb):
    _, nb, w = z6.shape
    qw = heads * MLSTM_DK
    q = z6[3, :, :qw].astype(F32)
    k = z6[3, :, qw:].astype(F32)
    v = z6[4].astype(F32).reshape(nb, heads, HEAD_DIM)
    mo = z6[5].astype(F32).reshape(nb, heads, HEAD_DIM)
    li = gates[:, :heads].reshape(nb, heads, 1)
    lf = gates[:, heads:2 * heads].reshape(nb, heads, 1)
    row = lambda i: (i, 0)
    blk3 = lambda i: (i, 0, 0)
    const = lambda i: (0, 0)
    return pl.pallas_call(
        functools.partial(_sample_mlstm_kernel, heads, tb),
        out_shape=(jax.ShapeDtypeStruct((nb, heads, HEAD_DIM), F32),
                   jax.ShapeDtypeStruct((nb, qw, HEAD_DIM), F32),
                   jax.ShapeDtypeStruct((nb, qw), F32),
                   jax.ShapeDtypeStruct((nb, heads, 1), F32)),
        grid=(nb // tb,),
        in_specs=[pl.BlockSpec((tb, qw), row),
                  pl.BlockSpec((tb, qw), row),
                  pl.BlockSpec((tb, heads, HEAD_DIM), blk3),
                  pl.BlockSpec((tb, heads, HEAD_DIM), blk3),
                  pl.BlockSpec((tb, heads, 1), blk3),
                  pl.BlockSpec((tb, heads, 1), blk3),
                  pl.BlockSpec((heads, 1), const),
                  pl.BlockSpec((heads, 1), const),
                  pl.BlockSpec((heads, HEAD_DIM), const),
                  pl.BlockSpec((tb, qw, HEAD_DIM), blk3),
                  pl.BlockSpec((tb, qw), row),
                  pl.BlockSpec((tb, heads, 1), blk3)],
        out_specs=(pl.BlockSpec((tb, heads, HEAD_DIM), blk3),
                   pl.BlockSpec((tb, qw, HEAD_DIM), blk3),
                   pl.BlockSpec((tb, qw), row),
                   pl.BlockSpec((tb, heads, 1), blk3)),
        compiler_params=_params(("parallel",)),
        name="sample_mlstm",
    )(q, k, v, mo, li, lf, b_gate[:heads].reshape(heads, 1), b_gate[heads:].reshape(heads, 1),
      g_mlstm.reshape(heads, HEAD_DIM), state_c.reshape(nb, qw, HEAD_DIM), state_n.reshape(nb, qw),
      state_m.reshape(nb, heads, 1))


ROW_TILE = 512
MLP_FF_TILE = 1024
SAMPLE_STATE_TILE = 8


def _tile(m, cap):
    return min(m, cap)


def kernel(x_prompt, x_sample, mem_prompt, cache_win_k, cache_win_v, state_mlstm_C, state_mlstm_n,
           state_mlstm_m, cache_mem_k, cache_mem_v, rel_bias, g_pre_mix, g_post_mix, w_in, b_gate,
           g_mlstm, w_out, g_pre_xatt, g_post_xatt, g_mem, w_xq, w_xkv, w_xo, g_pre_mlp, g_post_mlp,
           w_up, w_down):
    depth = w_in.shape[0]
    assert depth == 1 and x_prompt.shape[0] == 1 and x_sample.shape[1] == 1
    _, seq, d_model = x_prompt.shape
    n_dec = x_sample.shape[0]
    att_heads = cache_win_k.shape[3]
    ml_heads = state_mlstm_C.shape[2]
    x_heads = cache_mem_k.shape[3]
    att_w = att_heads * HEAD_DIM
    ml_w = ml_heads * HEAD_DIM
    x_w = x_heads * HEAD_DIM
    n_mem = mem_prompt.shape[1]
    keep = cache_win_k.shape[2]
    n_main = 3 * att_w + 2 * ml_heads * MLSTM_DK + 2 * ml_w
    assert att_w == ml_w == 1024 and n_main % 1024 == 0 and seq >= keep

    def row(g):
        return g[0].reshape(1, -1)

    assert n_main == 6 * 1024
    w_main = w_in[0, :, :n_main].astype(BF16)
    wg = w_in[0, :, n_main:].astype(BF16)
    w_out_b = w_out[0].astype(BF16)
    wq = w_xq[0].astype(BF16)
    wkv = w_xkv[0].astype(BF16)
    wo = w_xo[0].astype(BF16)
    wu = w_up[0].astype(BF16)
    wd = w_down[0].astype(BF16)
    rel_flat = rel_bias.reshape(-1)

    tm = _tile(seq, ROW_TILE)

    def mlp(x2, x2n, rows, side=None):
        return _mlp(x2, x2n, wu, wd, row(g_post_mlp), rows, MLP_FF_TILE, side)

    def mem_out_proj(o, x1, rows):
        return _proj_post(o, x1, wo, row(g_post_xatt), row(g_pre_mlp), rows)

    xs = x_sample[:, 0]
    z6s, kv32s, gates_s, _ = _in_proj(xs, row(g_pre_mix), w_main, wg, n_dec, False, n_dec)
    per_head = (n_dec, att_heads, HEAD_DIM)
    win_attn_args = (z6s[0].astype(F32).reshape(per_head), kv32s[0].reshape(per_head),
                     kv32s[1].reshape(per_head), cache_win_k[0], cache_win_v[0], rel_bias)

    xp = x_prompt[0]
    mem_kv = _norm_matmul(mem_prompt[0], row(g_mem), wkv)
    mk_p, mv_p = mem_kv[:, :x_w], mem_kv[:, x_w:]
    z6, kv32, _, gates_t, k_t, *zd = _in_proj(xp, row(g_pre_mix), w_main, wg, tm, True, keep)
    zd = [z6.reshape(z6.shape[0], 1, seq, att_w)] + zd
    parts = [_dil_attn(z, rel_flat, d, att_heads) for z, d in zip(zd, DILATIONS)]
    m_out, ce_p, m_p, att_s = _mlstm_prompt(z6, k_t, gates_t, b_gate[0], g_mlstm[0], ml_heads, win_attn_args)
    c_p, n_p = ce_p[:, :HEAD_DIM], ce_p[:, HEAD_DIM]
    x1, qx = _out_proj(tuple(p[0] for p in parts) + tuple(p[1] for p in parts), m_out, xp, w_out_b,
                       row(g_post_mix), row(g_pre_xatt), wq, tm, att_heads)
    x2, x2n = mem_out_proj(_xattn(qx, mk_p, mv_p, tm, x_heads), x1, tm)

    mo_s, c_s, n_s, m_s = _sample_mlstm(z6s, gates_s, b_gate[0], g_mlstm[0], state_mlstm_C[0],
                                        state_mlstm_n[0], state_mlstm_m[0], ml_heads, SAMPLE_STATE_TILE)
    x1s, qxs = _out_proj((att_s.reshape(n_dec, att_w).astype(BF16),), mo_s.reshape(n_dec, ml_w).astype(BF16),
                         xs, w_out_b, row(g_post_mix), row(g_pre_xatt), wq, n_dec, att_heads)

    q3 = qxs.astype(F32).reshape(n_dec, x_heads, HEAD_DIM)
    y_prompt, o_mem = mlp(x2, x2n, tm, (q3, cache_mem_k[0], cache_mem_v[0]))
    x2s, x2sn = mem_out_proj(o_mem.reshape(n_dec, x_w).astype(BF16), x1s, n_dec)
    y_sample = mlp(x2s, x2sn, n_dec)

    dk = MLSTM_DK
    return (y_prompt[None],
            y_sample[:, None],
            kv32[0].reshape(1, 1, keep, att_heads, HEAD_DIM),
            kv32[1].reshape(1, 1, keep, att_heads, HEAD_DIM),
            c_p.reshape(1, 1, ml_heads, dk, HEAD_DIM),
            n_p.reshape(1, 1, ml_heads, dk),
            m_p[:, 0].reshape(1, 1, ml_heads),
            mk_p.reshape(1, 1, n_mem, x_heads, HEAD_DIM),
            mv_p.reshape(1, 1, n_mem, x_heads, HEAD_DIM),
            kv32s[0].reshape(1, n_dec, 1, att_heads, HEAD_DIM),
            kv32s[1].reshape(1, n_dec, 1, att_heads, HEAD_DIM),
            c_s.reshape(1, n_dec, ml_heads, dk, HEAD_DIM),
            n_s.reshape(1, n_dec, ml_heads, dk),
            m_s.reshape(1, n_dec, ml_heads))
```

```python
import functools
import math

import numpy as np
import jax
import jax.numpy as jnp
from jax import lax
from jax.experimental import pallas as pl
from jax.experimental.pallas import tpu as pltpu

F32 = jnp.float32
BF16 = jnp.bfloat16

EPS = 1e-6
HEAD_DIM = 128
MLSTM_DK = 64
LANES = 128
DILATIONS = (1, 4, 16)
SUB_WINDOW = 128
ATT_BLOCK = 128
MLSTM_CHUNK = 128
REL_BUCKETS = 32
REL_MAX_DIST = 2048
NEG = -1e30
ATT_SCALE = HEAD_DIM ** -0.5
QK_SCALE = MLSTM_DK ** -0.5
VMEM_LIMIT = 56 * 2 ** 20


def _params(semantics):
    return pltpu.CompilerParams(dimension_semantics=semantics, vmem_limit_bytes=VMEM_LIMIT)


def _rms(xf, g):
    ms = jnp.mean(xf * xf, axis=-1, keepdims=True)
    return xf * lax.rsqrt(ms + EPS) * g


def _dot(a, b):
    return jnp.dot(a, b, preferred_element_type=F32)


def _dot_nt(a, b):
    return lax.dot_general(a, b, (((1,), (1,)), ((), ())), preferred_element_type=F32)


def _dot_tn(a, b):
    return lax.dot_general(a, b, (((0,), (0,)), ((), ())), preferred_element_type=F32)


def _log_sigmoid(x):
    return jnp.minimum(x, 0.0) - jnp.log(1.0 + jnp.exp(-jnp.abs(x)))


def _rel_bucket(dist):
    dist = np.asarray(dist)
    exact = REL_BUCKETS // 2
    far = exact + (np.log(np.maximum(dist, exact) / exact) / math.log(REL_MAX_DIST / exact)
                   * (REL_BUCKETS - exact)).astype(np.int32)
    return np.where(dist < exact, dist, np.minimum(far, REL_BUCKETS - 1)).astype(np.int32)


def _norm_matmul_kernel(x_ref, g_ref, w_ref, o_ref):
    xn = _rms(x_ref[...], g_ref[...]).astype(BF16)
    o_ref[...] = _dot(xn, w_ref[...])


def _norm_matmul(x, g, w):
    m, d = x.shape
    n = w.shape[1]
    return pl.pallas_call(
        _norm_matmul_kernel,
        out_shape=jax.ShapeDtypeStruct((m, n), F32),
        grid=(1,),
        in_specs=[pl.BlockSpec((m, d), lambda i: (0, 0)),
                  pl.BlockSpec((1, d), lambda i: (0, 0)),
                  pl.BlockSpec((d, n), lambda i: (0, 0))],
        out_specs=pl.BlockSpec((m, n), lambda i: (0, 0)),
        compiler_params=_params(("arbitrary",)),
        name="norm_matmul",
    )(x, g, w)


def _column_order(j):
    return (j % 2) * 3 + j // 2


def _in_proj_kernel(prompt, x_ref, g_ref, w_ref, wg_ref, *rest):
    if prompt:
        z_ref, kv_ref, gate_ref, gate_t_ref, kt_ref, zd_mid, zd_wide, xn_ref, slab, slab_mid = rest
    else:
        z_ref, kv_ref, gate_ref, gate_t_ref, xn_ref = rest
    j = pl.program_id(1)

    @pl.when(j == 0)
    def _():
        xn = _rms(x_ref[...], g_ref[...]).astype(BF16)
        xn_ref[...] = xn
        gates = _dot(xn, wg_ref[...])
        gate_ref[...] = gates
        gate_t_ref[...] = gates.T[:gate_t_ref.shape[0]]

    tm, tn = z_ref.shape
    n_steps = 6

    def project(c):
        acc = _dot(xn_ref[...], w_ref[:, c * tn:(c + 1) * tn])
        z_ref[...] = acc.astype(BF16)
        return acc

    if not prompt:
        for step in range(n_steps):
            @pl.when(j == step)
            def _(step=step):
                acc = project(step)
                if step in (1, 2):
                    kv_ref[...] = acc
        return

    mid = DILATIONS[1]
    ratio = DILATIONS[2] // mid

    def even_step(step):
        acc = project(_column_order(step))
        for h in range(tn // LANES):
            slab[h] = acc[:, h * LANES:(h + 1) * LANES]
        if step >= 2:
            kv_ref[...] = acc

    def odd_step(step):
        acc = project(_column_order(step))
        for h in range(tn // LANES):
            hl = slice(h * LANES, (h + 1) * LANES)
            for r in range(mid):
                rows = slab[h, pl.ds(r, tm // mid, stride=mid), :]
                zd_mid[r, :, hl] = rows.astype(BF16)
                slab_mid[h, r] = rows
            for r in range(mid):
                for q in range(ratio):
                    rows = slab_mid[h, r, pl.ds(q, tm // (mid * ratio), stride=ratio), :]
                    zd_wide[r + mid * q, :, hl] = rows.astype(BF16)
        if step == 1:
            kt_ref[...] = acc[:, tn // 2:].T.astype(BF16)

    for step in range(n_steps):
        pl.when(j == step)(functools.partial(odd_step if step % 2 else even_step, step))


def _in_proj(x, g, w_all, wg, tm, prompt, kv_rows):
    m, d = x.shape
    tn = 1024
    nblk = 6
    n_gate = wg.shape[1]
    wg_pad = jnp.zeros((d, LANES), BF16).at[:, :n_gate].set(wg)
    dils = DILATIONS[1:] if prompt else ()
    assert DILATIONS[2] % DILATIONS[1] == 0 and nblk == 6 and (m - kv_rows) % tm == 0
    const = lambda i, j: (0, 0)
    skip = (m - kv_rows) // tm
    if prompt:
        col = _column_order
        kv_c = lambda j: jnp.where(j < 4, 0, 1)
    else:
        col = lambda j: j
        kv_c = lambda j: jnp.clip(j - 1, 0, 1)
    kv_blk = lambda i, j: (jnp.where(i >= skip, kv_c(j), 0), jnp.maximum(i - skip, 0), 0)
    extra_out, extra_out_specs, extra_scratch = [], [], []
    if prompt:
        extra_out = [jax.ShapeDtypeStruct((tn // 2, m), BF16)] + [
            jax.ShapeDtypeStruct((3, dl, m // dl, tn), BF16) for dl in dils]
        extra_out_specs = [pl.BlockSpec((tn // 2, tm), lambda i, j: (0, i))] + [
            pl.BlockSpec((None, dl, tm // dl, tn), lambda i, j: (j // 2, 0, i, 0)) for dl in dils]
        extra_scratch = [pltpu.VMEM((tn // LANES, tm, LANES), F32),
                         pltpu.VMEM((tn // LANES, DILATIONS[1], tm // DILATIONS[1], LANES), F32)]
    return pl.pallas_call(
        functools.partial(_in_proj_kernel, prompt),
        out_shape=tuple([jax.ShapeDtypeStruct((nblk, m, tn), BF16),
                         jax.ShapeDtypeStruct((2, kv_rows, tn), F32),
                         jax.ShapeDtypeStruct((m, LANES), F32),
                         jax.ShapeDtypeStruct((n_gate, m), F32)] + extra_out),
        grid=(m // tm, nblk),
        in_specs=[pl.BlockSpec((tm, d), lambda i, j: (i, 0)),
                  pl.BlockSpec((1, d), const),
                  pl.BlockSpec(w_all.shape, const, pipeline_mode=pl.Buffered(1)),
                  pl.BlockSpec((d, LANES), const)],
        out_specs=tuple([pl.BlockSpec((None, tm, tn), lambda i, j: (col(j), i, 0)),
                         pl.BlockSpec((None, tm, tn), kv_blk),
                         pl.BlockSpec((tm, LANES), lambda i, j: (i, 0)),
                         pl.BlockSpec((n_gate, tm), lambda i, j: (0, i))] + extra_out_specs),
        scratch_shapes=[pltpu.VMEM((tm, d), BF16)] + extra_scratch,
        compiler_params=_params(("parallel", "arbitrary")),
        name="in_proj_dilated" if prompt else "in_proj",
    )(x, g, w_all, wg_pad)


def _dil_attn_kernel(heads, d, nq, rb_ref, bkt_ref, q_ref, kc_ref, kp_ref, vc_ref, vp_ref,
                     o_ref, lse_ref, bias, s_scr, p_scr):
    n = pl.program_id(0)
    r = pl.program_id(1)

    @pl.when((n == 0) & (r == 0))
    def _():
        bkt = bkt_ref[...]
        for h in range(heads):
            acc = jnp.full(bkt.shape, NEG, F32)
            for k in range(REL_BUCKETS):
                acc = jnp.where(bkt == k, rb_ref[k * heads + h], acc)
            bias[h] = acc

    blk = ATT_BLOCK
    in_prev = lax.broadcasted_iota(jnp.int32, (blk, 2 * blk), 1) < blk
    for u in range(nq):
        for h in range(heads):
            sl = slice(h * HEAD_DIM, (h + 1) * HEAD_DIM)
            qh = q_ref[u * blk:(u + 1) * blk, sl]
            if u == 0:
                keys = jnp.concatenate([kp_ref[:, sl], kc_ref[:blk, sl]], axis=0)
            else:
                keys = kc_ref[(u - 1) * blk:(u + 1) * blk, sl]
            s = _dot_nt(qh, keys) * ATT_SCALE + bias[h]
            if u == 0:
                s = jnp.where((n == 0) & in_prev, NEG, s)
            s_scr[u * heads + h] = s
    lane = lax.broadcasted_iota(jnp.int32, (blk, LANES), 1)
    m_tiles = []
    for u in range(nq):
        m_tile = jnp.zeros((blk, LANES), F32)
        for h in range(heads):
            s = s_scr[u * heads + h]
            m = jnp.max(s, axis=-1, keepdims=True)
            p_scr[u * heads + h] = jnp.exp(s - m).astype(BF16)
            m_tile = jnp.where(lane == h, m, m_tile)
        m_tiles.append(m_tile)
    ones = jnp.ones((2 * blk, HEAD_DIM), BF16)
    for u in range(nq):
        rows = pl.ds(r + u * blk * d, blk, stride=d)
        lse_tile = m_tiles[u]
        for h in range(heads):
            sl = slice(h * HEAD_DIM, (h + 1) * HEAD_DIM)
            if u == 0:
                vals = jnp.concatenate([vp_ref[:, sl], vc_ref[:blk, sl]], axis=0)
            else:
                vals = vc_ref[(u - 1) * blk:(u + 1) * blk, sl]
            o = _dot(p_scr[u * heads + h], jnp.concatenate([vals, ones], axis=1))
            l = o[:, HEAD_DIM:]
            o_ref[h, rows, :] = o[:, :HEAD_DIM] / l
            lse_tile = jnp.where(lane == h, lse_tile + jnp.log(l), lse_tile)
        lse_ref[rows, :] = lse_tile


DIL_ATTN_OUT_BLOCK_BYTES = 16 * 2 ** 20


def _dil_attn(zd, rel_flat, dilation, heads):
    d = dilation
    sub_len, w = zd.shape[2], zd.shape[3]
    s = sub_len * d
    blk = ATT_BLOCK
    nq = 2 if 2 * blk * d * w * 4 <= DIL_ATTN_OUT_BLOCK_BYTES else 1
    nb = sub_len // (nq * blk)
    qi = np.arange(blk)[:, None]
    ki = np.arange(2 * blk)[None, :]
    dist = blk + qi - ki
    bkt = np.where((dist >= 0) & (dist <= SUB_WINDOW), _rel_bucket(np.maximum(dist, 0) * d), -1).astype(np.int32)

    def cur(c):
        return pl.BlockSpec((None, None, nq * blk, w), lambda n, r: (c, r, n, 0))

    def prev(c):
        return pl.BlockSpec((None, None, blk, w), lambda n, r: (c, r, jnp.maximum(nq * n - 1, 0), 0))

    return pl.pallas_call(
        functools.partial(_dil_attn_kernel, heads, d, nq),
        out_shape=(jax.ShapeDtypeStruct((heads, s, HEAD_DIM), F32),
                   jax.ShapeDtypeStruct((s, LANES), F32)),
        grid=(nb, d),
        in_specs=[pl.BlockSpec(memory_space=pltpu.SMEM),
                  pl.BlockSpec((blk, 2 * blk), lambda n, r: (0, 0)),
                  cur(0), cur(1), prev(1), cur(2), prev(2)],
        out_specs=(pl.BlockSpec((heads, nq * blk * d, HEAD_DIM), lambda n, r: (0, n, 0)),
                   pl.BlockSpec((nq * blk * d, LANES), lambda n, r: (n, 0))),
        scratch_shapes=[pltpu.VMEM((heads, blk, 2 * blk), F32),
                        pltpu.VMEM((nq * heads, blk, 2 * blk), F32),
                        pltpu.VMEM((nq * heads, blk, 2 * blk), BF16)],
        compiler_params=_params(("arbitrary", "arbitrary")),
        name=f"dil_attn_d{d}",
    )(rel_flat, jnp.asarray(bkt), zd, zd, zd, zd, zd)


def _split_bf16(x):
    hi = x.astype(BF16)
    return hi, (x - hi.astype(F32)).astype(BF16)


def _mlstm_kernel(heads, n_side, qk_ref, kt_ref, v_ref, mo_ref, gt_ref, bgc_ref, gm_ref,
                  ce0_ref, m0_ref, *rest):
    step = pl.program_id(0)
    t = MLSTM_CHUNK
    qw = heads * MLSTM_DK
    dk = MLSTM_DK
    if n_side:
        side_in, rest = rest[:N_WIN_ATTN_IN], rest[N_WIN_ATTN_IN:]
        out_ref, ce_out, m_out, side_out, ce_s, m_s, sqk_s, tile_s, side_bias = rest
    else:
        out_ref, ce_out, m_out, ce_s, m_s, sqk_s, tile_s = rest

    @pl.when(step == 0)
    def _():
        ce_s[...] = ce0_ref[...]
        m_s[...] = m0_ref[...]
        if n_side:
            _sample_win_attn_init(side_in[0], side_in[1], side_bias)

    if n_side:
        _sample_win_attn_compute(n_side, side_in[0], *side_in[2:], side_out, side_bias)

    row = lax.broadcasted_iota(jnp.int32, (t, t), 0)
    col = lax.broadcasted_iota(jnp.int32, (t, t), 1)
    causal = col <= row
    eye = col == row
    upper = jnp.where(row <= col, 1.0, 0.0).astype(BF16)
    lane = lax.broadcasted_iota(jnp.int32, (1, LANES), 1)
    lane_h = lax.broadcasted_iota(jnp.int32, (heads, t), 1)
    blk_r = lax.broadcasted_iota(jnp.int32, (2 * t, 2 * t), 0) // t
    blk_c = lax.broadcasted_iota(jnp.int32, (2 * t, 2 * t), 1) // t
    ones_bd = jnp.where(blk_r == blk_c, 1.0, 0.0).astype(BF16)
    ones = jnp.ones((t, HEAD_DIM), BF16)

    gt = gt_ref[...] + bgc_ref[...]
    hi, lo = _split_bf16(_log_sigmoid(gt[heads:]))
    b2 = _dot(jnp.concatenate([hi, lo], axis=0), upper)
    b = b2[:heads] + b2[heads:]
    c = gt[:heads] - b
    m_prev = m_s[...]
    cmax = c
    for shift in (1, 2, 4, 8, 16, 32, 64):
        cmax = jnp.maximum(cmax, jnp.where(lane_h >= shift, pltpu.roll(cmax, shift, axis=1), NEG))
    a = jnp.maximum(cmax, m_prev)
    m_t = b + a
    a_last = a[:, t - 1:t]
    b_last = b[:, t - 1:t]
    m_new = b_last + a_last
    w_k = jnp.exp(c - a_last)
    decay = jnp.exp(b_last + m_prev - m_new)
    a_hi, a_lo = _split_bf16(a)
    mt_hi, mt_lo = _split_bf16(m_t)

    def diag(x_row):
        return jnp.where(eye, x_row.astype(F32), 0.0).astype(BF16)

    qms, dens = [], []
    for h in range(heads):
        pair, half = divmod(h, 2)
        hs = slice(h, h + 1)
        tile = (_dot(jnp.concatenate([diag(a_hi[hs]), diag(mt_hi[hs])], axis=1), ones_bd)
                + _dot(jnp.concatenate([diag(a_lo[hs]), diag(mt_lo[hs])], axis=1), ones_bd))
        tile_s[h] = tile
        w_intra = jnp.exp(jnp.where(causal, c[hs] - tile[:, :t], NEG))
        mine = (lane >= half * dk) & (lane < (half + 1) * dk)
        qm = jnp.where(mine, qk_ref[:, pair * LANES:(pair + 1) * LANES], 0.0) * QK_SCALE
        km = jnp.where(mine, qk_ref[:, qw + pair * LANES:qw + (pair + 1) * LANES], 0.0)
        sqk = _dot_nt(qm, km) * w_intra
        sqk_s[h] = sqk.astype(BF16)
        dens.append(jnp.sum(sqk, axis=-1, keepdims=True))
        qms.append(qm)

    for h in range(heads):
        pair = h // 2
        sl = slice(h * HEAD_DIM, (h + 1) * HEAD_DIM)
        tile = tile_s[h]
        w_prev = jnp.exp(m_prev[h:h + 1, :] - tile[:, :t])
        q_ce = _dot(qms[h], ce_s[pair * LANES:(pair + 1) * LANES, :].astype(BF16))
        num = _dot(sqk_s[h], v_ref[:, sl]) + w_prev * q_ce[:, :HEAD_DIM]
        den = dens[h] + w_prev * q_ce[:, HEAD_DIM:]
        hh = num / jnp.maximum(jnp.abs(den), jnp.exp(-tile[:, t:]))
        hn = hh * lax.rsqrt(jnp.mean(hh * hh, axis=-1, keepdims=True) + EPS)
        gate = jax.nn.sigmoid(mo_ref[:, sl].astype(F32))
        out_ref[:, sl] = (hn * gm_ref[:, sl] * gate).astype(out_ref.dtype)

    for h in range(heads):
        hs = slice(h * dk, (h + 1) * dk)
        kw = (kt_ref[hs, :].astype(F32) * w_k[h:h + 1, :]).astype(BF16)
        v_ext = jnp.concatenate([v_ref[:, h * HEAD_DIM:(h + 1) * HEAD_DIM], ones], axis=1)
        dec = jnp.concatenate([decay[h:h + 1, :], decay[h:h + 1, :]], axis=1)
        ce_s[hs, :] = dec * ce_s[hs, :] + _dot(kw, v_ext)
    m_s[...] = jnp.broadcast_to(m_new, (heads, LANES))

    @pl.when(step == pl.num_programs(0) - 1)
    def _():
        ce_out[...] = ce_s[...]
        m_out[...] = m_s[...]


def _mlstm_prompt(z6, k_t, gates_t, b_gate, g_mlstm, heads, side=None):
    _, s, w = z6.shape
    t = MLSTM_CHUNK
    qw = heads * MLSTM_DK
    n_steps = s // t
    ce0 = jnp.zeros((qw, 2 * HEAD_DIM), F32)
    m0 = jnp.zeros((heads, LANES), F32)
    const = lambda c: (0, 0)
    n_side, side_ops, side_specs, side_out, side_out_spec, side_scratch = 0, [], [], [], [], []
    if side is not None:
        n_side = side[0].shape[0] // n_steps
        assert n_side * n_steps == side[0].shape[0]
        side_ops, side_specs, out_shape, out_spec, side_scratch = _sample_win_attn_operands(*side, n_side)
        side_out, side_out_spec = [out_shape], [out_spec]
    return pl.pallas_call(
        functools.partial(_mlstm_kernel, heads, n_side),
        out_shape=tuple([jax.ShapeDtypeStruct((s, w), BF16),
                         jax.ShapeDtypeStruct((qw, 2 * HEAD_DIM), F32),
                         jax.ShapeDtypeStruct((heads, LANES), F32)] + side_out),
        grid=(n_steps,),
        in_specs=[pl.BlockSpec((None, t, w), lambda c: (3, c, 0)),
                  pl.BlockSpec((qw, t), lambda c: (0, c)),
                  pl.BlockSpec((None, t, w), lambda c: (4, c, 0)),
                  pl.BlockSpec((None, t, w), lambda c: (5, c, 0)),
                  pl.BlockSpec((2 * heads, t), lambda c: (0, c)),
                  pl.BlockSpec((2 * heads, 1), const),
                  pl.BlockSpec((1, w), const),
                  pl.BlockSpec((qw, 2 * HEAD_DIM), const),
                  pl.BlockSpec((heads, LANES), const)] + side_specs,
        out_specs=tuple([pl.BlockSpec((t, w), lambda c: (c, 0)),
                         pl.BlockSpec((qw, 2 * HEAD_DIM), const),
                         pl.BlockSpec((heads, LANES), const)] + side_out_spec),
        scratch_shapes=[pltpu.VMEM((qw, 2 * HEAD_DIM), F32),
                        pltpu.VMEM((heads, LANES), F32),
                        pltpu.VMEM((heads, t, t), BF16),
                        pltpu.VMEM((heads, t, 2 * t), F32)] + side_scratch,
        compiler_params=_params(("arbitrary",)),
        name="mlstm_prompt",
    )(z6, k_t, z6, z6, gates_t, b_gate.reshape(2 * heads, 1), g_mlstm.reshape(1, w), ce0, m0, *side_ops)


def _out_proj_kernel(merge, heads, *refs):
    if merge:
        (o1, o2, o3, l1, l2, l3, mo_ref, x_ref, wa_ref, wm_ref, gpost_ref, gpre_ref, wq_ref,
         x1_ref, qx_ref, att_s) = refs
        la, lb, lc = l1[...], l2[...], l3[...]
        mx = jnp.maximum(jnp.maximum(la, lb), lc)
        ea, eb, ec = jnp.exp(la - mx), jnp.exp(lb - mx), jnp.exp(lc - mx)
        tot = ea + eb + ec
        wa_, wb_, wc_ = ea / tot, eb / tot, ec / tot
        for h in range(heads):
            sl = slice(h * HEAD_DIM, (h + 1) * HEAD_DIM)
            mix = wa_[:, h:h + 1] * o1[h] + wb_[:, h:h + 1] * o2[h] + wc_[:, h:h + 1] * o3[h]
            att_s[:, sl] = mix.astype(BF16)
        att = att_s[...]
    else:
        att_ref, mo_ref, x_ref, wa_ref, wm_ref, gpost_ref, gpre_ref, wq_ref, x1_ref, qx_ref = refs
        att = att_ref[...]
    y = _dot(att, wa_ref[...]) + _dot(mo_ref[...], wm_ref[...])
    x1 = x_ref[...] + _rms(y, gpost_ref[...])
    x1_ref[...] = x1
    u = _rms(x1, gpre_ref[...]).astype(BF16)
    qx_ref[...] = _dot(u, wq_ref[...]).astype(qx_ref.dtype)


def _out_proj(att_parts, m_out, x, w_out, g_post, g_pre, w_xq, tm, heads):
    m, d = x.shape
    wm = m_out.shape[1]
    wa = w_out.shape[0] - wm
    assert wa == wm
    nq = w_xq.shape[1]
    merge = len(att_parts) == 6
    row = lambda i: (i, 0)
    const = lambda i: (0, 0)
    if merge:
        part_specs = ([pl.BlockSpec((heads, tm, HEAD_DIM), lambda i: (0, i, 0))] * 3
                      + [pl.BlockSpec((tm, LANES), row)] * 3)
    else:
        part_specs = [pl.BlockSpec((tm, wa), row)]
    return pl.pallas_call(
        functools.partial(_out_proj_kernel, merge, heads),
        out_shape=(jax.ShapeDtypeStruct((m, d), F32), jax.ShapeDtypeStruct((m, nq), BF16)),
        grid=(m // tm,),
        in_specs=part_specs + [pl.BlockSpec((tm, wm), row),
                               pl.BlockSpec((tm, d), row),
                               pl.BlockSpec((wa, d), const, pipeline_mode=pl.Buffered(1)),
                               pl.BlockSpec((wm, d), lambda i: (1, 0), pipeline_mode=pl.Buffered(1)),
                               pl.BlockSpec((1, d), const),
                               pl.BlockSpec((1, d), const),
                               pl.BlockSpec((d, nq), const, pipeline_mode=pl.Buffered(1))],
        out_specs=(pl.BlockSpec((tm, d), row), pl.BlockSpec((tm, nq), row)),
        scratch_shapes=[pltpu.VMEM((tm, wa), BF16)] if merge else [],
        compiler_params=_params(("parallel",)),
        name="out_proj_merge" if merge else "out_proj",
    )(*att_parts, m_out, x, w_out, w_out, g_post, g_pre, w_xq)


def _xattn_kernel(heads, q_ref, mk_ref, mv_ref, o_ref):
    for h in range(heads):
        sl = slice(h * HEAD_DIM, (h + 1) * HEAD_DIM)
        s = _dot_nt(q_ref[:, sl], mk_ref[:, sl].astype(BF16)) * ATT_SCALE
        m = jnp.max(s, axis=-1, keepdims=True)
        p = jnp.exp(s - m)
        l = jnp.sum(p, axis=-1, keepdims=True)
        o = _dot(p.astype(BF16), mv_ref[:, sl].astype(BF16))
        o_ref[:, sl] = (o / l).astype(o_ref.dtype)


def _xattn(qx, mem_k, mem_v, tm, heads):
    m, w = qx.shape
    n_mem = mem_k.shape[0]
    return pl.pallas_call(
        functools.partial(_xattn_kernel, heads),
        out_shape=jax.ShapeDtypeStruct((m, w), BF16),
        grid=(m // tm,),
        in_specs=[pl.BlockSpec((tm, w), lambda i: (i, 0)),
                  pl.BlockSpec((n_mem, w), lambda i: (0, 0)),
                  pl.BlockSpec((n_mem, w), lambda i: (0, 0))],
        out_specs=pl.BlockSpec((tm, w), lambda i: (i, 0)),
        compiler_params=_params(("parallel",)),
        name="xattn_prompt",
    )(qx, mem_k, mem_v)


def _proj_post_kernel(a_ref, x_ref, w_ref, g_ref, g_next_ref, y_ref, yn_ref):
    y = x_ref[...] + _rms(_dot(a_ref[...], w_ref[...]), g_ref[...])
    y_ref[...] = y
    yn_ref[...] = _rms(y, g_next_ref[...]).astype(BF16)


def _proj_post(a, x, w, g, g_next, tm):
    m, d = x.shape
    k = a.shape[1]
    row = lambda i: (i, 0)
    const = lambda i: (0, 0)
    return pl.pallas_call(
        _proj_post_kernel,
        out_shape=(jax.ShapeDtypeStruct((m, d), F32), jax.ShapeDtypeStruct((m, d), BF16)),
        grid=(m // tm,),
        in_specs=[pl.BlockSpec((tm, k), row),
                  pl.BlockSpec((tm, d), row),
                  pl.BlockSpec((k, d), const),
                  pl.BlockSpec((1, d), const),
                  pl.BlockSpec((1, d), const)],
        out_specs=(pl.BlockSpec((tm, d), row), pl.BlockSpec((tm, d), row)),
        compiler_params=_params(("parallel",)),
        name="proj_post",
    )(a, x, w, g, g_next)


def _mlp_kernel(n_side, x_ref, xn_ref, wu_ref, wd_ref, gpost_ref, *rest):
    if n_side:
        sq_ref, sk_ref, sv_ref, y_ref, so_ref, acc_s = rest
    else:
        y_ref, acc_s = rest
    j = pl.program_id(1)

    def step(first):
        hidden = jnp.square(jnp.maximum(_dot(xn_ref[...], wu_ref[...]), 0.0))
        update = _dot(hidden.astype(BF16), wd_ref[...])
        acc_s[...] = update if first else acc_s[...] + update
        for b in range(n_side):
            so_ref[b] = _single_query_attention(sq_ref[b], [(sk_ref[b], sv_ref[b], None)], None)

    pl.when(j == 0)(functools.partial(step, True))
    pl.when(j != 0)(functools.partial(step, False))

    @pl.when(j == pl.num_programs(1) - 1)
    def _():
        y_ref[...] = x_ref[...] + _rms(acc_s[...], gpost_ref[...])


def _mlp(x, xn, w_up, w_down, g_post, tm, tf, side=None):
    m, d = x.shape
    ff = w_up.shape[1]
    steps_j = ff // tf
    n_steps = (m // tm) * steps_j
    in_specs = [pl.BlockSpec((tm, d), lambda i, j: (i, 0)),
                pl.BlockSpec((tm, d), lambda i, j: (i, 0)),
                pl.BlockSpec((d, tf), lambda i, j: (0, j)),
                pl.BlockSpec((tf, d), lambda i, j: (j, 0)),
                pl.BlockSpec((1, d), lambda i, j: (0, 0))]
    out_shape = [jax.ShapeDtypeStruct((m, d), F32)]
    out_specs = [pl.BlockSpec((tm, d), lambda i, j: (i, 0))]
    n_side = 0
    if side is not None:
        sq, sk, sv = side
        nb, heads, hd = sq.shape
        n_side = nb // n_steps
        assert n_side * n_steps == nb
        tok = pl.BlockSpec((n_side, heads, hd), lambda i, j: (i * steps_j + j, 0, 0))
        mem = pl.BlockSpec((n_side,) + sk.shape[1:], lambda i, j: (i * steps_j + j, 0, 0, 0))
        in_specs += [tok, mem, mem]
        out_shape.append(jax.ShapeDtypeStruct((nb, heads, hd), F32))
        out_specs.append(tok)
    out = pl.pallas_call(
        functools.partial(_mlp_kernel, n_side),
        out_shape=tuple(out_shape),
        grid=(m // tm, steps_j),
        in_specs=in_specs,
        out_specs=tuple(out_specs),
        scratch_shapes=[pltpu.VMEM((tm, d), F32)],
        compiler_params=_params(("parallel", "arbitrary")),
        name="mlp_side" if side is not None else "mlp",
    )(x, xn, w_up, w_down, g_post, *(side or ()))
    return out if side is not None else out[0]


def _single_query_attention(q, groups, new):
    heads = q.shape[0]
    fold = groups[0][0].shape[1] // heads
    assert fold == 1 or new is None

    def per_slot(x):
        return [x[i * heads:(i + 1) * heads] for i in range(fold)]

    q = q * ATT_SCALE
    if fold > 1:
        q = jnp.concatenate([q] * fold, axis=0)
    scores = []
    for k3, _, b3 in groups:
        s = jnp.sum(k3 * q[None], axis=-1, keepdims=True)
        scores.append(s if b3 is None else s + b3)
    mx = jnp.max(scores[0], axis=0)
    for s in scores[1:]:
        mx = jnp.maximum(mx, jnp.max(s, axis=0))
    if new is not None:
        k_new, v_new, b_new, count = new
        s_new = jnp.sum(k_new * q, axis=-1, keepdims=True) + b_new
        mx = jnp.maximum(mx, s_new)
    if fold > 1:
        slots = per_slot(mx)
        shared = functools.reduce(jnp.maximum, slots)
        mx = jnp.concatenate([shared] * fold, axis=0)
    den = jnp.zeros(mx.shape, F32)
    acc = jnp.zeros(q.shape, F32)
    for s, (_, v3, _) in zip(scores, groups):
        p = jnp.exp(s - mx[None])
        den = den + jnp.sum(p, axis=0)
        acc = acc + jnp.sum(p * v3, axis=0)
    if new is not None:
        p_new = jnp.exp(s_new - mx) * count
        den = den + p_new
        acc = acc + p_new * v_new
    if fold > 1:
        den = functools.reduce(jnp.add, per_slot(den))
        acc = functools.reduce(jnp.add, per_slot(acc))
    return acc / den


N_WIN_ATTN_IN = 5 + 2 * len(DILATIONS)


def _sample_win_attn_init(rel_ref, bkt_ref, bias_s):
    for p in range(len(DILATIONS)):
        bkt = bkt_ref[p]
        acc = jnp.zeros(bkt.shape, F32)
        for k in range(REL_BUCKETS):
            acc = jnp.where(bkt == k, rel_ref[k][None], acc)
        bias_s[p] = acc


def _sample_win_attn_compute(tb, rel_ref, q_ref, kn_ref, vn_ref, k1, k4, k16, v1, v4, v16, o_ref, bias_s):
    for b in range(tb):
        groups = [(k[b], v[b], bias_s[p]) for p, (k, v) in enumerate(((k1, v1), (k4, v4), (k16, v16)))]
        new = (kn_ref[b], vn_ref[b], rel_ref[0], float(len(DILATIONS)))
        o_ref[b] = _single_query_attention(q_ref[b], groups, new)


def _sample_win_attn_operands(q, k_new, v_new, cache_k, cache_v, rel_bias, tb):
    nb, heads, hd = q.shape
    buf = cache_k.shape[1]
    r = SUB_WINDOW
    assert buf == r * DILATIONS[-1], "window buffer must hold exactly the widest dilated pattern"
    j = r - np.arange(r)
    bkt = np.stack([_rel_bucket(j * d) for d in DILATIONS]).astype(np.int32)
    bkt = np.broadcast_to(bkt[:, :, None, None], (len(DILATIONS), r, heads, 1))
    views, specs = [], []
    for cache in (cache_k, cache_v):
        for d in DILATIONS:
            views.append(cache.reshape(nb, buf // d, d, heads, hd))
            specs.append(pl.BlockSpec((tb, r, None, heads, hd), functools.partial(
                lambda i, blk: (i, blk, 0, 0, 0), blk=buf // d // r - 1)))
    tok = pl.BlockSpec((tb, heads, hd), lambda i: (i, 0, 0))
    operands = [rel_bias.reshape(REL_BUCKETS, heads, 1), jnp.asarray(bkt), q, k_new, v_new] + views
    in_specs = [pl.BlockSpec((REL_BUCKETS, heads, 1), lambda i: (0, 0, 0)),
                pl.BlockSpec((len(DILATIONS), r, heads, 1), lambda i: (0, 0, 0, 0)),
                tok, tok, tok] + specs
    scratch = [pltpu.VMEM((len(DILATIONS), r, heads, 1), F32)]
    return operands, in_specs, jax.ShapeDtypeStruct((nb, heads, hd), F32), tok, scratch


def _sample_mlstm_kernel(heads, tb, q_ref, k_ref, v_ref, mo_ref, li_ref, lf_ref, bi_ref, bf_ref,
                         gm_ref, c_ref, n_ref, m_ref, out_ref, c_out, n_out, m_out):
    qw = heads * MLSTM_DK
    hrow = lax.broadcasted_iota(jnp.int32, (LANES, qw), 0)
    hcol = lax.broadcasted_iota(jnp.int32, (LANES, qw), 1) // MLSTM_DK
    diag = hrow == hcol
    rows8 = slice(0, heads)
    for b in range(tb):
        q_bd = jnp.where(diag, q_ref[b:b + 1, :], 0.0) * QK_SCALE
        k_bd = jnp.where(diag, k_ref[b:b + 1, :], 0.0)
        c_b = c_ref[b]
        n_row = n_ref[b:b + 1, :]
        li = li_ref[b] + bi_ref[...]
        lf = _log_sigmoid(lf_ref[b] + bf_ref[...])
        m_prev = m_ref[b]
        qk = jnp.sum(q_bd * k_ref[b:b + 1, :], axis=-1, keepdims=True)[rows8]
        qn = jnp.sum(q_bd * n_row, axis=-1, keepdims=True)[rows8]
        q_c = _dot(q_bd.astype(BF16), c_b.astype(BF16))[rows8]
        inter = lf + m_prev
        m_t = jnp.maximum(inter, li)
        w_intra = jnp.exp(li - m_t)
        w_prev = jnp.exp(inter - m_t)
        sqk = qk * w_intra
        v_b = v_ref[b]
        num = sqk * v_b + w_prev * q_c
        den = sqk + w_prev * qn
        hh = num / jnp.maximum(jnp.abs(den), jnp.exp(-m_t))
        hn = hh * lax.rsqrt(jnp.mean(hh * hh, axis=-1, keepdims=True) + EPS)
        out_ref[b] = hn * gm_ref[...] * jax.nn.sigmoid(mo_ref[b])
        wv = jnp.concatenate([w_intra * v_b, jnp.zeros((LANES - heads, HEAD_DIM), F32)], axis=0)
        d_c = _dot_tn(k_bd.astype(BF16), wv.astype(BF16))
        for h in range(heads):
            hs = slice(h * MLSTM_DK, (h + 1) * MLSTM_DK)
            c_out[b, hs, :] = w_prev[h:h + 1, :] * c_b[hs, :] + d_c[hs, :]
        k_w = jnp.sum(k_bd[rows8] * w_intra, axis=0, keepdims=True)
        dec_row = jnp.sum(jnp.where(diag[rows8], w_prev, 0.0), axis=0, keepdims=True)
        n_out[b:b + 1, :] = dec_row * n_row + k_w
        m_out[b] = m_t


def _sample_mlstm(z6, gates, b_gate, g_mlstm, state_c, state_n, state_m, heads, tb):
    _, nb, w = z6.shape
    qw = heads * MLSTM_DK
    q = z6[3, :, :qw].astype(F32)
    k = z6[3, :, qw:].astype(F32)
    v = z6[4].astype(F32).reshape(nb, heads, HEAD_DIM)
    mo = z6[5].astype(F32).reshape(nb, heads, HEAD_DIM)
    li = gates[:, :heads].reshape(nb, heads, 1)
    lf = gates[:, heads:2 * heads].reshape(nb, heads, 1)
    row = lambda i: (i, 0)
    blk3 = lambda i: (i, 0, 0)
    const = lambda i: (0, 0)
    return pl.pallas_call(
        functools.partial(_sample_mlstm_kernel, heads, tb),
        out_shape=(jax.ShapeDtypeStruct((nb, heads, HEAD_DIM), F32),
                   jax.ShapeDtypeStruct((nb, qw, HEAD_DIM), F32),
                   jax.ShapeDtypeStruct((nb, qw), F32),
                   jax.ShapeDtypeStruct((nb, heads, 1), F32)),
        grid=(nb // tb,),
        in_specs=[pl.BlockSpec((tb, qw), row),
                  pl.BlockSpec((tb, qw), row),
                  pl.BlockSpec((tb, heads, HEAD_DIM), blk3),
                  pl.BlockSpec((tb, heads, HEAD_DIM), blk3),
                  pl.BlockSpec((tb, heads, 1), blk3),
                  pl.BlockSpec((tb, heads, 1), blk3),
                  pl.BlockSpec((heads, 1), const),
                  pl.BlockSpec((heads, 1), const),
                  pl.BlockSpec((heads, HEAD_DIM), const),
                  pl.BlockSpec((tb, qw, HEAD_DIM), blk3),
                  pl.BlockSpec((tb, qw), row),
                  pl.BlockSpec((tb, heads, 1), blk3)],
        out_specs=(pl.BlockSpec((tb, heads, HEAD_DIM), blk3),
                   pl.BlockSpec((tb, qw, HEAD_DIM), blk3),
                   pl.BlockSpec((tb, qw), row),
                   pl.BlockSpec((tb, heads, 1), blk3)),
        compiler_params=_params(("parallel",)),
        name="sample_mlstm",
    )(q, k, v, mo, li, lf, b_gate[:heads].reshape(heads, 1), b_gate[heads:].reshape(heads, 1),
      g_mlstm.reshape(heads, HEAD_DIM), state_c.reshape(nb, qw, HEAD_DIM), state_n.reshape(nb, qw),
      state_m.reshape(nb, heads, 1))


ROW_TILE = 512
MLP_FF_TILE = 1024
SAMPLE_STATE_TILE = 8


def _tile(m, cap):
    return min(m, cap)


def kernel(x_prompt, x_sample, mem_prompt, cache_win_k, cache_win_v, state_mlstm_C, state_mlstm_n,
           state_mlstm_m, cache_mem_k, cache_mem_v, rel_bias, g_pre_mix, g_post_mix, w_in, b_gate,
           g_mlstm, w_out, g_pre_xatt, g_post_xatt, g_mem, w_xq, w_xkv, w_xo, g_pre_mlp, g_post_mlp,
           w_up, w_down):
    depth = w_in.shape[0]
    assert depth == 1 and x_prompt.shape[0] == 1 and x_sample.shape[1] == 1
    _, seq, d_model = x_prompt.shape
    n_dec = x_sample.shape[0]
    att_heads = cache_win_k.shape[3]
    ml_heads = state_mlstm_C.shape[2]
    x_heads = cache_mem_k.shape[3]
    att_w = att_heads * HEAD_DIM
    ml_w = ml_heads * HEAD_DIM
    x_w = x_heads * HEAD_DIM
    n_mem = mem_prompt.shape[1]
    keep = cache_win_k.shape[2]
    n_main = 3 * att_w + 2 * ml_heads * MLSTM_DK + 2 * ml_w
    assert att_w == ml_w == 1024 and n_main % 1024 == 0 and seq >= keep

    def row(g):
        return g[0].reshape(1, -1)

    assert n_main == 6 * 1024
    w_main = w_in[0].astype(BF16)
    wg = w_in[0, :, n_main:].astype(BF16)
    w_out_b = w_out[0].astype(BF16)
    wq = w_xq[0].astype(BF16)
    wkv = w_xkv[0].astype(BF16)
    wo = w_xo[0].astype(BF16)
    wu = w_up[0].astype(BF16)
    wd = w_down[0].astype(BF16)
    rel_flat = rel_bias.reshape(-1)

    tm = _tile(seq, ROW_TILE)

    def mlp(x2, x2n, rows, side=None):
        return _mlp(x2, x2n, wu, wd, row(g_post_mlp), rows, MLP_FF_TILE, side)

    def mem_out_proj(o, x1, rows):
        return _proj_post(o, x1, wo, row(g_post_xatt), row(g_pre_mlp), rows)

    xs = x_sample[:, 0]
    z6s, kv32s, gates_s, _ = _in_proj(xs, row(g_pre_mix), w_main, wg, n_dec, False, n_dec)
    per_head = (n_dec, att_heads, HEAD_DIM)
    win_attn_args = (z6s[0].astype(F32).reshape(per_head), kv32s[0].reshape(per_head),
                     kv32s[1].reshape(per_head), cache_win_k[0], cache_win_v[0], rel_bias)

    xp = x_prompt[0]
    mem_kv = _norm_matmul(mem_prompt[0], row(g_mem), wkv)
    mk_p, mv_p = mem_kv[:, :x_w], mem_kv[:, x_w:]
    z6, kv32, _, gates_t, k_t, *zd = _in_proj(xp, row(g_pre_mix), w_main, wg, tm, True, keep)
    zd = [z6.reshape(z6.shape[0], 1, seq, att_w)] + zd
    parts = [_dil_attn(z, rel_flat, d, att_heads) for z, d in zip(zd, DILATIONS)]
    m_out, ce_p, m_p, att_s = _mlstm_prompt(z6, k_t, gates_t, b_gate[0], g_mlstm[0], ml_heads, win_attn_args)
    c_p, n_p = ce_p[:, :HEAD_DIM], ce_p[:, HEAD_DIM]
    x1, qx = _out_proj(tuple(p[0] for p in parts) + tuple(p[1] for p in parts), m_out, xp, w_out_b,
                       row(g_post_mix), row(g_pre_xatt), wq, tm, att_heads)
    x2, x2n = mem_out_proj(_xattn(qx, mk_p, mv_p, tm, x_heads), x1, tm)

    mo_s, c_s, n_s, m_s = _sample_mlstm(z6s, gates_s, b_gate[0], g_mlstm[0], state_mlstm_C[0],
                                        state_mlstm_n[0], state_mlstm_m[0], ml_heads, SAMPLE_STATE_TILE)
    x1s, qxs = _out_proj((att_s.reshape(n_dec, att_w).astype(BF16),), mo_s.reshape(n_dec, ml_w).astype(BF16),
                         xs, w_out_b, row(g_post_mix), row(g_pre_xatt), wq, n_dec, att_heads)

    q3 = qxs.astype(F32).reshape(n_dec, x_heads, HEAD_DIM)
    mem_fold = (n_dec, n_mem // 2, 2 * x_heads, HEAD_DIM)
    y_prompt, o_mem = mlp(x2, x2n, tm, (q3, cache_mem_k[0].reshape(mem_fold), cache_mem_v[0].reshape(mem_fold)))
    x2s, x2sn = mem_out_proj(o_mem.reshape(n_dec, x_w).astype(BF16), x1s, n_dec)
    y_sample = mlp(x2s, x2sn, n_dec)

    dk = MLSTM_DK
    return (y_prompt[None],
            y_sample[:, None],
            kv32[0].reshape(1, 1, keep, att_heads, HEAD_DIM),
            kv32[1].reshape(1, 1, keep, att_heads, HEAD_DIM),
            c_p.reshape(1, 1, ml_heads, dk, HEAD_DIM),
            n_p.reshape(1, 1, ml_heads, dk),
            m_p[:, 0].reshape(1, 1, ml_heads),
            mk_p.reshape(1, 1, n_mem, x_heads, HEAD_DIM),
            mv_p.reshape(1, 1, n_mem, x_heads, HEAD_DIM),
            kv32s[0].reshape(1, n_dec, 1, att_heads, HEAD_DIM),
            kv32s[1].reshape(1, n_dec, 1, att_heads, HEAD_DIM),
            c_s.reshape(1, n_dec, ml_heads, dk, HEAD_DIM),
            n_s.reshape(1, n_dec, ml_heads, dk),
            m_s.reshape(1, n_dec, ml_heads))
```
